```python
import math
import jax, jax.numpy as jnp
from jax import lax
import numpy as np

D_MODEL = 1024
BATCH = 16
SEQ = 2048
DEPTH = 1

HEAD_DIM = 64
A_Q_HEADS = 8
A_KV_HEADS = 2
B_DILATION_GROUPS = ((128, 1), (512, 4), (2048, 16))
B_HEADS_PER_GROUP = 4
B_HEADS = B_HEADS_PER_GROUP * len(B_DILATION_GROUPS)
GRID_W = 64
A_ROPE_THETA = 10000.0
PARTIAL_ROPE_THETA = 500000.0
PARTIAL_ROPE_DIMS = HEAD_DIM // 4
N_EXPERTS = 16
EC_CAPACITY_FACTOR = 2
D_FF_EXPERT = 2 * D_MODEL
Q_BLOCK = 128
LN_EPS = 1e-5
QK_NORM_EPS = 1e-6
MASK_VALUE = -1e30
DEEPNORM_ALPHA = (2.0 * DEPTH) ** 0.25
DEEPNORM_BETA = (8.0 * DEPTH) ** -0.25

A_Q_W = A_Q_HEADS * HEAD_DIM
A_KV_W = A_KV_HEADS * HEAD_DIM
B_W = B_HEADS * HEAD_DIM
B_OUT_W = B_HEADS_PER_GROUP * HEAD_DIM
IN_WIDTHS = (A_Q_W, A_KV_W, A_KV_W, B_W, B_W, B_W, D_MODEL, D_MODEL)
IN_TOTAL = sum(IN_WIDTHS)

kernel_name = "hybrid_gqa_dilated_ec_moe_encoder"


def layer_norm(x, g, b):
    xf = x.astype(jnp.float32)
    mu = jnp.mean(xf, axis=-1, keepdims=True)
    var = jnp.mean(jnp.square(xf - mu), axis=-1, keepdims=True)
    return ((xf - mu) * lax.rsqrt(var + LN_EPS) * g.astype(jnp.float32) + b.astype(jnp.float32)).astype(x.dtype)


def rms_norm(x, g):
    xf = x.astype(jnp.float32)
    y = xf * lax.rsqrt(jnp.mean(xf * xf, axis=-1, keepdims=True) + QK_NORM_EPS)
    return (y * g.astype(jnp.float32)).astype(x.dtype)


def rope_angles(pos, dim, theta):
    inv_freq = theta ** (-jnp.arange(0, dim, 2, dtype=jnp.float32) / dim)
    return pos.astype(jnp.float32)[:, None] * inv_freq[None, :]


def apply_rotary(x, ang):
    half = x.shape[-1] // 2
    cos = jnp.cos(ang)[None, :, None, :]
    sin = jnp.sin(ang)[None, :, None, :]
    x1 = x[..., :half].astype(jnp.float32)
    x2 = x[..., half:].astype(jnp.float32)
    return jnp.concatenate([x1 * cos - x2 * sin, x2 * cos + x1 * sin], axis=-1).astype(x.dtype)


def gqa_blocked(q, k, v):
    B, S, Hq, dh = q.shape
    Hkv = k.shape[2]
    G = Hq // Hkv
    nblk = S // Q_BLOCK
    scale = dh ** -0.5
    qb = q.reshape(B, nblk, Q_BLOCK, Hkv, G, dh).transpose(1, 0, 2, 3, 4, 5)

    def one_block(qblk):
        s = jnp.einsum('bqkgd,bskd->bkgqs', qblk, k, preferred_element_type=jnp.float32) * scale
        p = jax.nn.softmax(s, axis=-1).astype(v.dtype)
        return jnp.einsum('bkgqs,bskd->bqkgd', p, v)

    o = lax.map(one_block, qb)
    return o.transpose(1, 0, 2, 3, 4, 5).reshape(B, S, Hq * dh)


def dilated_band_attention(q, k, v, window, dilation):
    B, S, H, dh = q.shape
    half = window // (2 * dilation)
    n = S // dilation
    Qb = half
    nb = -(-n // Qb)
    pad_end = nb * Qb - n
    scale = dh ** -0.5

    def split(t):
        return t.reshape(B, n, dilation, H, dh).transpose(0, 2, 1, 3, 4)

    qs = jnp.pad(split(q), ((0, 0), (0, 0), (0, pad_end), (0, 0), (0, 0)))
    qs = qs.reshape(B, dilation, nb, Qb, H, dh)

    def key_blocks(t):
        tp = jnp.pad(split(t), ((0, 0), (0, 0), (Qb, pad_end + Qb), (0, 0), (0, 0)))
        tp = tp.reshape(B, dilation, nb + 2, Qb, H, dh)
        return jnp.concatenate([tp[:, :, :-2], tp[:, :, 1:-1], tp[:, :, 2:]], axis=3)

    ks = key_blocks(k)
    vs = key_blocks(v)
    a = jnp.arange(Qb)
    c = jnp.arange(3 * Qb)
    blk = jnp.arange(nb)
    rel = c[None, :] - Qb - a[:, None]
    key_idx = blk[:, None] * Qb + c[None, :] - Qb
    valid = (key_idx >= 0) & (key_idx < n)
    mask = (jnp.abs(rel) <= half)[None, :, :] & valid[:, None, :]

    s = jnp.einsum('brnqhd,brnkhd->brnhqk', qs, ks, preferred_element_type=jnp.float32) * scale
    s = jnp.where(mask[None, None, :, None, :, :], s, MASK_VALUE)
    lse = jax.nn.logsumexp(s, axis=-1)
    p = jnp.exp(s - lse[..., None]).astype(v.dtype)
    o = jnp.einsum('brnhqk,brnkhd->brnqhd', p, vs)
    o = o.reshape(B, dilation, nb * Qb, H, dh)[:, :, :n]
    o = o.transpose(0, 2, 1, 3, 4).reshape(B, S, H, dh)
    lse = lse.transpose(0, 1, 2, 4, 3).reshape(B, dilation, nb * Qb, H)[:, :, :n]
    lse = lse.transpose(0, 2, 1, 3).reshape(B, S, H)
    return o, lse


def hybrid_mixer(h, w_in, b_gate, qn_g, kn_g, w_branch_a, w_branch_b, w_out, ang_row, ang_col, ang_t):
    B, S, _ = h.shape
    proj = jnp.einsum('bsd,de->bse', h, w_in)
    offsets = []
    acc = 0
    for w in IN_WIDTHS[:-1]:
        acc += w
        offsets.append(acc)
    qa, ka, va, qb, kb, vb, ga, gb = jnp.split(proj, offsets, axis=-1)

    half_rot = HEAD_DIM // 2
    qa = rms_norm(qa.reshape(B, S, A_Q_HEADS, HEAD_DIM), qn_g)
    ka = rms_norm(ka.reshape(B, S, A_KV_HEADS, HEAD_DIM), kn_g)
    qa = jnp.concatenate([apply_rotary(qa[..., :half_rot], ang_row), apply_rotary(qa[..., half_rot:], ang_col)], axis=-1)
    ka = jnp.concatenate([apply_rotary(ka[..., :half_rot], ang_row), apply_rotary(ka[..., half_rot:], ang_col)], axis=-1)
    va = va.reshape(B, S, A_KV_HEADS, HEAD_DIM)
    ya = jnp.einsum('bse,ed->bsd', gqa_blocked(qa, ka, va), w_branch_a)

    def partial_rope(t):
        t = t.reshape(B, S, B_HEADS, HEAD_DIM)
        return jnp.concatenate([apply_rotary(t[..., :PARTIAL_ROPE_DIMS], ang_t), t[..., PARTIAL_ROPE_DIMS:]], axis=-1)
    qb = partial_rope(qb)
    kb = partial_rope(kb)
    vb = vb.reshape(B, S, B_HEADS, HEAD_DIM)
    outs = []
    lses = []
    for gi, (window, dilation) in enumerate(B_DILATION_GROUPS):
        sl = slice(gi * B_HEADS_PER_GROUP, (gi + 1) * B_HEADS_PER_GROUP)
        o, lse = dilated_band_attention(qb[:, :, sl], kb[:, :, sl], vb[:, :, sl], window, dilation)
        outs.append(o)
        lses.append(lse)
    wts = jax.nn.softmax(jnp.stack(lses, axis=0), axis=0)
    ob = jnp.einsum('gbsh,gbshd->bshd', wts.astype(vb.dtype), jnp.stack(outs, axis=0))
    yb = jnp.einsum('bse,ed->bsd', ob.reshape(B, S, B_OUT_W), w_branch_b)

    gates = jax.nn.sigmoid(jnp.concatenate([ga, gb], axis=-1).astype(jnp.float32) + b_gate.astype(jnp.float32))
    gates = gates.astype(ya.dtype)
    merged = gates[..., :D_MODEL] * ya + gates[..., D_MODEL:] * yb
    return jnp.einsum('bsd,de->bse', merged, w_out)


def expert_choice_ffn(h, w_router, w_gate, w_up, w_down):
    B, S, D = h.shape
    cap = EC_CAPACITY_FACTOR * S // N_EXPERTS
    logits = jnp.einsum('bsd,de->bse', h, w_router, preferred_element_type=jnp.float32)
    aff = jax.nn.softmax(logits, axis=-1)
    gate_vals, tok_idx = lax.top_k(aff.transpose(0, 2, 1), cap)
    xe = jax.vmap(lambda hb, ib: hb[ib])(h, tok_idx)
    g = jnp.einsum('becd,edf->becf', xe, w_gate)
    u = jnp.einsum('becd,edf->becf', xe, w_up)
    ye = jnp.einsum('becf,efd->becd', jax.nn.silu(g) * u, w_down)
    ye = ye * gate_vals[..., None].astype(ye.dtype)

    def combine(ib, yb):
        return jnp.zeros((S, D), yb.dtype).at[ib.reshape(-1)].add(yb.reshape(-1, D))

    return jax.vmap(combine)(tok_idx, ye)


def setup_inputs(seed: int = 0) -> dict:
    key = jax.random.key(seed)
    ks = jax.random.split(key, 20)
    L, D, E, F = DEPTH, D_MODEL, N_EXPERTS, D_FF_EXPERT
    nrm = lambda k, shape: jax.random.normal(k, shape, jnp.float32)
    col_scale = jnp.concatenate([
        jnp.full((w,), DEEPNORM_BETA if idx in (2, 5) else 1.0, jnp.float32)
        for idx, w in enumerate(IN_WIDTHS)])
    return {
        "x": nrm(ks[0], (BATCH, SEQ, D)),
        "ln0_g": 1.0 + 0.02 * nrm(ks[1], (D,)),
        "ln0_b": 0.02 * nrm(ks[2], (D,)),
        "w_in": nrm(ks[3], (L, D, IN_TOTAL)) * (D ** -0.5) * col_scale,
        "b_gate": 0.1 * nrm(ks[4], (L, 2 * D)),
        "qn_g": 1.0 + 0.02 * nrm(ks[5], (L, HEAD_DIM)),
        "kn_g": 1.0 + 0.02 * nrm(ks[6], (L, HEAD_DIM)),
        "w_branch_a": nrm(ks[7], (L, A_Q_W, D)) * (A_Q_W ** -0.5),
        "w_branch_b": nrm(ks[8], (L, B_OUT_W, D)) * (B_OUT_W ** -0.5),
        "w_out": nrm(ks[9], (L, D, D)) * (D ** -0.5) * DEEPNORM_BETA,
        "ln1_g": 1.0 + 0.02 * nrm(ks[10], (L, D)),
        "ln1_b": 0.02 * nrm(ks[11], (L, D)),
        "w_router": nrm(ks[12], (L, D, E)) * (D ** -0.5),
        "w_gate_e": nrm(ks[13], (L, E, D, F)) * (D ** -0.5),
        "w_up_e": nrm(ks[14], (L, E, D, F)) * (D ** -0.5),
        "w_down_e": nrm(ks[15], (L, E, F, D)) * (F ** -0.5) * DEEPNORM_BETA,
        "ln2_g": 1.0 + 0.02 * nrm(ks[16], (L, D)),
        "ln2_b": 0.02 * nrm(ks[17], (L, D)),
    }


def reference(x, ln0_g, ln0_b, w_in, b_gate, qn_g, kn_g, w_branch_a, w_branch_b, w_out,
              ln1_g, ln1_b, w_router, w_gate_e, w_up_e, w_down_e, ln2_g, ln2_b):
    S = x.shape[1]
    rows = S // GRID_W
    t = jnp.arange(S)
    row_idx = jnp.repeat(jnp.arange(rows), GRID_W)
    col_idx = jnp.tile(jnp.arange(GRID_W), rows)
    ang_row = rope_angles(row_idx, HEAD_DIM // 2, A_ROPE_THETA)
    ang_col = rope_angles(col_idx, HEAD_DIM // 2, A_ROPE_THETA)
    ang_t = rope_angles(t, PARTIAL_ROPE_DIMS, PARTIAL_ROPE_THETA)

    h = layer_norm(x, ln0_g, ln0_b)
    for l in range(DEPTH):
        mix = hybrid_mixer(h, w_in[l], b_gate[l], qn_g[l], kn_g[l], w_branch_a[l], w_branch_b[l],
                           w_out[l], ang_row, ang_col, ang_t)
        h = layer_norm(DEEPNORM_ALPHA * h + mix, ln1_g[l], ln1_b[l])
        ffn = expert_choice_ffn(h, w_router[l], w_gate_e[l], w_up_e[l], w_down_e[l])
        h = layer_norm(DEEPNORM_ALPHA * h + ffn, ln2_g[l], ln2_b[l])
    return h
```

```python
import functools

import jax
import jax.numpy as jnp
from jax import lax
from jax.experimental import pallas as pl
from jax.experimental.pallas import tpu as pltpu

F32 = jnp.float32
BF16 = jnp.bfloat16

D_MODEL = 1024
HEAD_DIM = 64
A_Q_HEADS = 8
A_KV_HEADS = 2
A_GROUP = A_Q_HEADS // A_KV_HEADS
B_GROUPS = ((128, 1), (512, 4), (2048, 16))
B_HEADS_PER_GROUP = 4
GRID_W = 64
A_ROPE_THETA = 10000.0
B_ROPE_THETA = 500000.0
B_ROPE_DIMS = HEAD_DIM // 4
N_EXPERTS = 16
EC_CAPACITY_FACTOR = 2
D_FF = 2 * D_MODEL
LN_EPS = 1e-5
QK_NORM_EPS = 1e-6
MASK_VALUE = -1e30
DEPTH = 1
DEEPNORM_ALPHA = (2.0 * DEPTH) ** 0.25
SM_SCALE = HEAD_DIM ** -0.5

A_Q_W = A_Q_HEADS * HEAD_DIM
A_KV_W = A_KV_HEADS * HEAD_DIM
B_GROUP_W = B_HEADS_PER_GROUP * HEAD_DIM
B_W = B_GROUP_W * len(B_GROUPS)
QKV_W = A_Q_W + 2 * A_KV_W + 3 * B_W
BAND_HALF = 64

LANES = 128
VMEM_LIMIT = 56 * 1024 * 1024

NT_DIMS = (((1,), (1,)), ((), ()))


def _params(n_axes):
    return pltpu.CompilerParams(dimension_semantics=("arbitrary",) * n_axes, vmem_limit_bytes=VMEM_LIMIT)


def _layer_norm(x, g, b):
    mu = jnp.mean(x, axis=-1, keepdims=True)
    xc = x - mu
    var = jnp.mean(xc * xc, axis=-1, keepdims=True)
    return xc * lax.rsqrt(var + LN_EPS) * g + b


def _split_bf16(x):
    hi = x.astype(BF16)
    lo = (x - hi.astype(F32)).astype(BF16)
    return hi, lo


def _swap_halves(y, lane, period, half):
    fwd = pltpu.roll(y, LANES - half, 1)
    bwd = pltpu.roll(y, half, 1)
    return jnp.where((lane % period) < half, fwd, bwd)


def _inproj_kernel(x_ref, g0_ref, b0_ref, w_ref, qng_ref, kng_ref, cosa_ref, sina_ref, cosb_ref, sinb_ref,
                   qa_ref, ka_ref, va_ref, qb0_ref, qb1_ref, qb2_ref, kb0_ref, kb1_ref, kb2_ref,
                   vb0_ref, vb1_ref, vb2_ref):
    h = _layer_norm(x_ref[...], g0_ref[...], b0_ref[...]).astype(BF16)
    lane = lax.broadcasted_iota(jnp.int32, (1, LANES), 1)
    r = lax.broadcasted_iota(jnp.int32, (LANES, LANES), 0)
    c = lax.broadcasted_iota(jnp.int32, (LANES, LANES), 1)
    head_sum = jnp.where((r // HEAD_DIM) == (c // HEAD_DIM), 1.0, 0.0).astype(BF16)
    cosa, sina = cosa_ref[...], sina_ref[...]
    cosb, sinb = cosb_ref[...], sinb_ref[...]

    def proj(off):
        return jnp.dot(h, w_ref[:, off:off + LANES], preferred_element_type=F32)

    def norm_rope_a(off, gain, scale):
        y = proj(off)
        hi, lo = _split_bf16(y * y)
        ss = (jnp.dot(hi, head_sum, preferred_element_type=F32)
              + jnp.dot(lo, head_sum, preferred_element_type=F32))
        y = y * lax.rsqrt(ss * (1.0 / HEAD_DIM) + QK_NORM_EPS) * gain
        y = y * cosa + _swap_halves(y, lane, HEAD_DIM // 2, HEAD_DIM // 4) * sina
        return (y * scale).astype(BF16)

    def rope_b(off, scale):
        y = proj(off)
        y = y * cosb + _swap_halves(y, lane, HEAD_DIM, B_ROPE_DIMS // 2) * sinb
        return y * scale

    for j in range(A_Q_W // LANES):
        qa_ref[:, j * LANES:(j + 1) * LANES] = norm_rope_a(j * LANES, qng_ref[...], SM_SCALE)
    ka_ref[...] = norm_rope_a(A_Q_W, kng_ref[...], 1.0)
    va_ref[...] = proj(A_Q_W + A_KV_W).astype(BF16)
    base = A_Q_W + 2 * A_KV_W
    for gi, (q_ref, k_ref, v_ref) in enumerate(((qb0_ref, kb0_ref, vb0_ref), (qb1_ref, kb1_ref, vb1_ref),
                                                (qb2_ref, kb2_ref, vb2_ref))):
        for j in range(B_GROUP_W // LANES):
            off = gi * B_GROUP_W + j * LANES
            q_ref[j] = rope_b(base + off, SM_SCALE)
            k_ref[j] = rope_b(base + B_W + off, 1.0)
            v_ref[j] = proj(base + 2 * B_W + off)


def _rope_tables(seq):
    lane = jnp.arange(LANES)
    d = lane % HEAD_DIM
    t = jnp.arange(seq)
    half_rot = HEAD_DIM // 2
    inv_a = A_ROPE_THETA ** (-jnp.arange(0, half_rot, 2, dtype=F32) / half_rot)
    j = d % half_rot
    pos_row = (t // GRID_W).astype(F32)
    pos_col = (t % GRID_W).astype(F32)
    freq_a = inv_a[j % (half_rot // 2)]
    ang_a = jnp.where((d < half_rot)[None, :], pos_row[:, None], pos_col[:, None]) * freq_a[None, :]
    sign_a = jnp.where(j < half_rot // 2, -1.0, 1.0).astype(F32)
    cosa = jnp.cos(ang_a)
    sina = jnp.sin(ang_a) * sign_a[None, :]
    inv_b = B_ROPE_THETA ** (-jnp.arange(0, B_ROPE_DIMS, 2, dtype=F32) / B_ROPE_DIMS)
    freq_b = inv_b[d % (B_ROPE_DIMS // 2)]
    ang_b = t.astype(F32)[:, None] * freq_b[None, :]
    rot = (d < B_ROPE_DIMS)[None, :]
    sign_b = jnp.where(d < B_ROPE_DIMS // 2, -1.0, 1.0).astype(F32)
    cosb = jnp.where(rot, jnp.cos(ang_b), 1.0)
    sinb = jnp.where(rot, jnp.sin(ang_b) * sign_b[None, :], 0.0)
    return cosa, sina, cosb, sinb


def _in_projection(x2, ln0_g, ln0_b, w_qkv, qn_g, kn_g, seq, tm):
    m = x2.shape[0]
    cosa, sina, cosb, sinb = _rope_tables(seq)
    tiles_per_seq = seq // tm
    row = lambda i: (i, 0)
    const = lambda i: (0, 0)
    table = lambda i: (i % tiles_per_seq, 0)
    widths = (A_Q_W, A_KV_W, A_KV_W) + (B_GROUP_W,) * 9
    gain_tile = lambda g: jnp.tile(g.reshape(1, HEAD_DIM), (1, LANES // HEAD_DIM))
    return pl.pallas_call(
        _inproj_kernel,
        grid=(m // tm,),
        in_specs=[
            pl.BlockSpec((tm, D_MODEL), row),
            pl.BlockSpec((1, D_MODEL), const),
            pl.BlockSpec((1, D_MODEL), const),
            pl.BlockSpec((D_MODEL, QKV_W), const),
            pl.BlockSpec((1, LANES), const),
            pl.BlockSpec((1, LANES), const),
            pl.BlockSpec((tm, LANES), table),
            pl.BlockSpec((tm, LANES), table),
            pl.BlockSpec((tm, LANES), table),
            pl.BlockSpec((tm, LANES), table),
        ],
        out_specs=([pl.BlockSpec((tm, w), row) for w in widths[:3]]
                   + [pl.BlockSpec((B_GROUP_W // LANES, tm, LANES), lambda i: (0, i, 0))] * 9),
        out_shape=([jax.ShapeDtypeStruct((m, w), BF16) for w in widths[:3]]
                   + [jax.ShapeDtypeStruct((B_GROUP_W // LANES, m, LANES), F32)] * 9),
        compiler_params=_params(1),
        name="in_projection",
    )(x2, ln0_g.reshape(1, D_MODEL), ln0_b.reshape(1, D_MODEL), w_qkv, gain_tile(qn_g), gain_tile(kn_g),
      cosa, sina, cosb, sinb)


def _attn_a_kernel(q_ref, k_ref, v_ref, o_ref):
    outs = []
    for kh in range(A_KV_HEADS):
        k = k_ref[0, :, kh * HEAD_DIM:(kh + 1) * HEAD_DIM]
        v = v_ref[0, :, kh * HEAD_DIM:(kh + 1) * HEAD_DIM]
        for g in range(A_GROUP):
            hq = kh * A_GROUP + g
            q = q_ref[0, :, hq * HEAD_DIM:(hq + 1) * HEAD_DIM]
            s = lax.dot_general(q, k, NT_DIMS, preferred_element_type=F32)
            p = jnp.exp(s - jnp.max(s, axis=-1, keepdims=True))
            denom = jnp.sum(p, axis=-1, keepdims=True)
            o = jnp.dot(p.astype(BF16), v, preferred_element_type=F32)
            outs.append(o / denom)
    o_ref[0] = jnp.concatenate(outs, axis=1).astype(BF16)


def _attention_a(qa, ka, va, tq):
    b, s, _ = qa.shape
    return pl.pallas_call(
        _attn_a_kernel,
        grid=(b, s // tq),
        in_specs=[
            pl.BlockSpec((1, tq, A_Q_W), lambda i, j: (i, j, 0)),
            pl.BlockSpec((1, s, A_KV_W), lambda i, j: (i, 0, 0)),
            pl.BlockSpec((1, s, A_KV_W), lambda i, j: (i, 0, 0)),
        ],
        out_specs=pl.BlockSpec((1, tq, A_Q_W), lambda i, j: (i, j, 0)),
        out_shape=jax.ShapeDtypeStruct((b, s, A_Q_W), BF16),
        compiler_params=_params(2),
        name="attention_a",
    )(qa, ka, va)


BAND_QB = 128
BAND_UNITS_PER_STEP = 4
BAND_STEPS = 4


def _dilated_kernel(q_ref, k_ref, v_ref, o_ref, lse_ref, *, n, dilation):
    win = min(2 * BAND_QB, n)
    n_blk = n // BAND_QB
    rows = B_HEADS_PER_GROUP * BAND_QB
    lane_head = lax.broadcasted_iota(jnp.int32, (1, B_GROUP_W), 1) // HEAD_DIM
    rel = (lax.broadcasted_iota(jnp.int32, (rows, win), 0) % BAND_QB
           - lax.broadcasted_iota(jnp.int32, (rows, win), 1))

    planes = B_GROUP_W // LANES

    def class_rows(r, start, size):
        if dilation == 1:
            return pl.ds(pl.multiple_of(start, BAND_HALF), size)
        return pl.ds(r + start * dilation, size, stride=dilation)

    def load(ref, rows_):
        return jnp.concatenate([ref[j, rows_, :] for j in range(planes)], axis=1)

    for i in range(BAND_UNITS_PER_STEP):
        unit = pl.program_id(1) * BAND_UNITS_PER_STEP + i
        r = unit // n_blk
        qs = (unit % n_blk) * BAND_QB
        ks = jnp.clip(qs - BAND_HALF, 0, n - win)
        q = load(q_ref, class_rows(r, qs, BAND_QB))
        k = load(k_ref, class_rows(r, ks, win)).astype(BF16)
        v = load(v_ref, class_rows(r, ks, win)).astype(BF16)
        q_stack = jnp.concatenate([jnp.where(lane_head == h, q, 0.0) for h in range(B_HEADS_PER_GROUP)], axis=0)
        s = lax.dot_general(q_stack.astype(BF16), k, NT_DIMS, preferred_element_type=F32)
        s = jnp.where(jnp.abs(rel + (qs - ks)) <= BAND_HALF, s, MASK_VALUE)
        mx = jnp.max(s, axis=-1, keepdims=True)
        p = jnp.exp(s - mx)
        denom = jnp.sum(p, axis=-1, keepdims=True)
        pv = jnp.dot(p.astype(BF16), v, preferred_element_type=F32) / denom
        lse = mx + jnp.log(denom)
        o = jnp.zeros((BAND_QB, B_GROUP_W), F32)
        l = jnp.zeros((BAND_QB, B_GROUP_W), F32)
        for h in range(B_HEADS_PER_GROUP):
            blk = slice(h * BAND_QB, (h + 1) * BAND_QB)
            o = jnp.where(lane_head == h, pv[blk], o)
            l = jnp.where(lane_head == h, lse[blk], l)
        for j in range(planes):
            o_ref[j, class_rows(r, qs, BAND_QB), :] = o[:, j * LANES:(j + 1) * LANES]
            lse_ref[j, class_rows(r, qs, BAND_QB), :] = l[:, j * LANES:(j + 1) * LANES]


def _dilated_attention(q, k, v, seq, dilation):
    planes, tokens, _ = q.shape
    n = seq // dilation
    assert dilation * (n // BAND_QB) == BAND_UNITS_PER_STEP * BAND_STEPS
    blk = pl.BlockSpec((planes, seq, LANES), lambda i, j: (0, i, 0))
    return pl.pallas_call(
        functools.partial(_dilated_kernel, n=n, dilation=dilation),
        grid=(tokens // seq, BAND_STEPS),
        in_specs=[blk, blk, blk],
        out_specs=[blk, blk],
        out_shape=[jax.ShapeDtypeStruct(q.shape, F32)] * 2,
        compiler_params=_params(2),
        name=f"dilated_attention_d{dilation}",
    )(q, k, v)


def _merge_kernel(x_ref, g0_ref, b0_ref, wg_ref, bg_ref, oa_ref, o0_ref, o1_ref, o2_ref, l0_ref, l1_ref, l2_ref,
                  wa_ref, wb_ref, wo_ref, g1_ref, b1_ref, wr_hi_ref, wr_lo_ref,
                  h1_ref, h1b_ref, logit_ref):
    h0 = _layer_norm(x_ref[...], g0_ref[...], b0_ref[...])
    gates = jax.nn.sigmoid(jnp.dot(h0.astype(BF16), wg_ref[...], preferred_element_type=F32) + bg_ref[...])
    ya = jnp.dot(oa_ref[...], wa_ref[...], preferred_element_type=F32)
    planes = lambda ref: jnp.concatenate([ref[j] for j in range(ref.shape[0])], axis=1)
    l0, l1, l2 = planes(l0_ref), planes(l1_ref), planes(l2_ref)
    mx = jnp.maximum(jnp.maximum(l0, l1), l2)
    e0, e1, e2 = jnp.exp(l0 - mx), jnp.exp(l1 - mx), jnp.exp(l2 - mx)
    ob = (e0 * planes(o0_ref) + e1 * planes(o1_ref) + e2 * planes(o2_ref)) / (e0 + e1 + e2)
    yb = jnp.dot(ob.astype(BF16), wb_ref[...], preferred_element_type=F32)
    merged = gates[:, :D_MODEL] * ya + gates[:, D_MODEL:] * yb
    mix = jnp.dot(merged.astype(BF16), wo_ref[...], preferred_element_type=F32)
    h1 = _layer_norm(DEEPNORM_ALPHA * h0 + mix, g1_ref[...], b1_ref[...])
    h1_ref[...] = h1
    h1b_ref[...] = h1.astype(BF16)
    hi, lo = _split_bf16(h1)
    logit_ref[0] = (lax.dot_general(wr_hi_ref[...], hi, NT_DIMS, preferred_element_type=F32)
                    + lax.dot_general(wr_hi_ref[...], lo, NT_DIMS, preferred_element_type=F32)
                    + lax.dot_general(wr_lo_ref[...], hi, NT_DIMS, preferred_element_type=F32))


def _merge(x2, ln0_g, ln0_b, w_gates, b_gate, oa, obs, lses, w_a, w_b, w_o, ln1_g, ln1_b, w_router, seq, tm):
    m = x2.shape[0]
    tiles_per_seq = seq // tm
    row = lambda i: (i, 0)
    const = lambda i: (0, 0)
    wr_t = w_router.T
    wr_hi = wr_t.astype(BF16)
    wr_lo = (wr_t - wr_hi.astype(F32)).astype(BF16)
    vec = lambda v: v.reshape(1, -1)
    full = lambda a: pl.BlockSpec(a.shape, const)
    tile = lambda w: pl.BlockSpec((tm, w), row)
    args = [x2, vec(ln0_g), vec(ln0_b), w_gates, vec(b_gate), oa, *obs, *lses, w_a, w_b, w_o,
            vec(ln1_g), vec(ln1_b), wr_hi, wr_lo]
    in_specs = [tile(D_MODEL), full(args[1]), full(args[2]), full(w_gates), full(args[4]), tile(A_Q_W),
                *[pl.BlockSpec((B_GROUP_W // LANES, tm, LANES), lambda i: (0, i, 0))] * 6, full(w_a), full(w_b), full(w_o), full(args[15]), full(args[16]),
                full(wr_hi), full(wr_lo)]
    return pl.pallas_call(
        _merge_kernel,
        grid=(m // tm,),
        in_specs=in_specs,
        out_specs=[tile(D_MODEL), tile(D_MODEL),
                   pl.BlockSpec((1, N_EXPERTS, tm), lambda i: (i // tiles_per_seq, 0, i % tiles_per_seq))],
        out_shape=[jax.ShapeDtypeStruct((m, D_MODEL), F32), jax.ShapeDtypeStruct((m, D_MODEL), BF16),
                   jax.ShapeDtypeStruct((m // seq, N_EXPERTS, seq), F32)],
        compiler_params=_params(1),
        name="merge_ln1_router",
    )(*args)


PREFIX_CHUNK = 256


def _prefix_count(mask, tri):
    ones = jnp.where(mask, 1.0, 0.0)
    carry = jnp.zeros((mask.shape[0], 1), F32)
    outs = []
    for j in range(mask.shape[1] // PREFIX_CHUNK):
        chunk = ones[:, j * PREFIX_CHUNK:(j + 1) * PREFIX_CHUNK]
        outs.append(jnp.dot(chunk.astype(BF16), tri, preferred_element_type=F32) + carry)
        carry = carry + jnp.sum(chunk, axis=1, keepdims=True)
    return jnp.concatenate(outs, axis=1)


THRESHOLD_BITS = 31
THRESHOLD_REFINE = 16


def _route_kernel(logit_ref, slot_ref, aff_ref, *, cap):
    n_seq = logit_ref.shape[0]
    affs = []
    for b in range(n_seq):
        lg = logit_ref[b]
        ex = jnp.exp(lg - jnp.max(lg, axis=0, keepdims=True))
        affs.append(ex / jnp.sum(ex, axis=0, keepdims=True))
    aff = jnp.concatenate(affs, axis=0)

    def count_ge(t):
        return jnp.sum(jnp.where(aff >= t, 1.0, 0.0), axis=1, keepdims=True)

    thr = jnp.zeros((aff.shape[0], 1), jnp.int32)
    for bit in range(THRESHOLD_BITS - 1, -1, -1):
        cand = thr | (1 << bit)
        thr = jnp.where(count_ge(pltpu.bitcast(cand, F32)) >= cap, cand, thr)
    lo = pltpu.bitcast(thr, F32)
    hi = pltpu.bitcast(thr + 1, F32)
    for _ in range(THRESHOLD_REFINE):
        mid = 0.5 * (lo + hi)
        take = count_ge(mid) >= cap
        lo = jnp.where(take, mid, lo)
        hi = jnp.where(take, hi, mid)
    above = aff >= hi
    tied = (aff >= lo) & (aff < hi)
    r = lax.broadcasted_iota(jnp.int32, (PREFIX_CHUNK, PREFIX_CHUNK), 0)
    c = lax.broadcasted_iota(jnp.int32, (PREFIX_CHUNK, PREFIX_CHUNK), 1)
    tri = jnp.where(r < c, 1.0, 0.0).astype(BF16)
    need = cap - jnp.sum(jnp.where(above, 1.0, 0.0), axis=1, keepdims=True)
    sel = above | (tied & (_prefix_count(tied, tri) < need))
    slot = jnp.where(sel, _prefix_count(sel, tri), -1.0).astype(jnp.int32)
    for b in range(n_seq):
        rows = slice(b * N_EXPERTS, (b + 1) * N_EXPERTS)
        slot_ref[b] = slot[rows]
        aff_ref[b] = aff[rows]


ROUTE_SEQS_PER_STEP = 4


def _route(logits, batch, seq, cap):
    nb = ROUTE_SEQS_PER_STEP if batch % ROUTE_SEQS_PER_STEP == 0 else 1
    blk = pl.BlockSpec((nb, N_EXPERTS, seq), lambda i: (i, 0, 0))
    return pl.pallas_call(
        functools.partial(_route_kernel, cap=cap),
        grid=(batch // nb,),
        in_specs=[blk],
        out_specs=[blk, blk],
        out_shape=[jax.ShapeDtypeStruct((batch, N_EXPERTS, seq), jnp.int32),
                   jax.ShapeDtypeStruct((batch, N_EXPERTS, seq), F32)],
        compiler_params=_params(1),
        name="route",
    )(logits)


def _dispatch_kernel(h_ref, slot_ref, aff_ref, xe_ref, gate_ref, *, cap):
    slot = slot_ref[0, 0]
    seq = slot.shape[1]
    onehot = lax.broadcasted_iota(jnp.int32, (cap, seq), 0) == slot
    sel = jnp.where(onehot, 1.0, 0.0).astype(BF16)
    xe_ref[0] = jnp.dot(sel, h_ref[0], preferred_element_type=F32).astype(BF16)
    gate = jnp.sum(jnp.where(onehot, aff_ref[0, 0], 0.0), axis=1, keepdims=True)
    gate_ref[0] = jnp.broadcast_to(gate, (cap, LANES))


def _dispatch(h1b, slot, aff, cap):
    b, s, _ = h1b.shape
    idx = pl.BlockSpec((1, 1, 1, s), lambda i, e: (i, e, 0, 0))
    return pl.pallas_call(
        functools.partial(_dispatch_kernel, cap=cap),
        grid=(b, N_EXPERTS),
        in_specs=[pl.BlockSpec((1, s, D_MODEL), lambda i, e: (i, 0, 0)), idx, idx],
        out_specs=[pl.BlockSpec((1, cap, D_MODEL), lambda i, e: (e, i, 0)),
                   pl.BlockSpec((1, cap, LANES), lambda i, e: (e, i, 0))],
        out_shape=[jax.ShapeDtypeStruct((N_EXPERTS, b * cap, D_MODEL), BF16),
                   jax.ShapeDtypeStruct((N_EXPERTS, b * cap, LANES), F32)],
        compiler_params=_params(2),
        name="dispatch",
    )(h1b, slot.reshape(b, N_EXPERTS, 1, s), aff.reshape(b, N_EXPERTS, 1, s))


FF_CHUNK = 512


def _expert_kernel(x_ref, gate_ref, wg_ref, wu_ref, wd_ref, y_ref):
    x = x_ref[0]
    acc = jnp.zeros((x.shape[0], D_MODEL), F32)
    for c in range(D_FF // FF_CHUNK):
        cols = slice(c * FF_CHUNK, (c + 1) * FF_CHUNK)
        g = jnp.dot(x, wg_ref[0, :, cols], preferred_element_type=F32)
        u = jnp.dot(x, wu_ref[0, :, cols], preferred_element_type=F32)
        act = (g * jax.nn.sigmoid(g) * u).astype(BF16)
        acc = acc + jnp.dot(act, wd_ref[0, cols, :], preferred_element_type=F32)
    y_ref[0] = (acc * gate_ref[0][:, :1]).astype(BF16)


def _expert_ffn(xe, gate, w_gate, w_up, w_down, tm):
    e, rows, _ = xe.shape
    tok = lambda w: pl.BlockSpec((1, tm, w), lambda i, j: (i, j, 0))
    wspec = lambda a: pl.BlockSpec((1,) + a.shape[1:], lambda i, j: (i, 0, 0))
    return pl.pallas_call(
        _expert_kernel,
        grid=(e, rows // tm),
        in_specs=[tok(D_MODEL), tok(LANES), wspec(w_gate), wspec(w_up), wspec(w_down)],
        out_specs=tok(D_MODEL),
        out_shape=jax.ShapeDtypeStruct((e, rows, D_MODEL), BF16),
        compiler_params=_params(2),
        name="expert_ffn",
    )(xe, gate, w_gate, w_up, w_down)


def _combine_kernel(h1_ref, slot_ref, y_ref, g2_ref, b2_ref, out_ref, *, cap):
    slot = slot_ref[0]
    lane = lax.broadcasted_iota(jnp.int32, (1, cap), 1)
    acc = jnp.zeros(h1_ref.shape[1:], F32)
    for e in range(N_EXPERTS):
        sel = jnp.where(slot[:, e:e + 1] == lane, 1.0, 0.0).astype(BF16)
        acc = acc + jnp.dot(sel, y_ref[e], preferred_element_type=F32)
    out_ref[0] = _layer_norm(DEEPNORM_ALPHA * h1_ref[0] + acc, g2_ref[...], b2_ref[...])


def _combine(h1, slot_t, y, ln2_g, ln2_b, cap, tt):
    b, s, _ = h1.shape
    const = lambda i, j: (0, 0)
    return pl.pallas_call(
        functools.partial(_combine_kernel, cap=cap),
        grid=(b, s // tt),
        in_specs=[pl.BlockSpec((1, tt, D_MODEL), lambda i, j: (i, j, 0)),
                  pl.BlockSpec((1, tt, N_EXPERTS), lambda i, j: (i, j, 0)),
                  pl.BlockSpec((N_EXPERTS, cap, D_MODEL), lambda i, j: (0, i, 0)),
                  pl.BlockSpec((1, D_MODEL), const), pl.BlockSpec((1, D_MODEL), const)],
        out_specs=pl.BlockSpec((1, tt, D_MODEL), lambda i, j: (i, j, 0)),
        out_shape=jax.ShapeDtypeStruct((b, s, D_MODEL), F32),
        compiler_params=_params(2),
        name="combine_ln2",
    )(h1, slot_t, y, ln2_g.reshape(1, D_MODEL), ln2_b.reshape(1, D_MODEL))


def kernel(x, ln0_g, ln0_b, w_in, b_gate, qn_g, kn_g, w_branch_a, w_branch_b, w_out, ln1_g, ln1_b, w_router,
           w_gate_e, w_up_e, w_down_e, ln2_g, ln2_b):
    batch, seq, _ = x.shape
    assert w_in.shape[0] == DEPTH
    cap = EC_CAPACITY_FACTOR * seq // N_EXPERTS
    tokens = batch * seq
    x2 = x.reshape(tokens, D_MODEL)
    w_in_b = w_in[0].astype(BF16)

    qkv = _in_projection(x2, ln0_g, ln0_b, w_in_b[:, :QKV_W], qn_g[0], kn_g[0], seq, tm=256)
    qa, ka, va = qkv[:3]
    qb, kb, vb = qkv[3:6], qkv[6:9], qkv[9:12]

    oa = _attention_a(qa.reshape(batch, seq, A_Q_W), ka.reshape(batch, seq, A_KV_W),
                      va.reshape(batch, seq, A_KV_W), tq=256).reshape(tokens, A_Q_W)

    obs, lses = [], []
    for gi, (window, dilation) in enumerate(B_GROUPS):
        assert window // (2 * dilation) == BAND_HALF
        o, lse = _dilated_attention(qb[gi], kb[gi], vb[gi], seq, dilation)
        obs.append(o)
        lses.append(lse)

    h1, h1b, logits = _merge(x2, ln0_g, ln0_b, w_in_b[:, QKV_W:], b_gate[0], oa, obs, lses,
                             w_branch_a[0].astype(BF16), w_branch_b[0].astype(BF16), w_out[0].astype(BF16),
                             ln1_g[0], ln1_b[0], w_router[0], seq, tm=256)

    slot, aff = _route(logits, batch, seq, cap)
    xe, gate = _dispatch(h1b.reshape(batch, seq, D_MODEL), slot, aff, cap)
    y = _expert_ffn(xe, gate, w_gate_e[0].astype(BF16), w_up_e[0].astype(BF16), w_down_e[0].astype(BF16),
                    tm=min(1024, batch * cap))
    return _combine(h1.reshape(batch, seq, D_MODEL), slot.transpose(0, 2, 1), y, ln2_g[0], ln2_b[0], cap, tt=512)
```

```python
import functools

import jax
import jax.numpy as jnp
from jax import lax
from jax.experimental import pallas as pl
from jax.experimental.pallas import tpu as pltpu

F32 = jnp.float32
BF16 = jnp.bfloat16

D_MODEL = 1024
HEAD_DIM = 64
A_Q_HEADS = 8
A_KV_HEADS = 2
A_GROUP = A_Q_HEADS // A_KV_HEADS
B_GROUPS = ((128, 1), (512, 4), (2048, 16))
B_HEADS_PER_GROUP = 4
GRID_W = 64
A_ROPE_THETA = 10000.0
B_ROPE_THETA = 500000.0
B_ROPE_DIMS = HEAD_DIM // 4
N_EXPERTS = 16
EC_CAPACITY_FACTOR = 2
D_FF = 2 * D_MODEL
LN_EPS = 1e-5
QK_NORM_EPS = 1e-6
MASK_VALUE = -1e30
DEPTH = 1
DEEPNORM_ALPHA = (2.0 * DEPTH) ** 0.25
SM_SCALE = HEAD_DIM ** -0.5
LOG2_E = 1.4426950408889634
A_Q_SCALE = SM_SCALE * LOG2_E

A_Q_W = A_Q_HEADS * HEAD_DIM
A_KV_W = A_KV_HEADS * HEAD_DIM
B_GROUP_W = B_HEADS_PER_GROUP * HEAD_DIM
B_W = B_GROUP_W * len(B_GROUPS)
QKV_W = A_Q_W + 2 * A_KV_W + 3 * B_W
BAND_HALF = 64

LANES = 128
VMEM_LIMIT = 56 * 1024 * 1024

NT_DIMS = (((1,), (1,)), ((), ()))


def _params(n_axes):
    return pltpu.CompilerParams(dimension_semantics=("arbitrary",) * n_axes, vmem_limit_bytes=VMEM_LIMIT)


def _layer_norm(x, g, b):
    mu = jnp.mean(x, axis=-1, keepdims=True)
    xc = x - mu
    var = jnp.mean(xc * xc, axis=-1, keepdims=True)
    return xc * lax.rsqrt(var + LN_EPS) * g + b


def _split_bf16(x):
    hi = x.astype(BF16)
    lo = (x - hi.astype(F32)).astype(BF16)
    return hi, lo


def _swap_halves(y, lane, period, half):
    fwd = pltpu.roll(y, LANES - half, 1)
    bwd = pltpu.roll(y, half, 1)
    return jnp.where((lane % period) < half, fwd, bwd)


def _inproj_kernel(x_ref, g0_ref, b0_ref, w_ref, qng_ref, kng_ref, cosa_ref, sina_ref, cosb_ref, sinb_ref,
                   qa_ref, ka_ref, va_ref, qb0_ref, qb1_ref, qb2_ref, kb0_ref, kb1_ref, kb2_ref,
                   vb0_ref, vb1_ref, vb2_ref):
    h = _layer_norm(x_ref[...], g0_ref[...], b0_ref[...]).astype(BF16)
    lane = lax.broadcasted_iota(jnp.int32, (1, LANES), 1)
    r = lax.broadcasted_iota(jnp.int32, (LANES, LANES), 0)
    c = lax.broadcasted_iota(jnp.int32, (LANES, LANES), 1)
    head_sum = jnp.where((r // HEAD_DIM) == (c // HEAD_DIM), 1.0, 0.0).astype(BF16)
    cosa, sina = cosa_ref[...], sina_ref[...]
    cosb, sinb = cosb_ref[...], sinb_ref[...]

    def proj(off):
        y = jnp.dot(h, w_ref[:, off:off + 2 * LANES], preferred_element_type=F32)
        return y[:, :LANES], y[:, LANES:]

    def norm_rope_a(y, gain, scale):
        hi, lo = _split_bf16(y * y)
        ss = (jnp.dot(hi, head_sum, preferred_element_type=F32)
              + jnp.dot(lo, head_sum, preferred_element_type=F32))
        y = y * lax.rsqrt(ss * (1.0 / HEAD_DIM) + QK_NORM_EPS) * gain
        y = y * cosa + _swap_halves(y, lane, HEAD_DIM // 2, HEAD_DIM // 4) * sina
        return (y * scale).astype(BF16)

    def rope_b(y, scale):
        y = y * cosb + _swap_halves(y, lane, HEAD_DIM, B_ROPE_DIMS // 2) * sinb
        return y * scale

    for j in range(A_Q_W // (2 * LANES)):
        for i, y in enumerate(proj(2 * j * LANES)):
            qa_ref[:, (2 * j + i) * LANES:(2 * j + i + 1) * LANES] = norm_rope_a(y, qng_ref[...], A_Q_SCALE)
    ka, va = proj(A_Q_W)
    ka_ref[...] = norm_rope_a(ka, kng_ref[...], 1.0)
    va_ref[...] = va.astype(BF16)
    base = A_Q_W + 2 * A_KV_W
    for gi, (q_ref, k_ref, v_ref) in enumerate(((qb0_ref, kb0_ref, vb0_ref), (qb1_ref, kb1_ref, vb1_ref),
                                                (qb2_ref, kb2_ref, vb2_ref))):
        off = base + gi * B_GROUP_W
        for j, (q, k, v) in enumerate(zip(proj(off), proj(off + B_W), proj(off + 2 * B_W))):
            q_ref[j] = rope_b(q, SM_SCALE)
            k_ref[j] = rope_b(k, 1.0)
            v_ref[j] = v


def _rope_tables(seq):
    lane = jnp.arange(LANES)
    d = lane % HEAD_DIM
    t = jnp.arange(seq)
    half_rot = HEAD_DIM // 2
    inv_a = A_ROPE_THETA ** (-jnp.arange(0, half_rot, 2, dtype=F32) / half_rot)
    j = d % half_rot
    pos_row = (t // GRID_W).astype(F32)
    pos_col = (t % GRID_W).astype(F32)
    freq_a = inv_a[j % (half_rot // 2)]
    ang_a = jnp.where((d < half_rot)[None, :], pos_row[:, None], pos_col[:, None]) * freq_a[None, :]
    sign_a = jnp.where(j < half_rot // 2, -1.0, 1.0).astype(F32)
    cosa = jnp.cos(ang_a)
    sina = jnp.sin(ang_a) * sign_a[None, :]
    inv_b = B_ROPE_THETA ** (-jnp.arange(0, B_ROPE_DIMS, 2, dtype=F32) / B_ROPE_DIMS)
    freq_b = inv_b[d % (B_ROPE_DIMS // 2)]
    ang_b = t.astype(F32)[:, None] * freq_b[None, :]
    rot = (d < B_ROPE_DIMS)[None, :]
    sign_b = jnp.where(d < B_ROPE_DIMS // 2, -1.0, 1.0).astype(F32)
    cosb = jnp.where(rot, jnp.cos(ang_b), 1.0)
    sinb = jnp.where(rot, jnp.sin(ang_b) * sign_b[None, :], 0.0)
    return cosa, sina, cosb, sinb


def _in_projection(x2, ln0_g, ln0_b, w_qkv, qn_g, kn_g, seq, tm):
    m = x2.shape[0]
    cosa, sina, cosb, sinb = _rope_tables(seq)
    tiles_per_seq = seq // tm
    row = lambda i: (i, 0)
    const = lambda i: (0, 0)
    table = lambda i: (i % tiles_per_seq, 0)
    widths = (A_Q_W, A_KV_W, A_KV_W) + (B_GROUP_W,) * 9
    gain_tile = lambda g: jnp.tile(g.reshape(1, HEAD_DIM), (1, LANES // HEAD_DIM))
    return pl.pallas_call(
        _inproj_kernel,
        grid=(m // tm,),
        in_specs=[
            pl.BlockSpec((tm, D_MODEL), row),
            pl.BlockSpec((1, D_MODEL), const),
            pl.BlockSpec((1, D_MODEL), const),
            pl.BlockSpec((D_MODEL, QKV_W), const),
            pl.BlockSpec((1, LANES), const),
            pl.BlockSpec((1, LANES), const),
            pl.BlockSpec((tm, LANES), table),
            pl.BlockSpec((tm, LANES), table),
            pl.BlockSpec((tm, LANES), table),
            pl.BlockSpec((tm, LANES), table),
        ],
        out_specs=([pl.BlockSpec((tm, w), row) for w in widths[:3]]
                   + [pl.BlockSpec((B_GROUP_W // LANES, tm, LANES), lambda i: (0, i, 0))] * 9),
        out_shape=([jax.ShapeDtypeStruct((m, w), BF16) for w in widths[:3]]
                   + [jax.ShapeDtypeStruct((B_GROUP_W // LANES, m, LANES), F32)] * 9),
        compiler_params=_params(1),
        name="in_projection",
    )(x2, ln0_g.reshape(1, D_MODEL), ln0_b.reshape(1, D_MODEL), w_qkv, gain_tile(qn_g), gain_tile(kn_g),
      cosa, sina, cosb, sinb)


def _attn_a_kernel(q_ref, k_ref, v_ref, o_ref):
    outs = []
    for kh in range(A_KV_HEADS):
        k = k_ref[0, :, kh * HEAD_DIM:(kh + 1) * HEAD_DIM]
        v = v_ref[0, :, kh * HEAD_DIM:(kh + 1) * HEAD_DIM]
        for g in range(A_GROUP):
            hq = kh * A_GROUP + g
            q = q_ref[0, :, hq * HEAD_DIM:(hq + 1) * HEAD_DIM]
            s = lax.dot_general(q, k, NT_DIMS, preferred_element_type=F32)
            p = jnp.exp2(s - jnp.max(s, axis=-1, keepdims=True))
            denom = jnp.sum(p, axis=-1, keepdims=True)
            o = jnp.dot(p.astype(BF16), v, preferred_element_type=F32)
            outs.append(o / denom)
    o_ref[0] = jnp.concatenate(outs, axis=1).astype(BF16)


def _attention_a(qa, ka, va, tq):
    b, s, _ = qa.shape
    return pl.pallas_call(
        _attn_a_kernel,
        grid=(b, s // tq),
        in_specs=[
            pl.BlockSpec((1, tq, A_Q_W), lambda i, j: (i, j, 0)),
            pl.BlockSpec((1, s, A_KV_W), lambda i, j: (i, 0, 0)),
            pl.BlockSpec((1, s, A_KV_W), lambda i, j: (i, 0, 0)),
        ],
        out_specs=pl.BlockSpec((1, tq, A_Q_W), lambda i, j: (i, j, 0)),
        out_shape=jax.ShapeDtypeStruct((b, s, A_Q_W), BF16),
        compiler_params=_params(2),
        name="attention_a",
    )(qa, ka, va)


BAND_QB = 128
BAND_UNITS_PER_STEP = 4
BAND_STEPS = 4


def _dilated_kernel(q_ref, k_ref, v_ref, o_ref, lse_ref, *, n, dilation):
    win = min(2 * BAND_QB, n)
    n_blk = n // BAND_QB
    rows = B_HEADS_PER_GROUP * BAND_QB
    lane_head = lax.broadcasted_iota(jnp.int32, (1, B_GROUP_W), 1) // HEAD_DIM
    rel = (lax.broadcasted_iota(jnp.int32, (rows, win), 0) % BAND_QB
           - lax.broadcasted_iota(jnp.int32, (rows, win), 1))

    planes = B_GROUP_W // LANES

    def class_rows(r, start, size):
        if dilation == 1:
            return pl.ds(pl.multiple_of(start, BAND_HALF), size)
        return pl.ds(r + start * dilation, size, stride=dilation)

    def load(ref, rows_):
        return jnp.concatenate([ref[j, rows_, :] for j in range(planes)], axis=1)

    for i in range(BAND_UNITS_PER_STEP):
        unit = pl.program_id(1) * BAND_UNITS_PER_STEP + i
        r = unit // n_blk
        qs = (unit % n_blk) * BAND_QB
        ks = jnp.clip(qs - BAND_HALF, 0, n - win)
        q = load(q_ref, class_rows(r, qs, BAND_QB))
        k = load(k_ref, class_rows(r, ks, win)).astype(BF16)
        v = load(v_ref, class_rows(r, ks, win)).astype(BF16)
        q_stack = jnp.concatenate([jnp.where(lane_head == h, q, 0.0) for h in range(B_HEADS_PER_GROUP)], axis=0)
        s = lax.dot_general(q_stack.astype(BF16), k, NT_DIMS, preferred_element_type=F32)
        s = jnp.where(jnp.abs(rel + (qs - ks)) <= BAND_HALF, s, MASK_VALUE)
        mx = jnp.max(s, axis=-1, keepdims=True)
        p = jnp.exp(s - mx)
        denom = jnp.sum(p, axis=-1, keepdims=True)
        pv = jnp.dot(p.astype(BF16), v, preferred_element_type=F32) / denom
        lse = mx + jnp.log(denom)
        o = jnp.zeros((BAND_QB, B_GROUP_W), F32)
        l = jnp.zeros((BAND_QB, B_GROUP_W), F32)
        for h in range(B_HEADS_PER_GROUP):
            blk = slice(h * BAND_QB, (h + 1) * BAND_QB)
            o = jnp.where(lane_head == h, pv[blk], o)
            l = jnp.where(lane_head == h, lse[blk], l)
        for j in range(planes):
            o_ref[j, class_rows(r, qs, BAND_QB), :] = o[:, j * LANES:(j + 1) * LANES]
            lse_ref[j, class_rows(r, qs, BAND_QB), :] = l[:, j * LANES:(j + 1) * LANES]


def _dilated_attention(q, k, v, seq, dilation):
    planes, tokens, _ = q.shape
    n = seq // dilation
    assert dilation * (n // BAND_QB) == BAND_UNITS_PER_STEP * BAND_STEPS
    blk = pl.BlockSpec((planes, seq, LANES), lambda i, j: (0, i, 0))
    return pl.pallas_call(
        functools.partial(_dilated_kernel, n=n, dilation=dilation),
        grid=(tokens // seq, BAND_STEPS),
        in_specs=[blk, blk, blk],
        out_specs=[blk, blk],
        out_shape=[jax.ShapeDtypeStruct(q.shape, F32)] * 2,
        compiler_params=_params(2),
        name=f"dilated_attention_d{dilation}",
    )(q, k, v)


MERGE_SUB_ROWS = 512


def _merge_kernel(x_ref, g0_ref, b0_ref, wg_ref, bg_ref, oa_ref, o0_ref, o1_ref, o2_ref, l0_ref, l1_ref, l2_ref,
                  wa_ref, wb_ref, wo_ref, g1_ref, b1_ref, wr_hi_ref, wr_lo_ref,
                  h1_ref, h1b_ref, logit_ref):
    tm = x_ref.shape[0]
    sub = min(MERGE_SUB_ROWS, tm)
    for r0 in range(0, tm, sub):
        rows = slice(r0, r0 + sub)
        planes = lambda ref: jnp.concatenate([ref[j, rows, :] for j in range(ref.shape[0])], axis=1)
        h0 = _layer_norm(x_ref[rows, :], g0_ref[...], b0_ref[...])
        gates = jax.nn.sigmoid(jnp.dot(h0.astype(BF16), wg_ref[...], preferred_element_type=F32) + bg_ref[...])
        ya = jnp.dot(oa_ref[rows, :], wa_ref[...], preferred_element_type=F32)
        l0, l1, l2 = planes(l0_ref), planes(l1_ref), planes(l2_ref)
        mx = jnp.maximum(jnp.maximum(l0, l1), l2)
        e0, e1, e2 = jnp.exp(l0 - mx), jnp.exp(l1 - mx), jnp.exp(l2 - mx)
        ob = (e0 * planes(o0_ref) + e1 * planes(o1_ref) + e2 * planes(o2_ref)) / (e0 + e1 + e2)
        yb = jnp.dot(ob.astype(BF16), wb_ref[...], preferred_element_type=F32)
        merged = gates[:, :D_MODEL] * ya + gates[:, D_MODEL:] * yb
        mix = jnp.dot(merged.astype(BF16), wo_ref[...], preferred_element_type=F32)
        h1 = _layer_norm(DEEPNORM_ALPHA * h0 + mix, g1_ref[...], b1_ref[...])
        h1_ref[rows, :] = h1
        h1b_ref[rows, :] = h1.astype(BF16)
        hi, lo = _split_bf16(h1)
        logit_ref[0, :, rows] = (lax.dot_general(wr_hi_ref[...], hi, NT_DIMS, preferred_element_type=F32)
                                 + lax.dot_general(wr_hi_ref[...], lo, NT_DIMS, preferred_element_type=F32)
                                 + lax.dot_general(wr_lo_ref[...], hi, NT_DIMS, preferred_element_type=F32))


def _merge(x2, ln0_g, ln0_b, w_gates, b_gate, oa, obs, lses, w_a, w_b, w_o, ln1_g, ln1_b, w_router, seq, tm):
    m = x2.shape[0]
    tiles_per_seq = seq // tm
    row = lambda i: (i, 0)
    const = lambda i: (0, 0)
    wr_t = w_router.T
    wr_hi = wr_t.astype(BF16)
    wr_lo = (wr_t - wr_hi.astype(F32)).astype(BF16)
    vec = lambda v: v.reshape(1, -1)
    full = lambda a: pl.BlockSpec(a.shape, const)
    tile = lambda w: pl.BlockSpec((tm, w), row)
    args = [x2, vec(ln0_g), vec(ln0_b), w_gates, vec(b_gate), oa, *obs, *lses, w_a, w_b, w_o,
            vec(ln1_g), vec(ln1_b), wr_hi, wr_lo]
    in_specs = [tile(D_MODEL), full(args[1]), full(args[2]), full(w_gates), full(args[4]), tile(A_Q_W),
                *[pl.BlockSpec((B_GROUP_W // LANES, tm, LANES), lambda i: (0, i, 0))] * 6, full(w_a), full(w_b), full(w_o), full(args[15]), full(args[16]),
                full(wr_hi), full(wr_lo)]
    return pl.pallas_call(
        _merge_kernel,
        grid=(m // tm,),
        in_specs=in_specs,
        out_specs=[tile(D_MODEL), tile(D_MODEL),
                   pl.BlockSpec((1, N_EXPERTS, tm), lambda i: (i // tiles_per_seq, 0, i % tiles_per_seq))],
        out_shape=[jax.ShapeDtypeStruct((m, D_MODEL), F32), jax.ShapeDtypeStruct((m, D_MODEL), BF16),
                   jax.ShapeDtypeStruct((m // seq, N_EXPERTS, seq), F32)],
        compiler_params=_params(1),
        name="merge_ln1_router",
    )(*args)


PREFIX_CHUNK = 256


def _prefix_count(mask, tri):
    ones = jnp.where(mask, 1.0, 0.0)
    carry = jnp.zeros((mask.shape[0], 1), F32)
    outs = []
    for j in range(mask.shape[1] // PREFIX_CHUNK):
        chunk = ones[:, j * PREFIX_CHUNK:(j + 1) * PREFIX_CHUNK]
        outs.append(jnp.dot(chunk.astype(BF16), tri, preferred_element_type=F32) + carry)
        carry = carry + jnp.sum(chunk, axis=1, keepdims=True)
    return jnp.concatenate(outs, axis=1)


THRESHOLD_BITS = 31
THRESHOLD_REFINE = 16


def _route_kernel(logit_ref, slot_ref, aff_ref, *, cap):
    n_seq = logit_ref.shape[0]
    affs = []
    for b in range(n_seq):
        lg = logit_ref[b]
        ex = jnp.exp(lg - jnp.max(lg, axis=0, keepdims=True))
        affs.append(ex / jnp.sum(ex, axis=0, keepdims=True))
    aff = jnp.concatenate(affs, axis=0)

    def count_ge(t):
        return jnp.sum(jnp.where(aff >= t, 1.0, 0.0), axis=1, keepdims=True)

    thr = jnp.zeros((aff.shape[0], 1), jnp.int32)
    for bit in range(THRESHOLD_BITS - 1, -1, -1):
        cand = thr | (1 << bit)
        thr = jnp.where(count_ge(pltpu.bitcast(cand, F32)) >= cap, cand, thr)
    lo = pltpu.bitcast(thr, F32)
    hi = pltpu.bitcast(thr + 1, F32)
    for _ in range(THRESHOLD_REFINE):
        mid = 0.5 * (lo + hi)
        take = count_ge(mid) >= cap
        lo = jnp.where(take, mid, lo)
        hi = jnp.where(take, hi, mid)
    above = aff >= hi
    tied = (aff >= lo) & (aff < hi)
    r = lax.broadcasted_iota(jnp.int32, (PREFIX_CHUNK, PREFIX_CHUNK), 0)
    c = lax.broadcasted_iota(jnp.int32, (PREFIX_CHUNK, PREFIX_CHUNK), 1)
    tri = jnp.where(r < c, 1.0, 0.0).astype(BF16)
    need = cap - jnp.sum(jnp.where(above, 1.0, 0.0), axis=1, keepdims=True)
    sel = above | (tied & (_prefix_count(tied, tri) < need))
    slot = jnp.where(sel, _prefix_count(sel, tri), -1.0).astype(jnp.int32)
    for b in range(n_seq):
        rows = slice(b * N_EXPERTS, (b + 1) * N_EXPERTS)
        slot_ref[b] = slot[rows]
        aff_ref[b] = aff[rows]


ROUTE_SEQS_PER_STEP = 4


def _route(logits, batch, seq, cap):
    nb = ROUTE_SEQS_PER_STEP if batch % ROUTE_SEQS_PER_STEP == 0 else 1
    blk = pl.BlockSpec((nb, N_EXPERTS, seq), lambda i: (i, 0, 0))
    return pl.pallas_call(
        functools.partial(_route_kernel, cap=cap),
        grid=(batch // nb,),
        in_specs=[blk],
        out_specs=[blk, blk],
        out_shape=[jax.ShapeDtypeStruct((batch, N_EXPERTS, seq), jnp.int32),
                   jax.ShapeDtypeStruct((batch, N_EXPERTS, seq), F32)],
        compiler_params=_params(1),
        name="route",
    )(logits)


def _dispatch_kernel(h_ref, slot_ref, aff_ref, wg_ref, wu_ref, wd_ref,
                     xe_ref, gate_ref, wgb_ref, wub_ref, wdb_ref, *, cap):
    slot = slot_ref[0, 0]
    seq = slot.shape[1]
    onehot = lax.broadcasted_iota(jnp.int32, (cap, seq), 0) == slot
    sel = jnp.where(onehot, 1.0, 0.0).astype(BF16)
    xe_ref[0] = jnp.dot(sel, h_ref[0], preferred_element_type=F32).astype(BF16)
    gate = jnp.sum(jnp.where(onehot, aff_ref[0, 0], 0.0), axis=1, keepdims=True)
    gate_ref[0] = jnp.broadcast_to(gate, (cap, LANES))
    wgb_ref[...] = wg_ref[...].astype(BF16)
    wub_ref[...] = wu_ref[...].astype(BF16)
    wdb_ref[...] = wd_ref[...].astype(BF16)


def _dispatch(h1b, slot, aff, w_gate, w_up, w_down, cap):
    b, s, _ = h1b.shape
    idx = pl.BlockSpec((1, 1, 1, s), lambda i, e: (i, e, 0, 0))
    wslice = lambda a: pl.BlockSpec((1, a.shape[1] // b, a.shape[2]), lambda i, e: (e, i, 0))
    weights = (w_gate, w_up, w_down)
    return pl.pallas_call(
        functools.partial(_dispatch_kernel, cap=cap),
        grid=(b, N_EXPERTS),
        in_specs=[pl.BlockSpec((1, s, D_MODEL), lambda i, e: (i, 0, 0)), idx, idx, *map(wslice, weights)],
        out_specs=[pl.BlockSpec((1, cap, D_MODEL), lambda i, e: (e, i, 0)),
                   pl.BlockSpec((1, cap, LANES), lambda i, e: (e, i, 0)), *map(wslice, weights)],
        out_shape=[jax.ShapeDtypeStruct((N_EXPERTS, b * cap, D_MODEL), BF16),
                   jax.ShapeDtypeStruct((N_EXPERTS, b * cap, LANES), F32),
                   *[jax.ShapeDtypeStruct(a.shape, BF16) for a in weights]],
        compiler_params=_params(2),
        name="dispatch",
    )(h1b, slot.reshape(b, N_EXPERTS, 1, s), aff.reshape(b, N_EXPERTS, 1, s), *weights)


FF_CHUNK = 512


def _expert_kernel(x_ref, gate_ref, wg_ref, wu_ref, wd_ref, y_ref):
    x = x_ref[0]
    acc = jnp.zeros((x.shape[0], D_MODEL), F32)
    for c in range(D_FF // FF_CHUNK):
        cols = slice(c * FF_CHUNK, (c + 1) * FF_CHUNK)
        g = jnp.dot(x, wg_ref[0, :, cols], preferred_element_type=F32)
        u = jnp.dot(x, wu_ref[0, :, cols], preferred_element_type=F32)
        act = (g * jax.nn.sigmoid(g) * u).astype(BF16)
        acc = acc + jnp.dot(act, wd_ref[0, cols, :], preferred_element_type=F32)
    y_ref[0] = (acc * gate_ref[0][:, :1]).astype(BF16)


def _expert_ffn(xe, gate, w_gate, w_up, w_down, tm):
    e, rows, _ = xe.shape
    tok = lambda w: pl.BlockSpec((1, tm, w), lambda i, j: (i, j, 0))
    wspec = lambda a: pl.BlockSpec((1,) + a.shape[1:], lambda i, j: (i, 0, 0))
    return pl.pallas_call(
        _expert_kernel,
        grid=(e, rows // tm),
        in_specs=[tok(D_MODEL), tok(LANES), wspec(w_gate), wspec(w_up), wspec(w_down)],
        out_specs=tok(D_MODEL),
        out_shape=jax.ShapeDtypeStruct((e, rows, D_MODEL), BF16),
        compiler_params=_params(2),
        name="expert_ffn",
    )(xe, gate, w_gate, w_up, w_down)


def _combine_kernel(h1_ref, slot_ref, y_ref, g2_ref, b2_ref, out_ref, *, cap):
    slot = slot_ref[0]
    lane = lax.broadcasted_iota(jnp.int32, (1, cap), 1)
    acc = jnp.zeros(h1_ref.shape[1:], F32)
    for e in range(N_EXPERTS):
        sel = jnp.where(slot[:, e:e + 1] == lane, 1.0, 0.0).astype(BF16)
        acc = acc + jnp.dot(sel, y_ref[e], preferred_element_type=F32)
    out_ref[0] = _layer_norm(DEEPNORM_ALPHA * h1_ref[0] + acc, g2_ref[...], b2_ref[...])


def _combine(h1, slot_t, y, ln2_g, ln2_b, cap, tt):
    b, s, _ = h1.shape
    const = lambda i, j: (0, 0)
    return pl.pallas_call(
        functools.partial(_combine_kernel, cap=cap),
        grid=(b, s // tt),
        in_specs=[pl.BlockSpec((1, tt, D_MODEL), lambda i, j: (i, j, 0)),
                  pl.BlockSpec((1, tt, N_EXPERTS), lambda i, j: (i, j, 0)),
                  pl.BlockSpec((N_EXPERTS, cap, D_MODEL), lambda i, j: (0, i, 0)),
                  pl.BlockSpec((1, D_MODEL), const), pl.BlockSpec((1, D_MODEL), const)],
        out_specs=pl.BlockSpec((1, tt, D_MODEL), lambda i, j: (i, j, 0)),
        out_shape=jax.ShapeDtypeStruct((b, s, D_MODEL), F32),
        compiler_params=_params(2),
        name="combine_ln2",
    )(h1, slot_t, y, ln2_g.reshape(1, D_MODEL), ln2_b.reshape(1, D_MODEL))


def kernel(x, ln0_g, ln0_b, w_in, b_gate, qn_g, kn_g, w_branch_a, w_branch_b, w_out, ln1_g, ln1_b, w_router,
           w_gate_e, w_up_e, w_down_e, ln2_g, ln2_b):
    batch, seq, _ = x.shape
    assert w_in.shape[0] == DEPTH
    cap = EC_CAPACITY_FACTOR * seq // N_EXPERTS
    tokens = batch * seq
    x2 = x.reshape(tokens, D_MODEL)
    w_in_b = w_in[0].astype(BF16)

    qkv = _in_projection(x2, ln0_g, ln0_b, w_in_b[:, :QKV_W], qn_g[0], kn_g[0], seq, tm=512)
    qa, ka, va = qkv[:3]
    qb, kb, vb = qkv[3:6], qkv[6:9], qkv[9:12]

    oa = _attention_a(qa.reshape(batch, seq, A_Q_W), ka.reshape(batch, seq, A_KV_W),
                      va.reshape(batch, seq, A_KV_W), tq=256).reshape(tokens, A_Q_W)

    obs, lses = [], []
    for gi, (window, dilation) in enumerate(B_GROUPS):
        assert window // (2 * dilation) == BAND_HALF
        o, lse = _dilated_attention(qb[gi], kb[gi], vb[gi], seq, dilation)
        obs.append(o)
        lses.append(lse)

    h1, h1b, logits = _merge(x2, ln0_g, ln0_b, w_in_b[:, QKV_W:], b_gate[0], oa, obs, lses,
                             w_branch_a[0].astype(BF16), w_branch_b[0].astype(BF16), w_out[0].astype(BF16),
                             ln1_g[0], ln1_b[0], w_router[0], seq, tm=512)

    slot, aff = _route(logits, batch, seq, cap)
    xe, gate, wg_b, wu_b, wd_b = _dispatch(h1b.reshape(batch, seq, D_MODEL), slot, aff,
                                           w_gate_e[0], w_up_e[0], w_down_e[0], cap)
    y = _expert_ffn(xe, gate, wg_b, wu_b, wd_b, tm=min(1024, batch * cap))
    return _combine(h1.reshape(batch, seq, D_MODEL), slot.transpose(0, 2, 1), y, ln2_g[0], ln2_b[0], cap, tt=512)
```

```python
import functools

import jax
import jax.numpy as jnp
from jax import lax
from jax.experimental import pallas as pl
from jax.experimental.pallas import tpu as pltpu

F32 = jnp.float32
BF16 = jnp.bfloat16

D_MODEL = 1024
HEAD_DIM = 64
A_Q_HEADS = 8
A_KV_HEADS = 2
A_GROUP = A_Q_HEADS // A_KV_HEADS
B_GROUPS = ((128, 1), (512, 4), (2048, 16))
B_HEADS_PER_GROUP = 4
GRID_W = 64
A_ROPE_THETA = 10000.0
B_ROPE_THETA = 500000.0
B_ROPE_DIMS = HEAD_DIM // 4
N_EXPERTS = 16
EC_CAPACITY_FACTOR = 2
D_FF = 2 * D_MODEL
LN_EPS = 1e-5
QK_NORM_EPS = 1e-6
MASK_VALUE = -1e30
DEPTH = 1
DEEPNORM_ALPHA = (2.0 * DEPTH) ** 0.25
SM_SCALE = HEAD_DIM ** -0.5
LOG2_E = 1.4426950408889634
A_Q_SCALE = SM_SCALE * LOG2_E

A_Q_W = A_Q_HEADS * HEAD_DIM
A_KV_W = A_KV_HEADS * HEAD_DIM
B_GROUP_W = B_HEADS_PER_GROUP * HEAD_DIM
B_W = B_GROUP_W * len(B_GROUPS)
QKV_W = A_Q_W + 2 * A_KV_W + 3 * B_W
BAND_HALF = 64

LANES = 128
VMEM_LIMIT = 56 * 1024 * 1024

NT_DIMS = (((1,), (1,)), ((), ()))


def _params(n_axes):
    return pltpu.CompilerParams(dimension_semantics=("arbitrary",) * n_axes, vmem_limit_bytes=VMEM_LIMIT)


def _layer_norm(x, g, b):
    mu = jnp.mean(x, axis=-1, keepdims=True)
    xc = x - mu
    var = jnp.mean(xc * xc, axis=-1, keepdims=True)
    return xc * lax.rsqrt(var + LN_EPS) * g + b


def _split_bf16(x):
    hi = x.astype(BF16)
    lo = (x - hi.astype(F32)).astype(BF16)
    return hi, lo


def _swap_halves(y, lane, period, half):
    fwd = pltpu.roll(y, LANES - half, 1)
    bwd = pltpu.roll(y, half, 1)
    return jnp.where((lane % period) < half, fwd, bwd)


def _inproj_kernel(x_ref, g0_ref, b0_ref, w_ref, qng_ref, kng_ref, cosa_ref, sina_ref, cosb_ref, sinb_ref,
                   qa_ref, ka_ref, va_ref, qb0_ref, qb1_ref, qb2_ref, kb0_ref, kb1_ref, kb2_ref,
                   vb0_ref, vb1_ref, vb2_ref):
    h = _layer_norm(x_ref[...], g0_ref[...], b0_ref[...]).astype(BF16)
    lane = lax.broadcasted_iota(jnp.int32, (1, LANES), 1)
    r = lax.broadcasted_iota(jnp.int32, (LANES, LANES), 0)
    c = lax.broadcasted_iota(jnp.int32, (LANES, LANES), 1)
    head_sum = jnp.where((r // HEAD_DIM) == (c // HEAD_DIM), 1.0, 0.0).astype(BF16)
    cosa, sina = cosa_ref[...], sina_ref[...]
    cosb, sinb = cosb_ref[...], sinb_ref[...]

    def proj(off):
        y = jnp.dot(h, w_ref[:, off:off + 2 * LANES], preferred_element_type=F32)
        return y[:, :LANES], y[:, LANES:]

    def norm_rope_a(y, gain, scale):
        hi, lo = _split_bf16(y * y)
        ss = (jnp.dot(hi, head_sum, preferred_element_type=F32)
              + jnp.dot(lo, head_sum, preferred_element_type=F32))
        y = y * lax.rsqrt(ss * (1.0 / HEAD_DIM) + QK_NORM_EPS) * gain
        y = y * cosa + _swap_halves(y, lane, HEAD_DIM // 2, HEAD_DIM // 4) * sina
        return (y * scale).astype(BF16)

    def rope_b(y, scale):
        y = y * cosb + _swap_halves(y, lane, HEAD_DIM, B_ROPE_DIMS // 2) * sinb
        return y * scale

    for j in range(A_Q_W // (2 * LANES)):
        for i, y in enumerate(proj(2 * j * LANES)):
            qa_ref[:, (2 * j + i) * LANES:(2 * j + i + 1) * LANES] = norm_rope_a(y, qng_ref[...], A_Q_SCALE)
    ka, va = proj(A_Q_W)
    ka_ref[...] = norm_rope_a(ka, kng_ref[...], 1.0)
    va_ref[...] = va.astype(BF16)
    base = A_Q_W + 2 * A_KV_W
    for gi, (q_ref, k_ref, v_ref) in enumerate(((qb0_ref, kb0_ref, vb0_ref), (qb1_ref, kb1_ref, vb1_ref),
                                                (qb2_ref, kb2_ref, vb2_ref))):
        off = base + gi * B_GROUP_W
        for j, (q, k, v) in enumerate(zip(proj(off), proj(off + B_W), proj(off + 2 * B_W))):
            q_ref[j] = rope_b(q, SM_SCALE)
            k_ref[j] = rope_b(k, 1.0)
            v_ref[j] = v


def _rope_tables(seq):
    lane = jnp.arange(LANES)
    d = lane % HEAD_DIM
    t = jnp.arange(seq)
    half_rot = HEAD_DIM // 2
    inv_a = A_ROPE_THETA ** (-jnp.arange(0, half_rot, 2, dtype=F32) / half_rot)
    j = d % half_rot
    pos_row = (t // GRID_W).astype(F32)
    pos_col = (t % GRID_W).astype(F32)
    freq_a = inv_a[j % (half_rot // 2)]
    ang_a = jnp.where((d < half_rot)[None, :], pos_row[:, None], pos_col[:, None]) * freq_a[None, :]
    sign_a = jnp.where(j < half_rot // 2, -1.0, 1.0).astype(F32)
    cosa = jnp.cos(ang_a)
    sina = jnp.sin(ang_a) * sign_a[None, :]
    inv_b = B_ROPE_THETA ** (-jnp.arange(0, B_ROPE_DIMS, 2, dtype=F32) / B_ROPE_DIMS)
    freq_b = inv_b[d % (B_ROPE_DIMS // 2)]
    ang_b = t.astype(F32)[:, None] * freq_b[None, :]
    rot = (d < B_ROPE_DIMS)[None, :]
    sign_b = jnp.where(d < B_ROPE_DIMS // 2, -1.0, 1.0).astype(F32)
    cosb = jnp.where(rot, jnp.cos(ang_b), 1.0)
    sinb = jnp.where(rot, jnp.sin(ang_b) * sign_b[None, :], 0.0)
    return cosa, sina, cosb, sinb


def _in_projection(x2, ln0_g, ln0_b, w_qkv, qn_g, kn_g, seq, tm):
    m = x2.shape[0]
    cosa, sina, cosb, sinb = _rope_tables(seq)
    tiles_per_seq = seq // tm
    row = lambda i: (i, 0)
    const = lambda i: (0, 0)
    table = lambda i: (i % tiles_per_seq, 0)
    widths = (A_Q_W, A_KV_W, A_KV_W) + (B_GROUP_W,) * 9
    gain_tile = lambda g: jnp.tile(g.reshape(1, HEAD_DIM), (1, LANES // HEAD_DIM))
    return pl.pallas_call(
        _inproj_kernel,
        grid=(m // tm,),
        in_specs=[
            pl.BlockSpec((tm, D_MODEL), row),
            pl.BlockSpec((1, D_MODEL), const),
            pl.BlockSpec((1, D_MODEL), const),
            pl.BlockSpec((D_MODEL, QKV_W), const),
            pl.BlockSpec((1, LANES), const),
            pl.BlockSpec((1, LANES), const),
            pl.BlockSpec((tm, LANES), table),
            pl.BlockSpec((tm, LANES), table),
            pl.BlockSpec((tm, LANES), table),
            pl.BlockSpec((tm, LANES), table),
        ],
        out_specs=([pl.BlockSpec((tm, w), row) for w in widths[:3]]
                   + [pl.BlockSpec((B_GROUP_W // LANES, tm, LANES), lambda i: (0, i, 0))] * 9),
        out_shape=([jax.ShapeDtypeStruct((m, w), BF16) for w in widths[:3]]
                   + [jax.ShapeDtypeStruct((B_GROUP_W // LANES, m, LANES), F32)] * 9),
        compiler_params=_params(1),
        name="in_projection",
    )(x2, ln0_g.reshape(1, D_MODEL), ln0_b.reshape(1, D_MODEL), w_qkv, gain_tile(qn_g), gain_tile(kn_g),
      cosa, sina, cosb, sinb)


def _attn_a_kernel(q_ref, k_ref, v_ref, wg_ref, wu_ref, wd_ref, o_ref, wgb_ref, wub_ref, wdb_ref):
    wgb_ref[...] = wg_ref[...].astype(BF16)
    wub_ref[...] = wu_ref[...].astype(BF16)
    wdb_ref[...] = wd_ref[...].astype(BF16)
    outs = []
    for kh in range(A_KV_HEADS):
        k = k_ref[0, :, kh * HEAD_DIM:(kh + 1) * HEAD_DIM]
        v = v_ref[0, :, kh * HEAD_DIM:(kh + 1) * HEAD_DIM]
        for g in range(A_GROUP):
            hq = kh * A_GROUP + g
            q = q_ref[0, :, hq * HEAD_DIM:(hq + 1) * HEAD_DIM]
            s = lax.dot_general(q, k, NT_DIMS, preferred_element_type=F32)
            p = jnp.exp2(s - jnp.max(s, axis=-1, keepdims=True))
            denom = jnp.sum(p, axis=-1, keepdims=True)
            o = jnp.dot(p.astype(BF16), v, preferred_element_type=F32)
            outs.append(o / denom)
    o_ref[0] = jnp.concatenate(outs, axis=1).astype(BF16)


def _attention_a(qa, ka, va, expert_weights, tq):
    b, s, _ = qa.shape
    steps = b * (s // tq)
    flat = [w.reshape(-1, w.shape[-1]) for w in expert_weights]
    wslice = lambda w: pl.BlockSpec((w.shape[0] // steps, w.shape[1]), lambda i, j: (i * (s // tq) + j, 0))
    o, *rounded = pl.pallas_call(
        _attn_a_kernel,
        grid=(b, s // tq),
        in_specs=[
            pl.BlockSpec((1, tq, A_Q_W), lambda i, j: (i, j, 0)),
            pl.BlockSpec((1, s, A_KV_W), lambda i, j: (i, 0, 0)),
            pl.BlockSpec((1, s, A_KV_W), lambda i, j: (i, 0, 0)),
            *map(wslice, flat),
        ],
        out_specs=[pl.BlockSpec((1, tq, A_Q_W), lambda i, j: (i, j, 0)), *map(wslice, flat)],
        out_shape=[jax.ShapeDtypeStruct((b, s, A_Q_W), BF16),
                   *[jax.ShapeDtypeStruct(w.shape, BF16) for w in flat]],
        compiler_params=_params(2),
        name="attention_a",
    )(qa, ka, va, *flat)
    return o, [r.reshape(w.shape) for r, w in zip(rounded, expert_weights)]


BAND_QB = 128
BAND_UNITS_PER_STEP = 4
BAND_STEPS = 4


def _dilated_kernel(q_ref, k_ref, v_ref, o_ref, lse_ref, *, n, dilation):
    win = min(2 * BAND_QB, n)
    n_blk = n // BAND_QB
    rows = B_HEADS_PER_GROUP * BAND_QB
    lane_head = lax.broadcasted_iota(jnp.int32, (1, B_GROUP_W), 1) // HEAD_DIM
    rel = (lax.broadcasted_iota(jnp.int32, (rows, win), 0) % BAND_QB
           - lax.broadcasted_iota(jnp.int32, (rows, win), 1))

    planes = B_GROUP_W // LANES

    def class_rows(r, start, size):
        if dilation == 1:
            return pl.ds(pl.multiple_of(start, BAND_HALF), size)
        return pl.ds(r + start * dilation, size, stride=dilation)

    def load(ref, rows_):
        return jnp.concatenate([ref[j, rows_, :] for j in range(planes)], axis=1)

    for i in range(BAND_UNITS_PER_STEP):
        unit = pl.program_id(1) * BAND_UNITS_PER_STEP + i
        r = unit // n_blk
        qs = (unit % n_blk) * BAND_QB
        ks = jnp.clip(qs - BAND_HALF, 0, n - win)
        q = load(q_ref, class_rows(r, qs, BAND_QB))
        k = load(k_ref, class_rows(r, ks, win)).astype(BF16)
        v = load(v_ref, class_rows(r, ks, win)).astype(BF16)
        q_stack = jnp.concatenate([jnp.where(lane_head == h, q, 0.0) for h in range(B_HEADS_PER_GROUP)], axis=0)
        s = lax.dot_general(q_stack.astype(BF16), k, NT_DIMS, preferred_element_type=F32)
        s = jnp.where(jnp.abs(rel + (qs - ks)) <= BAND_HALF, s, MASK_VALUE)
        mx = jnp.max(s, axis=-1, keepdims=True)
        p = jnp.exp(s - mx)
        denom = jnp.sum(p, axis=-1, keepdims=True)
        pv = jnp.dot(p.astype(BF16), v, preferred_element_type=F32) / denom
        lse = mx + jnp.log(denom)
        o = jnp.zeros((BAND_QB, B_GROUP_W), F32)
        l = jnp.zeros((BAND_QB, B_GROUP_W), F32)
        for h in range(B_HEADS_PER_GROUP):
            blk = slice(h * BAND_QB, (h + 1) * BAND_QB)
            o = jnp.where(lane_head == h, pv[blk], o)
            l = jnp.where(lane_head == h, lse[blk], l)
        for j in range(planes):
            o_ref[j, class_rows(r, qs, BAND_QB), :] = o[:, j * LANES:(j + 1) * LANES]
            lse_ref[j, class_rows(r, qs, BAND_QB), :] = l[:, j * LANES:(j + 1) * LANES]


def _dilated_attention(q, k, v, seq, dilation):
    planes, tokens, _ = q.shape
    n = seq // dilation
    assert dilation * (n // BAND_QB) == BAND_UNITS_PER_STEP * BAND_STEPS
    blk = pl.BlockSpec((planes, seq, LANES), lambda i, j: (0, i, 0))
    return pl.pallas_call(
        functools.partial(_dilated_kernel, n=n, dilation=dilation),
        grid=(tokens // seq, BAND_STEPS),
        in_specs=[blk, blk, blk],
        out_specs=[blk, blk],
        out_shape=[jax.ShapeDtypeStruct(q.shape, F32)] * 2,
        compiler_params=_params(2),
        name=f"dilated_attention_d{dilation}",
    )(q, k, v)


MERGE_SUB_ROWS = 512


def _merge_kernel(x_ref, g0_ref, b0_ref, wg_ref, bg_ref, oa_ref, o0_ref, o1_ref, o2_ref, l0_ref, l1_ref, l2_ref,
                  wa_ref, wb_ref, wo_ref, g1_ref, b1_ref, wr_hi_ref, wr_lo_ref,
                  h1_ref, h1b_ref, logit_ref):
    tm = x_ref.shape[0]
    sub = min(MERGE_SUB_ROWS, tm)
    for r0 in range(0, tm, sub):
        rows = slice(r0, r0 + sub)
        planes = lambda ref: jnp.concatenate([ref[j, rows, :] for j in range(ref.shape[0])], axis=1)
        h0 = _layer_norm(x_ref[rows, :], g0_ref[...], b0_ref[...])
        gates = jax.nn.sigmoid(jnp.dot(h0.astype(BF16), wg_ref[...], preferred_element_type=F32) + bg_ref[...])
        ya = jnp.dot(oa_ref[rows, :], wa_ref[...], preferred_element_type=F32)
        l0, l1, l2 = planes(l0_ref), planes(l1_ref), planes(l2_ref)
        mx = jnp.maximum(jnp.maximum(l0, l1), l2)
        e0, e1, e2 = jnp.exp(l0 - mx), jnp.exp(l1 - mx), jnp.exp(l2 - mx)
        ob = (e0 * planes(o0_ref) + e1 * planes(o1_ref) + e2 * planes(o2_ref)) / (e0 + e1 + e2)
        yb = jnp.dot(ob.astype(BF16), wb_ref[...], preferred_element_type=F32)
        merged = gates[:, :D_MODEL] * ya + gates[:, D_MODEL:] * yb
        mix = jnp.dot(merged.astype(BF16), wo_ref[...], preferred_element_type=F32)
        h1 = _layer_norm(DEEPNORM_ALPHA * h0 + mix, g1_ref[...], b1_ref[...])
        h1_ref[rows, :] = h1
        h1b_ref[rows, :] = h1.astype(BF16)
        hi, lo = _split_bf16(h1)
        logit_ref[0, :, rows] = (lax.dot_general(wr_hi_ref[...], hi, NT_DIMS, preferred_element_type=F32)
                                 + lax.dot_general(wr_hi_ref[...], lo, NT_DIMS, preferred_element_type=F32)
                                 + lax.dot_general(wr_lo_ref[...], hi, NT_DIMS, preferred_element_type=F32))


def _merge(x2, ln0_g, ln0_b, w_gates, b_gate, oa, obs, lses, w_a, w_b, w_o, ln1_g, ln1_b, w_router, seq, tm):
    m = x2.shape[0]
    tiles_per_seq = seq // tm
    row = lambda i: (i, 0)
    const = lambda i: (0, 0)
    wr_t = w_router.T
    wr_hi = wr_t.astype(BF16)
    wr_lo = (wr_t - wr_hi.astype(F32)).astype(BF16)
    vec = lambda v: v.reshape(1, -1)
    full = lambda a: pl.BlockSpec(a.shape, const)
    tile = lambda w: pl.BlockSpec((tm, w), row)
    args = [x2, vec(ln0_g), vec(ln0_b), w_gates, vec(b_gate), oa, *obs, *lses, w_a, w_b, w_o,
            vec(ln1_g), vec(ln1_b), wr_hi, wr_lo]
    in_specs = [tile(D_MODEL), full(args[1]), full(args[2]), full(w_gates), full(args[4]), tile(A_Q_W),
                *[pl.BlockSpec((B_GROUP_W // LANES, tm, LANES), lambda i: (0, i, 0))] * 6, full(w_a), full(w_b), full(w_o), full(args[15]), full(args[16]),
                full(wr_hi), full(wr_lo)]
    return pl.pallas_call(
        _merge_kernel,
        grid=(m // tm,),
        in_specs=in_specs,
        out_specs=[tile(D_MODEL), tile(D_MODEL),
                   pl.BlockSpec((1, N_EXPERTS, tm), lambda i: (i // tiles_per_seq, 0, i % tiles_per_seq))],
        out_shape=[jax.ShapeDtypeStruct((m, D_MODEL), F32), jax.ShapeDtypeStruct((m, D_MODEL), BF16),
                   jax.ShapeDtypeStruct((m // seq, N_EXPERTS, seq), F32)],
        compiler_params=_params(1),
        name="merge_ln1_router",
    )(*args)


PREFIX_CHUNK = 256


def _prefix_count(mask, tri):
    ones = jnp.where(mask, 1.0, 0.0)
    carry = jnp.zeros((mask.shape[0], 1), F32)
    outs, bounds = [], [carry]
    for j in range(mask.shape[1] // PREFIX_CHUNK):
        chunk = ones[:, j * PREFIX_CHUNK:(j + 1) * PREFIX_CHUNK]
        outs.append(jnp.dot(chunk.astype(BF16), tri, preferred_element_type=F32) + carry)
        carry = carry + jnp.sum(chunk, axis=1, keepdims=True)
        bounds.append(carry)
    return jnp.concatenate(outs, axis=1), bounds


THRESHOLD_BITS = 31
THRESHOLD_REFINE = 16


def _route_kernel(logit_ref, slot_ref, aff_ref, starts_ref, *, cap):
    n_seq = logit_ref.shape[0]
    affs = []
    for b in range(n_seq):
        lg = logit_ref[b]
        ex = jnp.exp(lg - jnp.max(lg, axis=0, keepdims=True))
        affs.append(ex / jnp.sum(ex, axis=0, keepdims=True))
    aff = jnp.concatenate(affs, axis=0)

    def count_ge(t):
        return jnp.sum(jnp.where(aff >= t, 1.0, 0.0), axis=1, keepdims=True)

    thr = jnp.zeros((aff.shape[0], 1), jnp.int32)
    for bit in range(THRESHOLD_BITS - 1, -1, -1):
        cand = thr | (1 << bit)
        thr = jnp.where(count_ge(pltpu.bitcast(cand, F32)) >= cap, cand, thr)
    lo = pltpu.bitcast(thr, F32)
    hi = pltpu.bitcast(thr + 1, F32)
    for _ in range(THRESHOLD_REFINE):
        mid = 0.5 * (lo + hi)
        take = count_ge(mid) >= cap
        lo = jnp.where(take, mid, lo)
        hi = jnp.where(take, hi, mid)
    above = aff >= hi
    tied = (aff >= lo) & (aff < hi)
    r = lax.broadcasted_iota(jnp.int32, (PREFIX_CHUNK, PREFIX_CHUNK), 0)
    c = lax.broadcasted_iota(jnp.int32, (PREFIX_CHUNK, PREFIX_CHUNK), 1)
    tri = jnp.where(r < c, 1.0, 0.0).astype(BF16)
    need = cap - jnp.sum(jnp.where(above, 1.0, 0.0), axis=1, keepdims=True)
    sel = above | (tied & (_prefix_count(tied, tri)[0] < need))
    rank, bounds = _prefix_count(sel, tri)
    slot = jnp.where(sel, rank, -1.0).astype(jnp.int32)
    lane = lax.broadcasted_iota(jnp.int32, (1, LANES), 1)
    starts = jnp.zeros((aff.shape[0], LANES), F32)
    for j, count in enumerate(bounds):
        starts = jnp.where(lane == j, count, starts)
    for b in range(n_seq):
        rows = slice(b * N_EXPERTS, (b + 1) * N_EXPERTS)
        slot_ref[b] = slot[rows]
        aff_ref[b] = aff[rows]
        starts_ref[b] = starts[rows].astype(jnp.int32)


ROUTE_SEQS_PER_STEP = 4


def _route(logits, batch, seq, cap):
    nb = ROUTE_SEQS_PER_STEP if batch % ROUTE_SEQS_PER_STEP == 0 else 1
    blk = pl.BlockSpec((nb, N_EXPERTS, seq), lambda i: (i, 0, 0))
    return pl.pallas_call(
        functools.partial(_route_kernel, cap=cap),
        grid=(batch // nb,),
        in_specs=[blk],
        out_specs=[blk, blk, pl.BlockSpec((nb, N_EXPERTS, LANES), lambda i: (i, 0, 0))],
        out_shape=[jax.ShapeDtypeStruct((batch, N_EXPERTS, seq), jnp.int32),
                   jax.ShapeDtypeStruct((batch, N_EXPERTS, seq), F32),
                   jax.ShapeDtypeStruct((batch, N_EXPERTS, LANES), jnp.int32)],
        compiler_params=_params(1),
        name="route",
    )(logits)


SUBLANES = 8
DISPATCH_WIN = 72


def _dispatch_kernel(starts_ref, h_ref, slot_ref, aff_ref, xe_ref, gate_ref, acc_ref, gacc_ref, *, cap):
    b, j = pl.program_id(0), pl.program_id(1)
    n_bounds = pl.num_programs(1) + 1
    win = DISPATCH_WIN

    @pl.when(j == 0)
    def _():
        acc_ref[...] = jnp.zeros_like(acc_ref)
        gacc_ref[...] = jnp.zeros_like(gacc_ref)

    base, n_win = [], 0
    for e in range(N_EXPERTS):
        at = (b * N_EXPERTS + e) * n_bounds + j
        first = (starts_ref[at] // SUBLANES) * SUBLANES
        base.append(first)
        n_win = jnp.maximum(n_win, (starts_ref[at + 1] - first + win - 1) // win)

    h = h_ref[0]
    wrow = lax.broadcasted_iota(jnp.int32, (win, h.shape[0]), 0)

    def window(k, carry):
        offs = [pl.multiple_of(jnp.minimum(base[e] + k * win, cap), SUBLANES) for e in range(N_EXPERTS)]
        onehots = [slot_ref[0, e:e + 1, :] == wrow + offs[e] for e in range(N_EXPERTS)]
        sel = jnp.concatenate([jnp.where(o, 1.0, 0.0) for o in onehots], axis=0).astype(BF16)
        rows = jnp.dot(sel, h, preferred_element_type=F32)
        for e in range(N_EXPERTS):
            at = pl.ds(offs[e], win)
            acc_ref[e, at, :] += rows[e * win:(e + 1) * win]
            gate = jnp.sum(jnp.where(onehots[e], aff_ref[0, e:e + 1, :], 0.0), axis=1, keepdims=True)
            gacc_ref[e, at, :] += jnp.broadcast_to(gate, (win, LANES))
        return carry

    lax.fori_loop(0, n_win, window, 0)

    @pl.when(j == pl.num_programs(1) - 1)
    def _():
        xe_ref[...] = acc_ref[:, :cap, :].astype(BF16)
        gate_ref[...] = gacc_ref[:, :cap, :]


def _dispatch(h1b, slot, aff, starts, cap):
    b, s, _ = h1b.shape
    n_chunks = s // PREFIX_CHUNK
    chunk = pl.BlockSpec((1, N_EXPERTS, PREFIX_CHUNK), lambda i, j, st: (i, 0, j))
    grid_spec = pltpu.PrefetchScalarGridSpec(
        num_scalar_prefetch=1,
        grid=(b, n_chunks),
        in_specs=[pl.BlockSpec((1, PREFIX_CHUNK, D_MODEL), lambda i, j, st: (i, j, 0)), chunk, chunk],
        out_specs=[pl.BlockSpec((N_EXPERTS, cap, D_MODEL), lambda i, j, st: (0, i, 0)),
                   pl.BlockSpec((N_EXPERTS, cap, LANES), lambda i, j, st: (0, i, 0))],
        scratch_shapes=[pltpu.VMEM((N_EXPERTS, cap + DISPATCH_WIN, D_MODEL), F32),
                        pltpu.VMEM((N_EXPERTS, cap + DISPATCH_WIN, LANES), F32)],
    )
    return pl.pallas_call(
        functools.partial(_dispatch_kernel, cap=cap),
        grid_spec=grid_spec,
        out_shape=[jax.ShapeDtypeStruct((N_EXPERTS, b * cap, D_MODEL), BF16),
                   jax.ShapeDtypeStruct((N_EXPERTS, b * cap, LANES), F32)],
        compiler_params=_params(2),
        name="dispatch",
    )(starts[:, :, :n_chunks + 1].reshape(-1), h1b, slot, aff)


FF_CHUNK = 512


def _expert_kernel(x_ref, gate_ref, wg_ref, wu_ref, wd_ref, y_ref):
    x = x_ref[0]
    acc = jnp.zeros((x.shape[0], D_MODEL), F32)
    for c in range(D_FF // FF_CHUNK):
        cols = slice(c * FF_CHUNK, (c + 1) * FF_CHUNK)
        g = jnp.dot(x, wg_ref[0, :, cols], preferred_element_type=F32)
        u = jnp.dot(x, wu_ref[0, :, cols], preferred_element_type=F32)
        act = (g * jax.nn.sigmoid(g) * u).astype(BF16)
        acc = acc + jnp.dot(act, wd_ref[0, cols, :], preferred_element_type=F32)
    y_ref[0] = (acc * gate_ref[0][:, :1]).astype(BF16)


def _expert_ffn(xe, gate, w_gate, w_up, w_down, tm):
    e, rows, _ = xe.shape
    tok = lambda w: pl.BlockSpec((1, tm, w), lambda i, j: (i, j, 0))
    wspec = lambda a: pl.BlockSpec((1,) + a.shape[1:], lambda i, j: (i, 0, 0))
    return pl.pallas_call(
        _expert_kernel,
        grid=(e, rows // tm),
        in_specs=[tok(D_MODEL), tok(LANES), wspec(w_gate), wspec(w_up), wspec(w_down)],
        out_specs=tok(D_MODEL),
        out_shape=jax.ShapeDtypeStruct((e, rows, D_MODEL), BF16),
        compiler_params=_params(2),
        name="expert_ffn",
    )(xe, gate, w_gate, w_up, w_down)


BF16_SUBLANES = 16
COMBINE_WIN = 80
TN_DIMS = (((0,), (0,)), ((), ()))


def _combine_kernel(starts_ref, h1_ref, slot_ref, y_ref, g2_ref, b2_ref, out_ref, acc_ref, *, cap):
    b, j = pl.program_id(0), pl.program_id(1)
    n_bounds = pl.num_programs(1) + 1
    win = COMBINE_WIN
    base, n_win = [], 0
    for e in range(N_EXPERTS):
        at = (b * N_EXPERTS + e) * n_bounds + j
        first = (starts_ref[at] // BF16_SUBLANES) * BF16_SUBLANES
        base.append(first)
        n_win = jnp.maximum(n_win, (starts_ref[at + 1] - first + win - 1) // win)

    acc_ref[...] = jnp.zeros_like(acc_ref)
    wrow = lax.broadcasted_iota(jnp.int32, (win, slot_ref.shape[2]), 0)

    def window(k, carry):
        sels, rows = [], []
        for e in range(N_EXPERTS):
            lower = base[e] + k * win
            off = pl.multiple_of(jnp.minimum(lower, cap - win), BF16_SUBLANES)
            target = jnp.where(wrow + off >= lower, wrow + off, -2)
            sels.append(jnp.where(slot_ref[0, e:e + 1, :] == target, 1.0, 0.0))
            rows.append(y_ref[e, pl.ds(off, win), :])
        sel = jnp.concatenate(sels, axis=0).astype(BF16)
        acc_ref[...] += lax.dot_general(sel, jnp.concatenate(rows, axis=0), TN_DIMS, preferred_element_type=F32)
        return carry

    lax.fori_loop(0, n_win, window, 0)
    out_ref[0] = _layer_norm(DEEPNORM_ALPHA * h1_ref[0] + acc_ref[...], g2_ref[...], b2_ref[...])


def _combine(h1, slot, starts, y, ln2_g, ln2_b, cap):
    b, s, _ = h1.shape
    n_chunks = s // PREFIX_CHUNK
    const = lambda i, j, st: (0, 0)
    grid_spec = pltpu.PrefetchScalarGridSpec(
        num_scalar_prefetch=1,
        grid=(b, n_chunks),
        in_specs=[pl.BlockSpec((1, PREFIX_CHUNK, D_MODEL), lambda i, j, st: (i, j, 0)),
                  pl.BlockSpec((1, N_EXPERTS, PREFIX_CHUNK), lambda i, j, st: (i, 0, j)),
                  pl.BlockSpec((N_EXPERTS, cap, D_MODEL), lambda i, j, st: (0, i, 0)),
                  pl.BlockSpec((1, D_MODEL), const), pl.BlockSpec((1, D_MODEL), const)],
        out_specs=pl.BlockSpec((1, PREFIX_CHUNK, D_MODEL), lambda i, j, st: (i, j, 0)),
        scratch_shapes=[pltpu.VMEM((PREFIX_CHUNK, D_MODEL), F32)],
    )
    return pl.pallas_call(
        functools.partial(_combine_kernel, cap=cap),
        grid_spec=grid_spec,
        out_shape=jax.ShapeDtypeStruct((b, s, D_MODEL), F32),
        compiler_params=_params(2),
        name="combine_ln2",
    )(starts[:, :, :n_chunks + 1].reshape(-1), h1, slot, y, ln2_g.reshape(1, D_MODEL), ln2_b.reshape(1, D_MODEL))


def kernel(x, ln0_g, ln0_b, w_in, b_gate, qn_g, kn_g, w_branch_a, w_branch_b, w_out, ln1_g, ln1_b, w_router,
           w_gate_e, w_up_e, w_down_e, ln2_g, ln2_b):
    batch, seq, _ = x.shape
    assert w_in.shape[0] == DEPTH
    cap = EC_CAPACITY_FACTOR * seq // N_EXPERTS
    tokens = batch * seq
    x2 = x.reshape(tokens, D_MODEL)
    w_in_b = w_in[0].astype(BF16)

    qkv = _in_projection(x2, ln0_g, ln0_b, w_in_b[:, :QKV_W], qn_g[0], kn_g[0], seq, tm=512)
    qa, ka, va = qkv[:3]
    qb, kb, vb = qkv[3:6], qkv[6:9], qkv[9:12]

    oa, (wg_b, wu_b, wd_b) = _attention_a(qa.reshape(batch, seq, A_Q_W), ka.reshape(batch, seq, A_KV_W),
                                          va.reshape(batch, seq, A_KV_W), (w_gate_e[0], w_up_e[0], w_down_e[0]),
                                          tq=256)
    oa = oa.reshape(tokens, A_Q_W)

    obs, lses = [], []
    for gi, (window, dilation) in enumerate(B_GROUPS):
        assert window // (2 * dilation) == BAND_HALF
        o, lse = _dilated_attention(qb[gi], kb[gi], vb[gi], seq, dilation)
        obs.append(o)
        lses.append(lse)

    h1, h1b, logits = _merge(x2, ln0_g, ln0_b, w_in_b[:, QKV_W:], b_gate[0], oa, obs, lses,
                             w_branch_a[0].astype(BF16), w_branch_b[0].astype(BF16), w_out[0].astype(BF16),
                             ln1_g[0], ln1_b[0], w_router[0], seq, tm=512)

    slot, aff, starts = _route(logits, batch, seq, cap)
    xe, gate = _dispatch(h1b.reshape(batch, seq, D_MODEL), slot, aff, starts, cap)
    y = _expert_ffn(xe, gate, wg_b, wu_b, wd_b, tm=min(1024, batch * cap))
    return _combine(h1.reshape(batch, seq, D_MODEL), slot, starts, y, ln2_g[0], ln2_b[0], cap)
```

```python
import functools

import jax
import jax.numpy as jnp
from jax import lax
from jax.experimental import pallas as pl
from jax.experimental.pallas import tpu as pltpu

F32 = jnp.float32
BF16 = jnp.bfloat16

D_MODEL = 1024
HEAD_DIM = 64
A_Q_HEADS = 8
A_KV_HEADS = 2
A_GROUP = A_Q_HEADS // A_KV_HEADS
B_GROUPS = ((128, 1), (512, 4), (2048, 16))
B_HEADS_PER_GROUP = 4
GRID_W = 64
A_ROPE_THETA = 10000.0
B_ROPE_THETA = 500000.0
B_ROPE_DIMS = HEAD_DIM // 4
N_EXPERTS = 16
EC_CAPACITY_FACTOR = 2
D_FF = 2 * D_MODEL
LN_EPS = 1e-5
QK_NORM_EPS = 1e-6
MASK_VALUE = -1e30
DEPTH = 1
DEEPNORM_ALPHA = (2.0 * DEPTH) ** 0.25
LOG2_E = 1.4426950408889634
Q_SCALE = HEAD_DIM ** -0.5 * LOG2_E

A_Q_W = A_Q_HEADS * HEAD_DIM
A_KV_W = A_KV_HEADS * HEAD_DIM
B_GROUP_W = B_HEADS_PER_GROUP * HEAD_DIM
B_W = B_GROUP_W * len(B_GROUPS)
QKV_W = A_Q_W + 2 * A_KV_W + 3 * B_W
BAND_HALF = 64

LANES = 128
VMEM_LIMIT = 56 * 1024 * 1024

NT_DIMS = (((1,), (1,)), ((), ()))


def _params(n_axes):
    return pltpu.CompilerParams(dimension_semantics=("arbitrary",) * n_axes, vmem_limit_bytes=VMEM_LIMIT)


def _layer_norm(x, g, b):
    mu = jnp.mean(x, axis=-1, keepdims=True)
    xc = x - mu
    var = jnp.mean(xc * xc, axis=-1, keepdims=True)
    return xc * lax.rsqrt(var + LN_EPS) * g + b


def _split_bf16(x):
    hi = x.astype(BF16)
    lo = (x - hi.astype(F32)).astype(BF16)
    return hi, lo


def _swap_halves(y, lane, period, half):
    fwd = pltpu.roll(y, LANES - half, 1)
    bwd = pltpu.roll(y, half, 1)
    return jnp.where((lane % period) < half, fwd, bwd)


def _inproj_kernel(x_ref, g0_ref, b0_ref, w_ref, qng_ref, kng_ref, cosa_ref, sina_ref, cosb_ref, sinb_ref,
                   qa_ref, ka_ref, va_ref, qb0_ref, qb1_ref, qb2_ref, kb0_ref, kb1_ref, kb2_ref,
                   vb0_ref, vb1_ref, vb2_ref):
    h = _layer_norm(x_ref[...], g0_ref[...], b0_ref[...]).astype(BF16)
    lane = lax.broadcasted_iota(jnp.int32, (1, LANES), 1)
    r = lax.broadcasted_iota(jnp.int32, (LANES, LANES), 0)
    c = lax.broadcasted_iota(jnp.int32, (LANES, LANES), 1)
    head_sum = jnp.where((r // HEAD_DIM) == (c // HEAD_DIM), 1.0, 0.0).astype(BF16)
    cosa, sina = cosa_ref[...], sina_ref[...]
    cosb, sinb = cosb_ref[...], sinb_ref[...]

    def proj(off):
        y = jnp.dot(h, w_ref[:, off:off + 2 * LANES], preferred_element_type=F32)
        return y[:, :LANES], y[:, LANES:]

    def norm_rope_a(y, gain, scale):
        hi, lo = _split_bf16(y * y)
        ss = (jnp.dot(hi, head_sum, preferred_element_type=F32)
              + jnp.dot(lo, head_sum, preferred_element_type=F32))
        y = y * lax.rsqrt(ss * (1.0 / HEAD_DIM) + QK_NORM_EPS) * gain
        y = y * cosa + _swap_halves(y, lane, HEAD_DIM // 2, HEAD_DIM // 4) * sina
        return (y * scale).astype(BF16)

    def rope_b(y, scale):
        y = y * cosb + _swap_halves(y, lane, HEAD_DIM, B_ROPE_DIMS // 2) * sinb
        return y * scale

    for j in range(A_Q_W // (2 * LANES)):
        for i, y in enumerate(proj(2 * j * LANES)):
            qa_ref[:, (2 * j + i) * LANES:(2 * j + i + 1) * LANES] = norm_rope_a(y, qng_ref[...], Q_SCALE)
    ka, va = proj(A_Q_W)
    ka_ref[...] = norm_rope_a(ka, kng_ref[...], 1.0)
    va_ref[...] = va.astype(BF16)
    base = A_Q_W + 2 * A_KV_W
    for gi, (q_ref, k_ref, v_ref) in enumerate(((qb0_ref, kb0_ref, vb0_ref), (qb1_ref, kb1_ref, vb1_ref),
                                                (qb2_ref, kb2_ref, vb2_ref))):
        off = base + gi * B_GROUP_W
        for j, (q, k, v) in enumerate(zip(proj(off), proj(off + B_W), proj(off + 2 * B_W))):
            q_ref[j] = rope_b(q, Q_SCALE)
            k_ref[j] = rope_b(k, 1.0)
            v_ref[j] = v


def _rope_tables(seq):
    lane = jnp.arange(LANES)
    d = lane % HEAD_DIM
    t = jnp.arange(seq)
    half_rot = HEAD_DIM // 2
    inv_a = A_ROPE_THETA ** (-jnp.arange(0, half_rot, 2, dtype=F32) / half_rot)
    j = d % half_rot
    pos_row = (t // GRID_W).astype(F32)
    pos_col = (t % GRID_W).astype(F32)
    freq_a = inv_a[j % (half_rot // 2)]
    ang_a = jnp.where((d < half_rot)[None, :], pos_row[:, None], pos_col[:, None]) * freq_a[None, :]
    sign_a = jnp.where(j < half_rot // 2, -1.0, 1.0).astype(F32)
    cosa = jnp.cos(ang_a)
    sina = jnp.sin(ang_a) * sign_a[None, :]
    inv_b = B_ROPE_THETA ** (-jnp.arange(0, B_ROPE_DIMS, 2, dtype=F32) / B_ROPE_DIMS)
    freq_b = inv_b[d % (B_ROPE_DIMS // 2)]
    ang_b = t.astype(F32)[:, None] * freq_b[None, :]
    rot = (d < B_ROPE_DIMS)[None, :]
    sign_b = jnp.where(d < B_ROPE_DIMS // 2, -1.0, 1.0).astype(F32)
    cosb = jnp.where(rot, jnp.cos(ang_b), 1.0)
    sinb = jnp.where(rot, jnp.sin(ang_b) * sign_b[None, :], 0.0)
    return cosa, sina, cosb, sinb


def _in_projection(x2, ln0_g, ln0_b, w_qkv, qn_g, kn_g, seq, tm):
    m = x2.shape[0]
    cosa, sina, cosb, sinb = _rope_tables(seq)
    tiles_per_seq = seq // tm
    row = lambda i: (i, 0)
    const = lambda i: (0, 0)
    table = lambda i: (i % tiles_per_seq, 0)
    widths = (A_Q_W, A_KV_W, A_KV_W) + (B_GROUP_W,) * 9
    gain_tile = lambda g: jnp.tile(g.reshape(1, HEAD_DIM), (1, LANES // HEAD_DIM))
    return pl.pallas_call(
        _inproj_kernel,
        grid=(m // tm,),
        in_specs=[
            pl.BlockSpec((tm, D_MODEL), row),
            pl.BlockSpec((1, D_MODEL), const),
            pl.BlockSpec((1, D_MODEL), const),
            pl.BlockSpec((D_MODEL, QKV_W), const),
            pl.BlockSpec((1, LANES), const),
            pl.BlockSpec((1, LANES), const),
            pl.BlockSpec((tm, LANES), table),
            pl.BlockSpec((tm, LANES), table),
            pl.BlockSpec((tm, LANES), table),
            pl.BlockSpec((tm, LANES), table),
        ],
        out_specs=([pl.BlockSpec((tm, w), row) for w in widths[:3]]
                   + [pl.BlockSpec((B_GROUP_W // LANES, tm, LANES), lambda i: (0, i, 0))] * 9),
        out_shape=([jax.ShapeDtypeStruct((m, w), BF16) for w in widths[:3]]
                   + [jax.ShapeDtypeStruct((B_GROUP_W // LANES, m, LANES), F32)] * 9),
        compiler_params=_params(1),
        name="in_projection",
    )(x2, ln0_g.reshape(1, D_MODEL), ln0_b.reshape(1, D_MODEL), w_qkv, gain_tile(qn_g), gain_tile(kn_g),
      cosa, sina, cosb, sinb)


def _attn_a_kernel(q_ref, k_ref, v_ref, wg_ref, wu_ref, wd_ref, o_ref, wgb_ref, wub_ref, wdb_ref):
    wgb_ref[...] = wg_ref[...].astype(BF16)
    wub_ref[...] = wu_ref[...].astype(BF16)
    wdb_ref[...] = wd_ref[...].astype(BF16)
    outs = []
    for kh in range(A_KV_HEADS):
        k = k_ref[0, :, kh * HEAD_DIM:(kh + 1) * HEAD_DIM]
        v = v_ref[0, :, kh * HEAD_DIM:(kh + 1) * HEAD_DIM]
        for g in range(A_GROUP):
            hq = kh * A_GROUP + g
            q = q_ref[0, :, hq * HEAD_DIM:(hq + 1) * HEAD_DIM]
            s = lax.dot_general(q, k, NT_DIMS, preferred_element_type=F32)
            p = jnp.exp2(s - jnp.max(s, axis=-1, keepdims=True))
            denom = jnp.sum(p, axis=-1, keepdims=True)
            o = jnp.dot(p.astype(BF16), v, preferred_element_type=F32)
            outs.append(o / denom)
    o_ref[0] = jnp.concatenate(outs, axis=1).astype(BF16)


def _attention_a(qa, ka, va, expert_weights, tq):
    b, s, _ = qa.shape
    steps = b * (s // tq)
    flat = [w.reshape(-1, w.shape[-1]) for w in expert_weights]
    wslice = lambda w: pl.BlockSpec((w.shape[0] // steps, w.shape[1]), lambda i, j: (i * (s // tq) + j, 0))
    o, *rounded = pl.pallas_call(
        _attn_a_kernel,
        grid=(b, s // tq),
        in_specs=[
            pl.BlockSpec((1, tq, A_Q_W), lambda i, j: (i, j, 0)),
            pl.BlockSpec((1, s, A_KV_W), lambda i, j: (i, 0, 0)),
            pl.BlockSpec((1, s, A_KV_W), lambda i, j: (i, 0, 0)),
            *map(wslice, flat),
        ],
        out_specs=[pl.BlockSpec((1, tq, A_Q_W), lambda i, j: (i, j, 0)), *map(wslice, flat)],
        out_shape=[jax.ShapeDtypeStruct((b, s, A_Q_W), BF16),
                   *[jax.ShapeDtypeStruct(w.shape, BF16) for w in flat]],
        compiler_params=_params(2),
        name="attention_a",
    )(qa, ka, va, *flat)
    return o, [r.reshape(w.shape) for r, w in zip(rounded, expert_weights)]


BAND_QB = 128
BAND_UNITS_PER_STEP = 8
BAND_STEPS = 2


def _band_bias(n):
    win = min(2 * BAND_QB, n)
    rows = B_HEADS_PER_GROUP * BAND_QB
    rel = (jnp.arange(rows)[:, None] % BAND_QB) - jnp.arange(win)[None, :]
    shifts = jnp.arange(2 * BAND_QB // BAND_HALF - 1) * BAND_HALF
    inside = jnp.abs(rel[None] + shifts[:, None, None]) <= BAND_HALF
    return jnp.where(inside, 0.0, MASK_VALUE).astype(F32)


def _dilated_kernel(q_ref, k_ref, v_ref, bias_ref, o_ref, lse_ref, *, n, dilation):
    win = min(2 * BAND_QB, n)
    n_blk = n // BAND_QB
    lane_head = lax.broadcasted_iota(jnp.int32, (1, B_GROUP_W), 1) // HEAD_DIM

    planes = B_GROUP_W // LANES

    def class_rows(r, start, size):
        if dilation == 1:
            return pl.ds(pl.multiple_of(start, BAND_HALF), size)
        return pl.ds(r + start * dilation, size, stride=dilation)

    def load(ref, rows_):
        return jnp.concatenate([ref[j, rows_, :] for j in range(planes)], axis=1)

    for i in range(BAND_UNITS_PER_STEP):
        unit = pl.program_id(1) * BAND_UNITS_PER_STEP + i
        r = unit // n_blk
        qs = (unit % n_blk) * BAND_QB
        ks = jnp.clip(qs - BAND_HALF, 0, n - win)
        q = load(q_ref, class_rows(r, qs, BAND_QB))
        k = load(k_ref, class_rows(r, ks, win)).astype(BF16)
        v = load(v_ref, class_rows(r, ks, win)).astype(BF16)
        q_stack = jnp.concatenate([jnp.where(lane_head == h, q, 0.0) for h in range(B_HEADS_PER_GROUP)], axis=0)
        s = lax.dot_general(q_stack.astype(BF16), k, NT_DIMS, preferred_element_type=F32)
        s = s + bias_ref[(qs - ks) // BAND_HALF]
        mx = jnp.max(s, axis=-1, keepdims=True)
        p = jnp.exp2(s - mx)
        denom = jnp.sum(p, axis=-1, keepdims=True)
        pv = jnp.dot(p.astype(BF16), v, preferred_element_type=F32) / denom
        lse = mx + jnp.log2(denom)
        o = jnp.zeros((BAND_QB, B_GROUP_W), F32)
        l = jnp.zeros((BAND_QB, B_GROUP_W), F32)
        for h in range(B_HEADS_PER_GROUP):
            blk = slice(h * BAND_QB, (h + 1) * BAND_QB)
            o = jnp.where(lane_head == h, pv[blk], o)
            l = jnp.where(lane_head == h, lse[blk], l)
        for j in range(planes):
            o_ref[j, class_rows(r, qs, BAND_QB), :] = o[:, j * LANES:(j + 1) * LANES]
            lse_ref[j, class_rows(r, qs, BAND_QB), :] = l[:, j * LANES:(j + 1) * LANES]


def _dilated_attention(q, k, v, seq, dilation):
    planes, tokens, _ = q.shape
    n = seq // dilation
    assert dilation * (n // BAND_QB) == BAND_UNITS_PER_STEP * BAND_STEPS
    blk = pl.BlockSpec((planes, seq, LANES), lambda i, j: (0, i, 0))
    bias = _band_bias(n)
    return pl.pallas_call(
        functools.partial(_dilated_kernel, n=n, dilation=dilation),
        grid=(tokens // seq, BAND_STEPS),
        in_specs=[blk, blk, blk, pl.BlockSpec(bias.shape, lambda i, j: (0, 0, 0))],
        out_specs=[blk, blk],
        out_shape=[jax.ShapeDtypeStruct(q.shape, F32)] * 2,
        compiler_params=_params(2),
        name=f"dilated_attention_d{dilation}",
    )(q, k, v, bias)


MERGE_SUB_ROWS = 512


def _merge_kernel(x_ref, g0_ref, b0_ref, wg_ref, bg_ref, oa_ref, o0_ref, o1_ref, o2_ref, l0_ref, l1_ref, l2_ref,
                  wa_ref, wb_ref, wo_ref, g1_ref, b1_ref, wr_hi_ref, wr_lo_ref,
                  h1_ref, h1b_ref, logit_ref):
    tm = x_ref.shape[0]
    sub = min(MERGE_SUB_ROWS, tm)
    for r0 in range(0, tm, sub):
        rows = slice(r0, r0 + sub)
        planes = lambda ref: jnp.concatenate([ref[j, rows, :] for j in range(ref.shape[0])], axis=1)
        h0 = _layer_norm(x_ref[rows, :], g0_ref[...], b0_ref[...])
        gates = jax.nn.sigmoid(jnp.dot(h0.astype(BF16), wg_ref[...], preferred_element_type=F32) + bg_ref[...])
        ya = jnp.dot(oa_ref[rows, :], wa_ref[...], preferred_element_type=F32)
        l0, l1, l2 = planes(l0_ref), planes(l1_ref), planes(l2_ref)
        mx = jnp.maximum(jnp.maximum(l0, l1), l2)
        e0, e1, e2 = jnp.exp2(l0 - mx), jnp.exp2(l1 - mx), jnp.exp2(l2 - mx)
        ob = (e0 * planes(o0_ref) + e1 * planes(o1_ref) + e2 * planes(o2_ref)) / (e0 + e1 + e2)
        yb = jnp.dot(ob.astype(BF16), wb_ref[...], preferred_element_type=F32)
        merged = gates[:, :D_MODEL] * ya + gates[:, D_MODEL:] * yb
        mix = jnp.dot(merged.astype(BF16), wo_ref[...], preferred_element_type=F32)
        h1 = _layer_norm(DEEPNORM_ALPHA * h0 + mix, g1_ref[...], b1_ref[...])
        h1_ref[rows, :] = h1
        h1b_ref[rows, :] = h1.astype(BF16)
        hi, lo = _split_bf16(h1)
        logit_ref[0, :, rows] = (lax.dot_general(wr_hi_ref[...], hi, NT_DIMS, preferred_element_type=F32)
                                 + lax.dot_general(wr_hi_ref[...], lo, NT_DIMS, preferred_element_type=F32)
                                 + lax.dot_general(wr_lo_ref[...], hi, NT_DIMS, preferred_element_type=F32))


def _merge(x2, ln0_g, ln0_b, w_gates, b_gate, oa, obs, lses, w_a, w_b, w_o, ln1_g, ln1_b, w_router, seq, tm):
    m = x2.shape[0]
    tiles_per_seq = seq // tm
    row = lambda i: (i, 0)
    const = lambda i: (0, 0)
    wr_t = w_router.T
    wr_hi = wr_t.astype(BF16)
    wr_lo = (wr_t - wr_hi.astype(F32)).astype(BF16)
    vec = lambda v: v.reshape(1, -1)
    full = lambda a: pl.BlockSpec(a.shape, const)
    tile = lambda w: pl.BlockSpec((tm, w), row)
    args = [x2, vec(ln0_g), vec(ln0_b), w_gates, vec(b_gate), oa, *obs, *lses, w_a, w_b, w_o,
            vec(ln1_g), vec(ln1_b), wr_hi, wr_lo]
    in_specs = [tile(D_MODEL), full(args[1]), full(args[2]), full(w_gates), full(args[4]), tile(A_Q_W),
                *[pl.BlockSpec((B_GROUP_W // LANES, tm, LANES), lambda i: (0, i, 0))] * 6, full(w_a), full(w_b), full(w_o), full(args[15]), full(args[16]),
                full(wr_hi), full(wr_lo)]
    return pl.pallas_call(
        _merge_kernel,
        grid=(m // tm,),
        in_specs=in_specs,
        out_specs=[tile(D_MODEL), tile(D_MODEL),
                   pl.BlockSpec((1, N_EXPERTS, tm), lambda i: (i // tiles_per_seq, 0, i % tiles_per_seq))],
        out_shape=[jax.ShapeDtypeStruct((m, D_MODEL), F32), jax.ShapeDtypeStruct((m, D_MODEL), BF16),
                   jax.ShapeDtypeStruct((m // seq, N_EXPERTS, seq), F32)],
        compiler_params=_params(1),
        name="merge_ln1_router",
    )(*args)


PREFIX_CHUNK = 256


def _prefix_count(mask, tri):
    ones = jnp.where(mask, 1.0, 0.0)
    carry = jnp.zeros((mask.shape[0], 1), F32)
    outs, bounds = [], [carry]
    for j in range(mask.shape[1] // PREFIX_CHUNK):
        chunk = ones[:, j * PREFIX_CHUNK:(j + 1) * PREFIX_CHUNK]
        outs.append(jnp.dot(chunk.astype(BF16), tri, preferred_element_type=F32) + carry)
        carry = carry + jnp.sum(chunk, axis=1, keepdims=True)
        bounds.append(carry)
    return jnp.concatenate(outs, axis=1), bounds


THRESHOLD_BITS = 31
THRESHOLD_REFINE = 16


def _route_kernel(logit_ref, slot_ref, aff_ref, starts_ref, *, cap):
    n_seq = logit_ref.shape[0]
    affs = []
    for b in range(n_seq):
        lg = logit_ref[b]
        ex = jnp.exp(lg - jnp.max(lg, axis=0, keepdims=True))
        affs.append(ex / jnp.sum(ex, axis=0, keepdims=True))
    aff = jnp.concatenate(affs, axis=0)

    def count_ge(t):
        return jnp.sum(jnp.where(aff >= t, 1.0, 0.0), axis=1, keepdims=True)

    thr = jnp.zeros((aff.shape[0], 1), jnp.int32)
    for bit in range(THRESHOLD_BITS - 1, -1, -1):
        cand = thr | (1 << bit)
        thr = jnp.where(count_ge(pltpu.bitcast(cand, F32)) >= cap, cand, thr)
    lo = pltpu.bitcast(thr, F32)
    hi = pltpu.bitcast(thr + 1, F32)
    for _ in range(THRESHOLD_REFINE):
        mid = 0.5 * (lo + hi)
        take = count_ge(mid) >= cap
        lo = jnp.where(take, mid, lo)
        hi = jnp.where(take, hi, mid)
    above = aff >= hi
    tied = (aff >= lo) & (aff < hi)
    r = lax.broadcasted_iota(jnp.int32, (PREFIX_CHUNK, PREFIX_CHUNK), 0)
    c = lax.broadcasted_iota(jnp.int32, (PREFIX_CHUNK, PREFIX_CHUNK), 1)
    tri = jnp.where(r < c, 1.0, 0.0).astype(BF16)
    need = cap - jnp.sum(jnp.where(above, 1.0, 0.0), axis=1, keepdims=True)
    sel = above | (tied & (_prefix_count(tied, tri)[0] < need))
    rank, bounds = _prefix_count(sel, tri)
    slot = jnp.where(sel, rank, -1.0).astype(jnp.int32)
    lane = lax.broadcasted_iota(jnp.int32, (1, LANES), 1)
    starts = jnp.zeros((aff.shape[0], LANES), F32)
    for j, count in enumerate(bounds):
        starts = jnp.where(lane == j, count, starts)
    for b in range(n_seq):
        rows = slice(b * N_EXPERTS, (b + 1) * N_EXPERTS)
        slot_ref[b] = slot[rows]
        aff_ref[b] = aff[rows]
        starts_ref[b] = starts[rows].astype(jnp.int32)


ROUTE_SEQS_PER_STEP = 4


def _route(logits, batch, seq, cap):
    nb = ROUTE_SEQS_PER_STEP if batch % ROUTE_SEQS_PER_STEP == 0 else 1
    blk = pl.BlockSpec((nb, N_EXPERTS, seq), lambda i: (i, 0, 0))
    return pl.pallas_call(
        functools.partial(_route_kernel, cap=cap),
        grid=(batch // nb,),
        in_specs=[blk],
        out_specs=[blk, blk, pl.BlockSpec((nb, N_EXPERTS, LANES), lambda i: (i, 0, 0))],
        out_shape=[jax.ShapeDtypeStruct((batch, N_EXPERTS, seq), jnp.int32),
                   jax.ShapeDtypeStruct((batch, N_EXPERTS, seq), F32),
                   jax.ShapeDtypeStruct((batch, N_EXPERTS, LANES), jnp.int32)],
        compiler_params=_params(1),
        name="route",
    )(logits)


SUBLANES = 8
DISPATCH_WIN = 56


def _dispatch_kernel(starts_ref, h_ref, slot_ref, aff_ref, xe_ref, gate_ref, acc_ref, gacc_ref, *, cap):
    b, j = pl.program_id(0), pl.program_id(1)
    n_bounds = pl.num_programs(1) + 1
    win = DISPATCH_WIN

    @pl.when(j == 0)
    def _():
        acc_ref[...] = jnp.zeros_like(acc_ref)
        gacc_ref[...] = jnp.zeros_like(gacc_ref)

    base, n_win = [], 0
    for e in range(N_EXPERTS):
        at = (b * N_EXPERTS + e) * n_bounds + j
        first = (starts_ref[at] // SUBLANES) * SUBLANES
        base.append(first)
        n_win = jnp.maximum(n_win, (starts_ref[at + 1] - first + win - 1) // win)

    h = h_ref[0]
    wrow = lax.broadcasted_iota(jnp.int32, (win, h.shape[0]), 0)

    def window(k, carry):
        offs = [pl.multiple_of(jnp.minimum(base[e] + k * win, cap), SUBLANES) for e in range(N_EXPERTS)]
        onehots = [slot_ref[0, e:e + 1, :] == wrow + offs[e] for e in range(N_EXPERTS)]
        sel = jnp.concatenate([jnp.where(o, 1.0, 0.0) for o in onehots], axis=0).astype(BF16)
        rows = jnp.dot(sel, h, preferred_element_type=F32)
        for e in range(N_EXPERTS):
            at = pl.ds(offs[e], win)
            acc_ref[e, at, :] += rows[e * win:(e + 1) * win]
            gate = jnp.sum(jnp.where(onehots[e], aff_ref[0, e:e + 1, :], 0.0), axis=1, keepdims=True)
            gacc_ref[e, at, :] += jnp.broadcast_to(gate, (win, LANES))
        return carry

    lax.fori_loop(0, n_win, window, 0)

    @pl.when(j == pl.num_programs(1) - 1)
    def _():
        xe_ref[...] = acc_ref[:, :cap, :].astype(BF16)
        gate_ref[...] = gacc_ref[:, :cap, :]


def _dispatch(h1b, slot, aff, starts, cap):
    b, s, _ = h1b.shape
    n_chunks = s // PREFIX_CHUNK
    chunk = pl.BlockSpec((1, N_EXPERTS, PREFIX_CHUNK), lambda i, j, st: (i, 0, j))
    grid_spec = pltpu.PrefetchScalarGridSpec(
        num_scalar_prefetch=1,
        grid=(b, n_chunks),
        in_specs=[pl.BlockSpec((1, PREFIX_CHUNK, D_MODEL), lambda i, j, st: (i, j, 0)), chunk, chunk],
        out_specs=[pl.BlockSpec((N_EXPERTS, cap, D_MODEL), lambda i, j, st: (0, i, 0)),
                   pl.BlockSpec((N_EXPERTS, cap, LANES), lambda i, j, st: (0, i, 0))],
        scratch_shapes=[pltpu.VMEM((N_EXPERTS, cap + DISPATCH_WIN, D_MODEL), F32),
                        pltpu.VMEM((N_EXPERTS, cap + DISPATCH_WIN, LANES), F32)],
    )
    return pl.pallas_call(
        functools.partial(_dispatch_kernel, cap=cap),
        grid_spec=grid_spec,
        out_shape=[jax.ShapeDtypeStruct((N_EXPERTS, b * cap, D_MODEL), BF16),
                   jax.ShapeDtypeStruct((N_EXPERTS, b * cap, LANES), F32)],
        compiler_params=_params(2),
        name="dispatch",
    )(starts[:, :, :n_chunks + 1].reshape(-1), h1b, slot, aff)


FF_CHUNK = 512


def _expert_kernel(x_ref, gate_ref, wg_ref, wu_ref, wd_ref, y_ref):
    x = x_ref[0]
    acc = jnp.zeros((x.shape[0], D_MODEL), F32)
    for c in range(D_FF // FF_CHUNK):
        cols = slice(c * FF_CHUNK, (c + 1) * FF_CHUNK)
        g = jnp.dot(x, wg_ref[0, :, cols], preferred_element_type=F32)
        u = jnp.dot(x, wu_ref[0, :, cols], preferred_element_type=F32)
        act = (g * jax.nn.sigmoid(g) * u).astype(BF16)
        acc = acc + jnp.dot(act, wd_ref[0, cols, :], preferred_element_type=F32)
    y_ref[0] = (acc * gate_ref[0][:, :1]).astype(BF16)


def _expert_ffn(xe, gate, w_gate, w_up, w_down, tm):
    e, rows, _ = xe.shape
    tok = lambda w: pl.BlockSpec((1, tm, w), lambda i, j: (i, j, 0))
    wspec = lambda a: pl.BlockSpec((1,) + a.shape[1:], lambda i, j: (i, 0, 0))
    return pl.pallas_call(
        _expert_kernel,
        grid=(e, rows // tm),
        in_specs=[tok(D_MODEL), tok(LANES), wspec(w_gate), wspec(w_up), wspec(w_down)],
        out_specs=tok(D_MODEL),
        out_shape=jax.ShapeDtypeStruct((e, rows, D_MODEL), BF16),
        compiler_params=_params(2),
        name="expert_ffn",
    )(xe, gate, w_gate, w_up, w_down)


BF16_SUBLANES = 16
COMBINE_WIN = 64
TN_DIMS = (((0,), (0,)), ((), ()))


def _combine_kernel(starts_ref, h1_ref, slot_ref, y_ref, g2_ref, b2_ref, out_ref, acc_ref, *, cap):
    b, j = pl.program_id(0), pl.program_id(1)
    n_bounds = pl.num_programs(1) + 1
    win = COMBINE_WIN
    base, n_win = [], 0
    for e in range(N_EXPERTS):
        at = (b * N_EXPERTS + e) * n_bounds + j
        first = (starts_ref[at] // BF16_SUBLANES) * BF16_SUBLANES
        base.append(first)
        n_win = jnp.maximum(n_win, (starts_ref[at + 1] - first + win - 1) // win)

    acc_ref[...] = jnp.zeros_like(acc_ref)
    wrow = lax.broadcasted_iota(jnp.int32, (win, slot_ref.shape[2]), 0)

    def window(k, carry):
        sels, rows = [], []
        for e in range(N_EXPERTS):
            lower = base[e] + k * win
            off = pl.multiple_of(jnp.minimum(lower, cap - win), BF16_SUBLANES)
            target = jnp.where(wrow + off >= lower, wrow + off, -2)
            sels.append(jnp.where(slot_ref[0, e:e + 1, :] == target, 1.0, 0.0))
            rows.append(y_ref[e, pl.ds(off, win), :])
        sel = jnp.concatenate(sels, axis=0).astype(BF16)
        acc_ref[...] += lax.dot_general(sel, jnp.concatenate(rows, axis=0), TN_DIMS, preferred_element_type=F32)
        return carry

    lax.fori_loop(0, n_win, window, 0)
    out_ref[0] = _layer_norm(DEEPNORM_ALPHA * h1_ref[0] + acc_ref[...], g2_ref[...], b2_ref[...])


def _combine(h1, slot, starts, y, ln2_g, ln2_b, cap):
    b, s, _ = h1.shape
    n_chunks = s // PREFIX_CHUNK
    const = lambda i, j, st: (0, 0)
    grid_spec = pltpu.PrefetchScalarGridSpec(
        num_scalar_prefetch=1,
        grid=(b, n_chunks),
        in_specs=[pl.BlockSpec((1, PREFIX_CHUNK, D_MODEL), lambda i, j, st: (i, j, 0)),
                  pl.BlockSpec((1, N_EXPERTS, PREFIX_CHUNK), lambda i, j, st: (i, 0, j)),
                  pl.BlockSpec((N_EXPERTS, cap, D_MODEL), lambda i, j, st: (0, i, 0)),
                  pl.BlockSpec((1, D_MODEL), const), pl.BlockSpec((1, D_MODEL), const)],
        out_specs=pl.BlockSpec((1, PREFIX_CHUNK, D_MODEL), lambda i, j, st: (i, j, 0)),
        scratch_shapes=[pltpu.VMEM((PREFIX_CHUNK, D_MODEL), F32)],
    )
    return pl.pallas_call(
        functools.partial(_combine_kernel, cap=cap),
        grid_spec=grid_spec,
        out_shape=jax.ShapeDtypeStruct((b, s, D_MODEL), F32),
        compiler_params=_params(2),
        name="combine_ln2",
    )(starts[:, :, :n_chunks + 1].reshape(-1), h1, slot, y, ln2_g.reshape(1, D_MODEL), ln2_b.reshape(1, D_MODEL))


def kernel(x, ln0_g, ln0_b, w_in, b_gate, qn_g, kn_g, w_branch_a, w_branch_b, w_out, ln1_g, ln1_b, w_router,
           w_gate_e, w_up_e, w_down_e, ln2_g, ln2_b):
    batch, seq, _ = x.shape
    assert w_in.shape[0] == DEPTH
    cap = EC_CAPACITY_FACTOR * seq // N_EXPERTS
    tokens = batch * seq
    x2 = x.reshape(tokens, D_MODEL)
    w_in_b = w_in[0].astype(BF16)

    qkv = _in_projection(x2, ln0_g, ln0_b, w_in_b[:, :QKV_W], qn_g[0], kn_g[0], seq, tm=512)
    qa, ka, va = qkv[:3]
    qb, kb, vb = qkv[3:6], qkv[6:9], qkv[9:12]

    oa, (wg_b, wu_b, wd_b) = _attention_a(qa.reshape(batch, seq, A_Q_W), ka.reshape(batch, seq, A_KV_W),
                                          va.reshape(batch, seq, A_KV_W), (w_gate_e[0], w_up_e[0], w_down_e[0]),
                                          tq=512)
    oa = oa.reshape(tokens, A_Q_W)

    obs, lses = [], []
    for gi, (window, dilation) in enumerate(B_GROUPS):
        assert window // (2 * dilation) == BAND_HALF
        o, lse = _dilated_attention(qb[gi], kb[gi], vb[gi], seq, dilation)
        obs.append(o)
        lses.append(lse)

    h1, h1b, logits = _merge(x2, ln0_g, ln0_b, w_in_b[:, QKV_W:], b_gate[0], oa, obs, lses,
                             w_branch_a[0].astype(BF16), w_branch_b[0].astype(BF16), w_out[0].astype(BF16),
                             ln1_g[0], ln1_b[0], w_router[0], seq, tm=512)

    slot, aff, starts = _route(logits, batch, seq, cap)
    xe, gate = _dispatch(h1b.reshape(batch, seq, D_MODEL), slot, aff, starts, cap)
    y = _expert_ffn(xe, gate, wg_b, wu_b, wd_b, tm=min(1024, batch * cap))
    return _combine(h1.reshape(batch, seq, D_MODEL), slot, starts, y, ln2_g[0], ln2_b[0], cap)
```

```python
import functools

import jax
import jax.numpy as jnp
from jax import lax
from jax.experimental import pallas as pl
from jax.experimental.pallas import tpu as pltpu

F32 = jnp.float32
BF16 = jnp.bfloat16

D_MODEL = 1024
HEAD_DIM = 64
A_Q_HEADS = 8
A_KV_HEADS = 2
A_GROUP = A_Q_HEADS // A_KV_HEADS
B_GROUPS = ((128, 1), (512, 4), (2048, 16))
B_HEADS_PER_GROUP = 4
GRID_W = 64
A_ROPE_THETA = 10000.0
B_ROPE_THETA = 500000.0
B_ROPE_DIMS = HEAD_DIM // 4
N_EXPERTS = 16
EC_CAPACITY_FACTOR = 2
D_FF = 2 * D_MODEL
LN_EPS = 1e-5
QK_NORM_EPS = 1e-6
MASK_VALUE = -1e30
DEPTH = 1
DEEPNORM_ALPHA = (2.0 * DEPTH) ** 0.25
LOG2_E = 1.4426950408889634
Q_SCALE = HEAD_DIM ** -0.5 * LOG2_E

A_Q_W = A_Q_HEADS * HEAD_DIM
A_KV_W = A_KV_HEADS * HEAD_DIM
B_GROUP_W = B_HEADS_PER_GROUP * HEAD_DIM
B_W = B_GROUP_W * len(B_GROUPS)
QKV_W = A_Q_W + 2 * A_KV_W + 3 * B_W
BAND_HALF = 64

LANES = 128
VMEM_LIMIT = 56 * 1024 * 1024

NT_DIMS = (((1,), (1,)), ((), ()))


def _params(n_axes):
    return pltpu.CompilerParams(dimension_semantics=("arbitrary",) * n_axes, vmem_limit_bytes=VMEM_LIMIT)


def _layer_norm(x, g, b):
    mu = jnp.mean(x, axis=-1, keepdims=True)
    xc = x - mu
    var = jnp.mean(xc * xc, axis=-1, keepdims=True)
    return xc * lax.rsqrt(var + LN_EPS) * g + b


def _sigmoid(x):
    return 0.5 * jnp.tanh(0.5 * x) + 0.5


def _split_bf16(x):
    hi = x.astype(BF16)
    lo = (x - hi.astype(F32)).astype(BF16)
    return hi, lo


def _swap_halves(y, lane, period, half):
    fwd = pltpu.roll(y, LANES - half, 1)
    bwd = pltpu.roll(y, half, 1)
    return jnp.where((lane % period) < half, fwd, bwd)


def _inproj_kernel(x_ref, g0_ref, b0_ref, w_ref, qng_ref, kng_ref, cosa_ref, sina_ref, cosb_ref, sinb_ref,
                   qa_ref, ka_ref, va_ref, qb0_ref, qb1_ref, qb2_ref, kb0_ref, kb1_ref, kb2_ref,
                   vb0_ref, vb1_ref, vb2_ref, h_ref):
    step = pl.program_id(0)

    @pl.when(step == 0)
    def _():
        h_ref[1] = jnp.zeros(h_ref.shape[1:], BF16)

    h = h_ref[(step + 1) % 2]
    lane = lax.broadcasted_iota(jnp.int32, (1, LANES), 1)
    r = lax.broadcasted_iota(jnp.int32, (LANES, LANES), 0)
    c = lax.broadcasted_iota(jnp.int32, (LANES, LANES), 1)
    head_sum = jnp.where((r // HEAD_DIM) == (c // HEAD_DIM), 1.0, 0.0).astype(BF16)
    cosa, sina = cosa_ref[...], sina_ref[...]
    cosb, sinb = cosb_ref[...], sinb_ref[...]

    def proj(off):
        y = jnp.dot(h, w_ref[:, off:off + 2 * LANES], preferred_element_type=F32)
        return y[:, :LANES], y[:, LANES:]

    def norm_rope_a(y, gain, scale):
        hi, lo = _split_bf16(y * y)
        ss = (jnp.dot(hi, head_sum, preferred_element_type=F32)
              + jnp.dot(lo, head_sum, preferred_element_type=F32))
        y = y * lax.rsqrt(ss * (1.0 / HEAD_DIM) + QK_NORM_EPS) * gain
        y = y * cosa + _swap_halves(y, lane, HEAD_DIM // 2, HEAD_DIM // 4) * sina
        return (y * scale).astype(BF16)

    def rope_b(y, scale):
        y = y * cosb + _swap_halves(y, lane, HEAD_DIM, B_ROPE_DIMS // 2) * sinb
        return y * scale

    def project_b(gi, q_ref, k_ref, v_ref):
        off = A_Q_W + 2 * A_KV_W + gi * B_GROUP_W
        for j, (q, k, v) in enumerate(zip(proj(off), proj(off + B_W), proj(off + 2 * B_W))):
            q_ref[j] = rope_b(q, Q_SCALE)
            k_ref[j] = rope_b(k, 1.0)
            v_ref[j] = v

    qa_tiles = [y for j in range(A_Q_W // (2 * LANES)) for y in proj(2 * j * LANES)]
    ka, va = proj(A_Q_W)
    project_b(0, qb0_ref, kb0_ref, vb0_ref)
    for j, y in enumerate(qa_tiles):
        qa_ref[:, j * LANES:(j + 1) * LANES] = norm_rope_a(y, qng_ref[...], Q_SCALE)
    ka_ref[...] = norm_rope_a(ka, kng_ref[...], 1.0)
    va_ref[...] = va.astype(BF16)
    project_b(1, qb1_ref, kb1_ref, vb1_ref)
    project_b(2, qb2_ref, kb2_ref, vb2_ref)

    h_ref[step % 2] = _layer_norm(x_ref[...], g0_ref[...], b0_ref[...]).astype(BF16)


def _rope_tables(seq):
    lane = jnp.arange(LANES)
    d = lane % HEAD_DIM
    t = jnp.arange(seq)
    half_rot = HEAD_DIM // 2
    inv_a = A_ROPE_THETA ** (-jnp.arange(0, half_rot, 2, dtype=F32) / half_rot)
    j = d % half_rot
    pos_row = (t // GRID_W).astype(F32)
    pos_col = (t % GRID_W).astype(F32)
    freq_a = inv_a[j % (half_rot // 2)]
    ang_a = jnp.where((d < half_rot)[None, :], pos_row[:, None], pos_col[:, None]) * freq_a[None, :]
    sign_a = jnp.where(j < half_rot // 2, -1.0, 1.0).astype(F32)
    cosa = jnp.cos(ang_a)
    sina = jnp.sin(ang_a) * sign_a[None, :]
    inv_b = B_ROPE_THETA ** (-jnp.arange(0, B_ROPE_DIMS, 2, dtype=F32) / B_ROPE_DIMS)
    freq_b = inv_b[d % (B_ROPE_DIMS // 2)]
    ang_b = t.astype(F32)[:, None] * freq_b[None, :]
    rot = (d < B_ROPE_DIMS)[None, :]
    sign_b = jnp.where(d < B_ROPE_DIMS // 2, -1.0, 1.0).astype(F32)
    cosb = jnp.where(rot, jnp.cos(ang_b), 1.0)
    sinb = jnp.where(rot, jnp.sin(ang_b) * sign_b[None, :], 0.0)
    return cosa, sina, cosb, sinb


def _in_projection(x2, ln0_g, ln0_b, w_qkv, qn_g, kn_g, seq, tm):
    m = x2.shape[0]
    cosa, sina, cosb, sinb = _rope_tables(seq)
    n_tiles = m // tm
    tiles_per_seq = seq // tm
    dst = lambda i: jnp.maximum(i - 1, 0)
    row = lambda i: (dst(i), 0)
    const = lambda i: (0, 0)
    table = lambda i: (dst(i) % tiles_per_seq, 0)
    widths = (A_Q_W, A_KV_W, A_KV_W) + (B_GROUP_W,) * 9
    gain_tile = lambda g: jnp.tile(g.reshape(1, HEAD_DIM), (1, LANES // HEAD_DIM))
    return pl.pallas_call(
        _inproj_kernel,
        grid=(n_tiles + 1,),
        in_specs=[
            pl.BlockSpec((tm, D_MODEL), lambda i: (jnp.minimum(i, n_tiles - 1), 0)),
            pl.BlockSpec((1, D_MODEL), const),
            pl.BlockSpec((1, D_MODEL), const),
            pl.BlockSpec((D_MODEL, QKV_W), const),
            pl.BlockSpec((1, LANES), const),
            pl.BlockSpec((1, LANES), const),
            pl.BlockSpec((tm, LANES), table),
            pl.BlockSpec((tm, LANES), table),
            pl.BlockSpec((tm, LANES), table),
            pl.BlockSpec((tm, LANES), table),
        ],
        out_specs=([pl.BlockSpec((tm, w), row) for w in widths[:3]]
                   + [pl.BlockSpec((B_GROUP_W // LANES, tm, LANES), lambda i: (0, dst(i), 0))] * 9),
        out_shape=([jax.ShapeDtypeStruct((m, w), BF16) for w in widths[:3]]
                   + [jax.ShapeDtypeStruct((B_GROUP_W // LANES, m, LANES), F32)] * 9),
        scratch_shapes=[pltpu.VMEM((2, tm, D_MODEL), BF16)],
        compiler_params=_params(1),
        name="in_projection",
    )(x2, ln0_g.reshape(1, D_MODEL), ln0_b.reshape(1, D_MODEL), w_qkv, gain_tile(qn_g), gain_tile(kn_g),
      cosa, sina, cosb, sinb)


def _attn_a_kernel(q_ref, k_ref, v_ref, wg_ref, wu_ref, wd_ref, o_ref, wgb_ref, wub_ref, wdb_ref):
    wgb_ref[...] = wg_ref[...].astype(BF16)
    wub_ref[...] = wu_ref[...].astype(BF16)
    wdb_ref[...] = wd_ref[...].astype(BF16)
    def scores(hq):
        kh = hq // A_GROUP
        q = q_ref[0, :, hq * HEAD_DIM:(hq + 1) * HEAD_DIM]
        k = k_ref[0, :, kh * HEAD_DIM:(kh + 1) * HEAD_DIM]
        return lax.dot_general(q, k, NT_DIMS, preferred_element_type=F32)

    outs = []
    s = scores(0)
    for hq in range(A_Q_HEADS):
        s_next = scores(hq + 1) if hq + 1 < A_Q_HEADS else None
        kh = hq // A_GROUP
        v = v_ref[0, :, kh * HEAD_DIM:(kh + 1) * HEAD_DIM]
        p = jnp.exp2(s - jnp.max(s, axis=-1, keepdims=True))
        denom = jnp.sum(p, axis=-1, keepdims=True)
        o = jnp.dot(p.astype(BF16), v, preferred_element_type=F32)
        outs.append(o / denom)
        s = s_next
    o_ref[0] = jnp.concatenate(outs, axis=1).astype(BF16)


def _attention_a(qa, ka, va, expert_weights, tq):
    b, s, _ = qa.shape
    steps = b * (s // tq)
    flat = [w.reshape(-1, w.shape[-1]) for w in expert_weights]
    wslice = lambda w: pl.BlockSpec((w.shape[0] // steps, w.shape[1]), lambda i, j: (i * (s // tq) + j, 0))
    o, *rounded = pl.pallas_call(
        _attn_a_kernel,
        grid=(b, s // tq),
        in_specs=[
            pl.BlockSpec((1, tq, A_Q_W), lambda i, j: (i, j, 0)),
            pl.BlockSpec((1, s, A_KV_W), lambda i, j: (i, 0, 0)),
            pl.BlockSpec((1, s, A_KV_W), lambda i, j: (i, 0, 0)),
            *map(wslice, flat),
        ],
        out_specs=[pl.BlockSpec((1, tq, A_Q_W), lambda i, j: (i, j, 0)), *map(wslice, flat)],
        out_shape=[jax.ShapeDtypeStruct((b, s, A_Q_W), BF16),
                   *[jax.ShapeDtypeStruct(w.shape, BF16) for w in flat]],
        compiler_params=_params(2),
        name="attention_a",
    )(qa, ka, va, *flat)
    return o, [r.reshape(w.shape) for r, w in zip(rounded, expert_weights)]


BAND_QB = 128
BAND_UNITS_PER_STEP = 8
BAND_STEPS = 2


def _band_bias(n):
    win = min(2 * BAND_QB, n)
    rows = B_HEADS_PER_GROUP * BAND_QB
    rel = (jnp.arange(rows)[:, None] % BAND_QB) - jnp.arange(win)[None, :]
    shifts = jnp.arange(2 * BAND_QB // BAND_HALF - 1) * BAND_HALF
    inside = jnp.abs(rel[None] + shifts[:, None, None]) <= BAND_HALF
    return jnp.where(inside, 0.0, MASK_VALUE).astype(F32)


def _dilated_kernel(q_ref, k_ref, v_ref, bias_ref, o_ref, lse_ref, *, n, dilation):
    win = min(2 * BAND_QB, n)
    n_blk = n // BAND_QB
    lane_head = lax.broadcasted_iota(jnp.int32, (1, B_GROUP_W), 1) // HEAD_DIM

    planes = B_GROUP_W // LANES

    def class_rows(r, start, size):
        if dilation == 1:
            return pl.ds(pl.multiple_of(start, BAND_HALF), size)
        return pl.ds(r + start * dilation, size, stride=dilation)

    def load(ref, rows_):
        return jnp.concatenate([ref[j, rows_, :] for j in range(planes)], axis=1)

    def scores(i):
        unit = pl.program_id(1) * BAND_UNITS_PER_STEP + i
        r = unit // n_blk
        qs = (unit % n_blk) * BAND_QB
        ks = jnp.clip(qs - BAND_HALF, 0, n - win)
        q = load(q_ref, class_rows(r, qs, BAND_QB))
        k = load(k_ref, class_rows(r, ks, win)).astype(BF16)
        q_stack = jnp.concatenate([jnp.where(lane_head == h, q, 0.0) for h in range(B_HEADS_PER_GROUP)], axis=0)
        s = lax.dot_general(q_stack.astype(BF16), k, NT_DIMS, preferred_element_type=F32)
        return r, qs, ks, s + bias_ref[(qs - ks) // BAND_HALF]

    nxt = scores(0)
    for i in range(BAND_UNITS_PER_STEP):
        r, qs, ks, s = nxt
        if i + 1 < BAND_UNITS_PER_STEP:
            nxt = scores(i + 1)
        v = load(v_ref, class_rows(r, ks, win)).astype(BF16)
        mx = jnp.max(s, axis=-1, keepdims=True)
        p = jnp.exp2(s - mx)
        denom = jnp.sum(p, axis=-1, keepdims=True)
        pv = jnp.dot(p.astype(BF16), v, preferred_element_type=F32) / denom
        lse = mx + jnp.log2(denom)
        o = jnp.zeros((BAND_QB, B_GROUP_W), F32)
        l = jnp.zeros((BAND_QB, B_GROUP_W), F32)
        for h in range(B_HEADS_PER_GROUP):
            blk = slice(h * BAND_QB, (h + 1) * BAND_QB)
            o = jnp.where(lane_head == h, pv[blk], o)
            l = jnp.where(lane_head == h, lse[blk], l)
        for j in range(planes):
            o_ref[j, class_rows(r, qs, BAND_QB), :] = o[:, j * LANES:(j + 1) * LANES]
            lse_ref[j, class_rows(r, qs, BAND_QB), :] = l[:, j * LANES:(j + 1) * LANES]


def _dilated_attention(q, k, v, seq, dilation):
    planes, tokens, _ = q.shape
    n = seq // dilation
    assert dilation * (n // BAND_QB) == BAND_UNITS_PER_STEP * BAND_STEPS
    blk = pl.BlockSpec((planes, seq, LANES), lambda i, j: (0, i, 0))
    bias = _band_bias(n)
    return pl.pallas_call(
        functools.partial(_dilated_kernel, n=n, dilation=dilation),
        grid=(tokens // seq, BAND_STEPS),
        in_specs=[blk, blk, blk, pl.BlockSpec(bias.shape, lambda i, j: (0, 0, 0))],
        out_specs=[blk, blk],
        out_shape=[jax.ShapeDtypeStruct(q.shape, F32)] * 2,
        compiler_params=_params(2),
        name=f"dilated_attention_d{dilation}",
    )(q, k, v, bias)


MERGE_SUB_ROWS = 512


def _merge_kernel(x_ref, g0_ref, b0_ref, wg_ref, bg_ref, oa_ref, o0_ref, o1_ref, o2_ref, l0_ref, l1_ref, l2_ref,
                  wa_ref, wb_ref, wo_ref, g1_ref, b1_ref, wr_hi_ref, wr_lo_ref,
                  h1_ref, h1b_ref, logit_ref, z_ref):
    i = pl.program_id(0)

    @pl.when(i == 0)
    def _():
        z_ref[1] = jnp.zeros(z_ref.shape[1:], F32)

    planes = lambda ref: jnp.concatenate([ref[j] for j in range(ref.shape[0])], axis=1)
    ya = jnp.dot(oa_ref[...], wa_ref[...], preferred_element_type=F32)
    h0 = _layer_norm(x_ref[...], g0_ref[...], b0_ref[...])
    gates = _sigmoid(jnp.dot(h0.astype(BF16), wg_ref[...], preferred_element_type=F32) + bg_ref[...])

    h1 = _layer_norm(z_ref[(i + 1) % 2], g1_ref[...], b1_ref[...])
    h1_ref[...] = h1
    h1b_ref[...] = h1.astype(BF16)
    hi, lo = _split_bf16(h1)
    logit_ref[0] = (lax.dot_general(wr_hi_ref[...], hi, NT_DIMS, preferred_element_type=F32)
                    + lax.dot_general(wr_hi_ref[...], lo, NT_DIMS, preferred_element_type=F32)
                    + lax.dot_general(wr_lo_ref[...], hi, NT_DIMS, preferred_element_type=F32))

    l0, l1, l2 = planes(l0_ref), planes(l1_ref), planes(l2_ref)
    mx = jnp.maximum(jnp.maximum(l0, l1), l2)
    e0, e1, e2 = jnp.exp2(l0 - mx), jnp.exp2(l1 - mx), jnp.exp2(l2 - mx)
    ob = (e0 * planes(o0_ref) + e1 * planes(o1_ref) + e2 * planes(o2_ref)) / (e0 + e1 + e2)
    yb = jnp.dot(ob.astype(BF16), wb_ref[...], preferred_element_type=F32)
    merged = gates[:, :D_MODEL] * ya + gates[:, D_MODEL:] * yb
    mix = jnp.dot(merged.astype(BF16), wo_ref[...], preferred_element_type=F32)
    z_ref[i % 2] = DEEPNORM_ALPHA * h0 + mix


def _merge(x2, ln0_g, ln0_b, w_gates, b_gate, oa, obs, lses, w_a, w_b, w_o, ln1_g, ln1_b, w_router, seq, tm):
    m = x2.shape[0]
    n_tiles = m // tm
    tiles_per_seq = seq // tm
    src = lambda i: jnp.minimum(i, n_tiles - 1)
    dst = lambda i: jnp.maximum(i - 1, 0)
    const = lambda i: (0, 0)
    wr_t = w_router.T
    wr_hi = wr_t.astype(BF16)
    wr_lo = (wr_t - wr_hi.astype(F32)).astype(BF16)
    vec = lambda v: v.reshape(1, -1)
    full = lambda a: pl.BlockSpec(a.shape, const)
    tile_in = lambda w: pl.BlockSpec((tm, w), lambda i: (src(i), 0))
    tile_out = lambda w: pl.BlockSpec((tm, w), lambda i: (dst(i), 0))
    plane_in = pl.BlockSpec((B_GROUP_W // LANES, tm, LANES), lambda i: (0, src(i), 0))
    args = [x2, vec(ln0_g), vec(ln0_b), w_gates, vec(b_gate), oa, *obs, *lses, w_a, w_b, w_o,
            vec(ln1_g), vec(ln1_b), wr_hi, wr_lo]
    in_specs = [tile_in(D_MODEL), full(args[1]), full(args[2]), full(w_gates), full(args[4]), tile_in(A_Q_W),
                *[plane_in] * 6, full(w_a), full(w_b), full(w_o), full(args[15]), full(args[16]),
                full(wr_hi), full(wr_lo)]
    return pl.pallas_call(
        _merge_kernel,
        grid=(n_tiles + 1,),
        in_specs=in_specs,
        out_specs=[tile_out(D_MODEL), tile_out(D_MODEL),
                   pl.BlockSpec((1, N_EXPERTS, tm),
                                lambda i: (dst(i) // tiles_per_seq, 0, dst(i) % tiles_per_seq))],
        out_shape=[jax.ShapeDtypeStruct((m, D_MODEL), F32), jax.ShapeDtypeStruct((m, D_MODEL), BF16),
                   jax.ShapeDtypeStruct((m // seq, N_EXPERTS, seq), F32)],
        scratch_shapes=[pltpu.VMEM((2, tm, D_MODEL), F32)],
        compiler_params=_params(1),
        name="merge_ln1_router",
    )(*args)


PREFIX_CHUNK = 256


def _prefix_count(mask, tri):
    ones = jnp.where(mask, 1.0, 0.0)
    carry = jnp.zeros((mask.shape[0], 1), F32)
    outs, bounds = [], [carry]
    for j in range(mask.shape[1] // PREFIX_CHUNK):
        chunk = ones[:, j * PREFIX_CHUNK:(j + 1) * PREFIX_CHUNK]
        outs.append(jnp.dot(chunk.astype(BF16), tri, preferred_element_type=F32) + carry)
        carry = carry + jnp.sum(chunk, axis=1, keepdims=True)
        bounds.append(carry)
    return jnp.concatenate(outs, axis=1), bounds


THRESHOLD_BITS = 31
THRESHOLD_REFINE = 16


def _route_kernel(logit_ref, slot_ref, aff_ref, starts_ref, *, cap):
    n_seq = logit_ref.shape[0]
    affs = []
    for b in range(n_seq):
        lg = logit_ref[b]
        ex = jnp.exp(lg - jnp.max(lg, axis=0, keepdims=True))
        affs.append(ex / jnp.sum(ex, axis=0, keepdims=True))
    aff = jnp.concatenate(affs, axis=0)

    def count_ge(t):
        return jnp.sum(jnp.where(aff >= t, 1.0, 0.0), axis=1, keepdims=True)

    thr = jnp.zeros((aff.shape[0], 1), jnp.int32)
    for bit in range(THRESHOLD_BITS - 1, -1, -1):
        cand = thr | (1 << bit)
        thr = jnp.where(count_ge(pltpu.bitcast(cand, F32)) >= cap, cand, thr)
    lo = pltpu.bitcast(thr, F32)
    hi = pltpu.bitcast(thr + 1, F32)
    for _ in range(THRESHOLD_REFINE):
        mid = 0.5 * (lo + hi)
        take = count_ge(mid) >= cap
        lo = jnp.where(take, mid, lo)
        hi = jnp.where(take, hi, mid)
    above = aff >= hi
    tied = (aff >= lo) & (aff < hi)
    r = lax.broadcasted_iota(jnp.int32, (PREFIX_CHUNK, PREFIX_CHUNK), 0)
    c = lax.broadcasted_iota(jnp.int32, (PREFIX_CHUNK, PREFIX_CHUNK), 1)
    tri = jnp.where(r < c, 1.0, 0.0).astype(BF16)
    need = cap - jnp.sum(jnp.where(above, 1.0, 0.0), axis=1, keepdims=True)
    sel = above | (tied & (_prefix_count(tied, tri)[0] < need))
    rank, bounds = _prefix_count(sel, tri)
    slot = jnp.where(sel, rank, -1.0).astype(jnp.int32)
    lane = lax.broadcasted_iota(jnp.int32, (1, LANES), 1)
    starts = jnp.zeros((aff.shape[0], LANES), F32)
    for j, count in enumerate(bounds):
        starts = jnp.where(lane == j, count, starts)
    for b in range(n_seq):
        rows = slice(b * N_EXPERTS, (b + 1) * N_EXPERTS)
        slot_ref[b] = slot[rows]
        aff_ref[b] = aff[rows]
        starts_ref[b] = starts[rows].astype(jnp.int32)


ROUTE_SEQS_PER_STEP = 4


def _route(logits, batch, seq, cap):
    nb = ROUTE_SEQS_PER_STEP if batch % ROUTE_SEQS_PER_STEP == 0 else 1
    blk = pl.BlockSpec((nb, N_EXPERTS, seq), lambda i: (i, 0, 0))
    return pl.pallas_call(
        functools.partial(_route_kernel, cap=cap),
        grid=(batch // nb,),
        in_specs=[blk],
        out_specs=[blk, blk, pl.BlockSpec((nb, N_EXPERTS, LANES), lambda i: (i, 0, 0))],
        out_shape=[jax.ShapeDtypeStruct((batch, N_EXPERTS, seq), jnp.int32),
                   jax.ShapeDtypeStruct((batch, N_EXPERTS, seq), F32),
                   jax.ShapeDtypeStruct((batch, N_EXPERTS, LANES), jnp.int32)],
        compiler_params=_params(1),
        name="route",
    )(logits)


SUBLANES = 8
DISPATCH_WIN = 56


def _dispatch_kernel(starts_ref, h_ref, slot_ref, aff_ref, xe_ref, gate_ref, acc_ref, gacc_ref, *, cap):
    b, j = pl.program_id(0), pl.program_id(1)
    n_bounds = pl.num_programs(1) + 1
    win = DISPATCH_WIN

    @pl.when(j == 0)
    def _():
        acc_ref[...] = jnp.zeros_like(acc_ref)
        gacc_ref[...] = jnp.zeros_like(gacc_ref)

    base, n_win = [], 0
    for e in range(N_EXPERTS):
        at = (b * N_EXPERTS + e) * n_bounds + j
        first = (starts_ref[at] // SUBLANES) * SUBLANES
        base.append(first)
        n_win = jnp.maximum(n_win, (starts_ref[at + 1] - first + win - 1) // win)

    h = h_ref[0]
    wrow = lax.broadcasted_iota(jnp.int32, (win, h.shape[0]), 0)

    def window(k, carry):
        offs = [pl.multiple_of(jnp.minimum(base[e] + k * win, cap), SUBLANES) for e in range(N_EXPERTS)]
        onehots = [slot_ref[0, e:e + 1, :] == wrow + offs[e] for e in range(N_EXPERTS)]
        sel = jnp.concatenate([jnp.where(o, 1.0, 0.0) for o in onehots], axis=0).astype(BF16)
        rows = jnp.dot(sel, h, preferred_element_type=F32)
        for e in range(N_EXPERTS):
            at = pl.ds(offs[e], win)
            acc_ref[e, at, :] += rows[e * win:(e + 1) * win]
            gate = jnp.sum(jnp.where(onehots[e], aff_ref[0, e:e + 1, :], 0.0), axis=1, keepdims=True)
            gacc_ref[e, at, :] += jnp.broadcast_to(gate, (win, LANES))
        return carry

    lax.fori_loop(0, n_win, window, 0)

    @pl.when(j == pl.num_programs(1) - 1)
    def _():
        xe_ref[...] = acc_ref[:, :cap, :].astype(BF16)
        gate_ref[...] = gacc_ref[:, :cap, :]


def _dispatch(h1b, slot, aff, starts, cap):
    b, s, _ = h1b.shape
    n_chunks = s // PREFIX_CHUNK
    chunk = pl.BlockSpec((1, N_EXPERTS, PREFIX_CHUNK), lambda i, j, st: (i, 0, j))
    grid_spec = pltpu.PrefetchScalarGridSpec(
        num_scalar_prefetch=1,
        grid=(b, n_chunks),
        in_specs=[pl.BlockSpec((1, PREFIX_CHUNK, D_MODEL), lambda i, j, st: (i, j, 0)), chunk, chunk],
        out_specs=[pl.BlockSpec((N_EXPERTS, cap, D_MODEL), lambda i, j, st: (0, i, 0)),
                   pl.BlockSpec((N_EXPERTS, cap, LANES), lambda i, j, st: (0, i, 0))],
        scratch_shapes=[pltpu.VMEM((N_EXPERTS, cap + DISPATCH_WIN, D_MODEL), F32),
                        pltpu.VMEM((N_EXPERTS, cap + DISPATCH_WIN, LANES), F32)],
    )
    return pl.pallas_call(
        functools.partial(_dispatch_kernel, cap=cap),
        grid_spec=grid_spec,
        out_shape=[jax.ShapeDtypeStruct((N_EXPERTS, b * cap, D_MODEL), BF16),
                   jax.ShapeDtypeStruct((N_EXPERTS, b * cap, LANES), F32)],
        compiler_params=_params(2),
        name="dispatch",
    )(starts[:, :, :n_chunks + 1].reshape(-1), h1b, slot, aff)


FF_CHUNK = 512


def _expert_kernel(x_ref, gate_ref, wg_ref, wu_ref, wd_ref, y_ref):
    x = x_ref[0]
    acc = jnp.zeros((x.shape[0], D_MODEL), F32)
    for c in range(D_FF // FF_CHUNK):
        cols = slice(c * FF_CHUNK, (c + 1) * FF_CHUNK)
        g = jnp.dot(x, wg_ref[0, :, cols], preferred_element_type=F32)
        u = jnp.dot(x, wu_ref[0, :, cols], preferred_element_type=F32)
        act = (g * jax.nn.sigmoid(g) * u).astype(BF16)
        acc = acc + jnp.dot(act, wd_ref[0, cols, :], preferred_element_type=F32)
    y_ref[0] = (acc * gate_ref[0][:, :1]).astype(BF16)


def _expert_ffn(xe, gate, w_gate, w_up, w_down, tm):
    e, rows, _ = xe.shape
    tok = lambda w: pl.BlockSpec((1, tm, w), lambda i, j: (i, j, 0))
    wspec = lambda a: pl.BlockSpec((1,) + a.shape[1:], lambda i, j: (i, 0, 0))
    return pl.pallas_call(
        _expert_kernel,
        grid=(e, rows // tm),
        in_specs=[tok(D_MODEL), tok(LANES), wspec(w_gate), wspec(w_up), wspec(w_down)],
        out_specs=tok(D_MODEL),
        out_shape=jax.ShapeDtypeStruct((e, rows, D_MODEL), BF16),
        compiler_params=_params(2),
        name="expert_ffn",
    )(xe, gate, w_gate, w_up, w_down)


BF16_SUBLANES = 16
COMBINE_WIN = 64
TN_DIMS = (((0,), (0,)), ((), ()))


def _combine_kernel(starts_ref, h1_ref, slot_ref, y_ref, g2_ref, b2_ref, out_ref, acc_ref, *, cap):
    b, j = pl.program_id(0), pl.program_id(1)
    n_bounds = pl.num_programs(1) + 1
    win = COMBINE_WIN
    base, n_win = [], 0
    for e in range(N_EXPERTS):
        at = (b * N_EXPERTS + e) * n_bounds + j
        first = (starts_ref[at] // BF16_SUBLANES) * BF16_SUBLANES
        base.append(first)
        n_win = jnp.maximum(n_win, (starts_ref[at + 1] - first + win - 1) // win)

    acc_ref[...] = jnp.zeros_like(acc_ref)
    wrow = lax.broadcasted_iota(jnp.int32, (win, slot_ref.shape[2]), 0)

    def window(k, carry):
        sels, rows = [], []
        for e in range(N_EXPERTS):
            lower = base[e] + k * win
            off = pl.multiple_of(jnp.minimum(lower, cap - win), BF16_SUBLANES)
            target = jnp.where(wrow + off >= lower, wrow + off, -2)
            sels.append(jnp.where(slot_ref[0, e:e + 1, :] == target, 1.0, 0.0))
            rows.append(y_ref[e, pl.ds(off, win), :])
        sel = jnp.concatenate(sels, axis=0).astype(BF16)
        acc_ref[...] += lax.dot_general(sel, jnp.concatenate(rows, axis=0), TN_DIMS, preferred_element_type=F32)
        return carry

    lax.fori_loop(0, n_win, window, 0)
    out_ref[0] = _layer_norm(DEEPNORM_ALPHA * h1_ref[0] + acc_ref[...], g2_ref[...], b2_ref[...])


def _combine(h1, slot, starts, y, ln2_g, ln2_b, cap):
    b, s, _ = h1.shape
    n_chunks = s // PREFIX_CHUNK
    const = lambda i, j, st: (0, 0)
    grid_spec = pltpu.PrefetchScalarGridSpec(
        num_scalar_prefetch=1,
        grid=(b, n_chunks),
        in_specs=[pl.BlockSpec((1, PREFIX_CHUNK, D_MODEL), lambda i, j, st: (i, j, 0)),
                  pl.BlockSpec((1, N_EXPERTS, PREFIX_CHUNK), lambda i, j, st: (i, 0, j)),
                  pl.BlockSpec((N_EXPERTS, cap, D_MODEL), lambda i, j, st: (0, i, 0)),
                  pl.BlockSpec((1, D_MODEL), const), pl.BlockSpec((1, D_MODEL), const)],
        out_specs=pl.BlockSpec((1, PREFIX_CHUNK, D_MODEL), lambda i, j, st: (i, j, 0)),
        scratch_shapes=[pltpu.VMEM((PREFIX_CHUNK, D_MODEL), F32)],
    )
    return pl.pallas_call(
        functools.partial(_combine_kernel, cap=cap),
        grid_spec=grid_spec,
        out_shape=jax.ShapeDtypeStruct((b, s, D_MODEL), F32),
        compiler_params=_params(2),
        name="combine_ln2",
    )(starts[:, :, :n_chunks + 1].reshape(-1), h1, slot, y, ln2_g.reshape(1, D_MODEL), ln2_b.reshape(1, D_MODEL))


def kernel(x, ln0_g, ln0_b, w_in, b_gate, qn_g, kn_g, w_branch_a, w_branch_b, w_out, ln1_g, ln1_b, w_router,
           w_gate_e, w_up_e, w_down_e, ln2_g, ln2_b):
    batch, seq, _ = x.shape
    assert w_in.shape[0] == DEPTH
    cap = EC_CAPACITY_FACTOR * seq // N_EXPERTS
    tokens = batch * seq
    x2 = x.reshape(tokens, D_MODEL)
    w_in_b = w_in[0].astype(BF16)

    qkv = _in_projection(x2, ln0_g, ln0_b, w_in_b[:, :QKV_W], qn_g[0], kn_g[0], seq, tm=512)
    qa, ka, va = qkv[:3]
    qb, kb, vb = qkv[3:6], qkv[6:9], qkv[9:12]

    oa, (wg_b, wu_b, wd_b) = _attention_a(qa.reshape(batch, seq, A_Q_W), ka.reshape(batch, seq, A_KV_W),
                                          va.reshape(batch, seq, A_KV_W), (w_gate_e[0], w_up_e[0], w_down_e[0]),
                                          tq=512)
    oa = oa.reshape(tokens, A_Q_W)

    obs, lses = [], []
    for gi, (window, dilation) in enumerate(B_GROUPS):
        assert window // (2 * dilation) == BAND_HALF
        o, lse = _dilated_attention(qb[gi], kb[gi], vb[gi], seq, dilation)
        obs.append(o)
        lses.append(lse)

    h1, h1b, logits = _merge(x2, ln0_g, ln0_b, w_in_b[:, QKV_W:], b_gate[0], oa, obs, lses,
                             w_branch_a[0].astype(BF16), w_branch_b[0].astype(BF16), w_out[0].astype(BF16),
                             ln1_g[0], ln1_b[0], w_router[0], seq, tm=512)

    slot, aff, starts = _route(logits, batch, seq, cap)
    xe, gate = _dispatch(h1b.reshape(batch, seq, D_MODEL), slot, aff, starts, cap)
    y = _expert_ffn(xe, gate, wg_b, wu_b, wd_b, tm=min(1024, batch * cap))
    return _combine(h1.reshape(batch, seq, D_MODEL), slot, starts, y, ln2_g[0], ln2_b[0], cap)
```

```python
import functools

import jax
import jax.numpy as jnp
from jax import lax
from jax.experimental import pallas as pl
from jax.experimental.pallas import tpu as pltpu

F32 = jnp.float32
BF16 = jnp.bfloat16

D_MODEL = 1024
HEAD_DIM = 64
A_Q_HEADS = 8
A_KV_HEADS = 2
A_GROUP = A_Q_HEADS // A_KV_HEADS
B_GROUPS = ((128, 1), (512, 4), (2048, 16))
B_HEADS_PER_GROUP = 4
GRID_W = 64
A_ROPE_THETA = 10000.0
B_ROPE_THETA = 500000.0
B_ROPE_DIMS = HEAD_DIM // 4
N_EXPERTS = 16
EC_CAPACITY_FACTOR = 2
D_FF = 2 * D_MODEL
LN_EPS = 1e-5
QK_NORM_EPS = 1e-6
MASK_VALUE = -1e30
DEPTH = 1
DEEPNORM_ALPHA = (2.0 * DEPTH) ** 0.25
LOG2_E = 1.4426950408889634
Q_SCALE = HEAD_DIM ** -0.5 * LOG2_E

A_Q_W = A_Q_HEADS * HEAD_DIM
A_KV_W = A_KV_HEADS * HEAD_DIM
B_GROUP_W = B_HEADS_PER_GROUP * HEAD_DIM
B_W = B_GROUP_W * len(B_GROUPS)
QKV_W = A_Q_W + 2 * A_KV_W + 3 * B_W
BAND_HALF = 64

LANES = 128
VMEM_LIMIT = 56 * 1024 * 1024

NT_DIMS = (((1,), (1,)), ((), ()))


def _params(n_axes):
    return pltpu.CompilerParams(dimension_semantics=("arbitrary",) * n_axes, vmem_limit_bytes=VMEM_LIMIT)


def _layer_norm(x, g, b):
    mu = jnp.mean(x, axis=-1, keepdims=True)
    xc = x - mu
    var = jnp.mean(xc * xc, axis=-1, keepdims=True)
    return xc * lax.rsqrt(var + LN_EPS) * g + b


def _sigmoid(x):
    return 0.5 * jnp.tanh(0.5 * x) + 0.5


def _split_bf16(x):
    hi = x.astype(BF16)
    lo = (x - hi.astype(F32)).astype(BF16)
    return hi, lo


def _swap_halves(y, lane, period, half):
    fwd = pltpu.roll(y, LANES - half, 1)
    bwd = pltpu.roll(y, half, 1)
    return jnp.where((lane % period) < half, fwd, bwd)


def _inproj_kernel(x_ref, g0_ref, b0_ref, w_ref, qng_ref, kng_ref, cosa_ref, sina_ref, cosb_ref, sinb_ref,
                   qa_ref, ka_ref, va_ref, qb0_ref, qb1_ref, qb2_ref, kb0_ref, kb1_ref, kb2_ref,
                   vb0_ref, vb1_ref, vb2_ref, h_ref):
    step = pl.program_id(0)

    @pl.when(step == 0)
    def _():
        h_ref[1] = jnp.zeros(h_ref.shape[1:], BF16)

    h = h_ref[(step + 1) % 2]
    lane = lax.broadcasted_iota(jnp.int32, (1, LANES), 1)
    r = lax.broadcasted_iota(jnp.int32, (LANES, LANES), 0)
    c = lax.broadcasted_iota(jnp.int32, (LANES, LANES), 1)
    head_sum = jnp.where((r // HEAD_DIM) == (c // HEAD_DIM), 1.0, 0.0).astype(BF16)
    cosa, sina = cosa_ref[...], sina_ref[...]
    cosb, sinb = cosb_ref[...], sinb_ref[...]

    def proj(off):
        y = jnp.dot(h, w_ref[:, off:off + 2 * LANES], preferred_element_type=F32)
        return y[:, :LANES], y[:, LANES:]

    def norm_rope_a(y, gain, scale):
        hi, lo = _split_bf16(y * y)
        ss = (jnp.dot(hi, head_sum, preferred_element_type=F32)
              + jnp.dot(lo, head_sum, preferred_element_type=F32))
        y = y * lax.rsqrt(ss * (1.0 / HEAD_DIM) + QK_NORM_EPS) * gain
        y = y * cosa + _swap_halves(y, lane, HEAD_DIM // 2, HEAD_DIM // 4) * sina
        return (y * scale).astype(BF16)

    def rope_b(y, scale):
        y = y * cosb + _swap_halves(y, lane, HEAD_DIM, B_ROPE_DIMS // 2) * sinb
        return y * scale

    def project_b(gi, q_ref, k_ref, v_ref):
        off = A_Q_W + 2 * A_KV_W + gi * B_GROUP_W
        for j, (q, k, v) in enumerate(zip(proj(off), proj(off + B_W), proj(off + 2 * B_W))):
            q_ref[j] = rope_b(q, Q_SCALE)
            k_ref[j] = rope_b(k, 1.0)
            v_ref[j] = v

    qa_tiles = [y for j in range(A_Q_W // (2 * LANES)) for y in proj(2 * j * LANES)]
    ka, va = proj(A_Q_W)
    project_b(0, qb0_ref, kb0_ref, vb0_ref)
    for j, y in enumerate(qa_tiles):
        qa_ref[:, j * LANES:(j + 1) * LANES] = norm_rope_a(y, qng_ref[...], Q_SCALE)
    ka_ref[...] = norm_rope_a(ka, kng_ref[...], 1.0)
    va_ref[...] = va.astype(BF16)
    project_b(1, qb1_ref, kb1_ref, vb1_ref)
    project_b(2, qb2_ref, kb2_ref, vb2_ref)

    h_ref[step % 2] = _layer_norm(x_ref[...], g0_ref[...], b0_ref[...]).astype(BF16)


def _rope_tables(seq):
    lane = jnp.arange(LANES)
    d = lane % HEAD_DIM
    t = jnp.arange(seq)
    half_rot = HEAD_DIM // 2
    inv_a = A_ROPE_THETA ** (-jnp.arange(0, half_rot, 2, dtype=F32) / half_rot)
    j = d % half_rot
    pos_row = (t // GRID_W).astype(F32)
    pos_col = (t % GRID_W).astype(F32)
    freq_a = inv_a[j % (half_rot // 2)]
    ang_a = jnp.where((d < half_rot)[None, :], pos_row[:, None], pos_col[:, None]) * freq_a[None, :]
    sign_a = jnp.where(j < half_rot // 2, -1.0, 1.0).astype(F32)
    cosa = jnp.cos(ang_a)
    sina = jnp.sin(ang_a) * sign_a[None, :]
    inv_b = B_ROPE_THETA ** (-jnp.arange(0, B_ROPE_DIMS, 2, dtype=F32) / B_ROPE_DIMS)
    freq_b = inv_b[d % (B_ROPE_DIMS // 2)]
    ang_b = t.astype(F32)[:, None] * freq_b[None, :]
    rot = (d < B_ROPE_DIMS)[None, :]
    sign_b = jnp.where(d < B_ROPE_DIMS // 2, -1.0, 1.0).astype(F32)
    cosb = jnp.where(rot, jnp.cos(ang_b), 1.0)
    sinb = jnp.where(rot, jnp.sin(ang_b) * sign_b[None, :], 0.0)
    return cosa, sina, cosb, sinb


def _in_projection(x2, ln0_g, ln0_b, w_qkv, qn_g, kn_g, seq, tm):
    m = x2.shape[0]
    cosa, sina, cosb, sinb = _rope_tables(seq)
    n_tiles = m // tm
    tiles_per_seq = seq // tm
    dst = lambda i: jnp.maximum(i - 1, 0)
    row = lambda i: (dst(i), 0)
    const = lambda i: (0, 0)
    table = lambda i: (dst(i) % tiles_per_seq, 0)
    widths = (A_Q_W, A_KV_W, A_KV_W) + (B_GROUP_W,) * 9
    gain_tile = lambda g: jnp.tile(g.reshape(1, HEAD_DIM), (1, LANES // HEAD_DIM))
    return pl.pallas_call(
        _inproj_kernel,
        grid=(n_tiles + 1,),
        in_specs=[
            pl.BlockSpec((tm, D_MODEL), lambda i: (jnp.minimum(i, n_tiles - 1), 0)),
            pl.BlockSpec((1, D_MODEL), const),
            pl.BlockSpec((1, D_MODEL), const),
            pl.BlockSpec((D_MODEL, QKV_W), const),
            pl.BlockSpec((1, LANES), const),
            pl.BlockSpec((1, LANES), const),
            pl.BlockSpec((tm, LANES), table),
            pl.BlockSpec((tm, LANES), table),
            pl.BlockSpec((tm, LANES), table),
            pl.BlockSpec((tm, LANES), table),
        ],
        out_specs=([pl.BlockSpec((tm, w), row) for w in widths[:3]]
                   + [pl.BlockSpec((B_GROUP_W // LANES, tm, LANES), lambda i: (0, dst(i), 0))] * 9),
        out_shape=([jax.ShapeDtypeStruct((m, w), BF16) for w in widths[:3]]
                   + [jax.ShapeDtypeStruct((B_GROUP_W // LANES, m, LANES), F32)] * 9),
        scratch_shapes=[pltpu.VMEM((2, tm, D_MODEL), BF16)],
        compiler_params=_params(1),
        name="in_projection",
    )(x2, ln0_g.reshape(1, D_MODEL), ln0_b.reshape(1, D_MODEL), w_qkv, gain_tile(qn_g), gain_tile(kn_g),
      cosa, sina, cosb, sinb)


def _attn_a_kernel(q_ref, k_ref, v_ref, wg_ref, wu_ref, wd_ref, o_ref, wgb_ref, wub_ref, wdb_ref):
    wgb_ref[...] = wg_ref[...].astype(BF16)
    wub_ref[...] = wu_ref[...].astype(BF16)
    wdb_ref[...] = wd_ref[...].astype(BF16)
    def scores(hq):
        kh = hq // A_GROUP
        q = q_ref[0, :, hq * HEAD_DIM:(hq + 1) * HEAD_DIM]
        k = k_ref[0, :, kh * HEAD_DIM:(kh + 1) * HEAD_DIM]
        return lax.dot_general(q, k, NT_DIMS, preferred_element_type=F32)

    outs = []
    s = scores(0)
    for hq in range(A_Q_HEADS):
        s_next = scores(hq + 1) if hq + 1 < A_Q_HEADS else None
        kh = hq // A_GROUP
        v = v_ref[0, :, kh * HEAD_DIM:(kh + 1) * HEAD_DIM]
        p = jnp.exp2(s - jnp.max(s, axis=-1, keepdims=True))
        denom = jnp.sum(p, axis=-1, keepdims=True)
        o = jnp.dot(p.astype(BF16), v, preferred_element_type=F32)
        outs.append(o / denom)
        s = s_next
    o_ref[0] = jnp.concatenate(outs, axis=1).astype(BF16)


def _attention_a(qa, ka, va, expert_weights, tq):
    b, s, _ = qa.shape
    steps = b * (s // tq)
    flat = [w.reshape(-1, w.shape[-1]) for w in expert_weights]
    wslice = lambda w: pl.BlockSpec((w.shape[0] // steps, w.shape[1]), lambda i, j: (i * (s // tq) + j, 0))
    o, *rounded = pl.pallas_call(
        _attn_a_kernel,
        grid=(b, s // tq),
        in_specs=[
            pl.BlockSpec((1, tq, A_Q_W), lambda i, j: (i, j, 0)),
            pl.BlockSpec((1, s, A_KV_W), lambda i, j: (i, 0, 0)),
            pl.BlockSpec((1, s, A_KV_W), lambda i, j: (i, 0, 0)),
            *map(wslice, flat),
        ],
        out_specs=[pl.BlockSpec((1, tq, A_Q_W), lambda i, j: (i, j, 0)), *map(wslice, flat)],
        out_shape=[jax.ShapeDtypeStruct((b, s, A_Q_W), BF16),
                   *[jax.ShapeDtypeStruct(w.shape, BF16) for w in flat]],
        compiler_params=_params(2),
        name="attention_a",
    )(qa, ka, va, *flat)
    return o, [r.reshape(w.shape) for r, w in zip(rounded, expert_weights)]


BAND_QB = 128
BAND_UNITS_PER_STEP = 8
BAND_STEPS = 2


def _band_bias(n):
    win = min(2 * BAND_QB, n)
    rows = B_HEADS_PER_GROUP * BAND_QB
    rel = (jnp.arange(rows)[:, None] % BAND_QB) - jnp.arange(win)[None, :]
    shifts = jnp.arange(2 * BAND_QB // BAND_HALF - 1) * BAND_HALF
    inside = jnp.abs(rel[None] + shifts[:, None, None]) <= BAND_HALF
    return jnp.where(inside, 0.0, MASK_VALUE).astype(F32)


def _dilated_kernel(q_ref, k_ref, v_ref, bias_ref, o_ref, lse_ref, *, n, dilation):
    win = min(2 * BAND_QB, n)
    n_blk = n // BAND_QB
    lane_head = lax.broadcasted_iota(jnp.int32, (1, B_GROUP_W), 1) // HEAD_DIM

    planes = B_GROUP_W // LANES

    def class_rows(r, start, size):
        if dilation == 1:
            return pl.ds(pl.multiple_of(start, BAND_HALF), size)
        return pl.ds(r + start * dilation, size, stride=dilation)

    def load(ref, rows_):
        return jnp.concatenate([ref[j, rows_, :] for j in range(planes)], axis=1)

    def scores(i):
        unit = pl.program_id(1) * BAND_UNITS_PER_STEP + i
        r = unit // n_blk
        qs = (unit % n_blk) * BAND_QB
        ks = jnp.clip(qs - BAND_HALF, 0, n - win)
        q = load(q_ref, class_rows(r, qs, BAND_QB))
        k = load(k_ref, class_rows(r, ks, win)).astype(BF16)
        q_stack = jnp.concatenate([jnp.where(lane_head == h, q, 0.0) for h in range(B_HEADS_PER_GROUP)], axis=0)
        s = lax.dot_general(q_stack.astype(BF16), k, NT_DIMS, preferred_element_type=F32)
        return r, qs, ks, s + bias_ref[(qs - ks) // BAND_HALF]

    nxt = scores(0)
    for i in range(BAND_UNITS_PER_STEP):
        r, qs, ks, s = nxt
        if i + 1 < BAND_UNITS_PER_STEP:
            nxt = scores(i + 1)
        v = load(v_ref, class_rows(r, ks, win)).astype(BF16)
        mx = jnp.max(s, axis=-1, keepdims=True)
        p = jnp.exp2(s - mx)
        denom = jnp.sum(p, axis=-1, keepdims=True)
        pv = jnp.dot(p.astype(BF16), v, preferred_element_type=F32) / denom
        lse = mx + jnp.log2(denom)
        o = jnp.zeros((BAND_QB, B_GROUP_W), F32)
        l = jnp.zeros((BAND_QB, B_GROUP_W), F32)
        for h in range(B_HEADS_PER_GROUP):
            blk = slice(h * BAND_QB, (h + 1) * BAND_QB)
            o = jnp.where(lane_head == h, pv[blk], o)
            l = jnp.where(lane_head == h, lse[blk], l)
        for j in range(planes):
            o_ref[j, class_rows(r, qs, BAND_QB), :] = o[:, j * LANES:(j + 1) * LANES]
            lse_ref[j, class_rows(r, qs, BAND_QB), :] = l[:, j * LANES:(j + 1) * LANES]


def _dilated_attention(q, k, v, seq, dilation):
    planes, tokens, _ = q.shape
    n = seq // dilation
    assert dilation * (n // BAND_QB) == BAND_UNITS_PER_STEP * BAND_STEPS
    blk = pl.BlockSpec((planes, seq, LANES), lambda i, j: (0, i, 0))
    bias = _band_bias(n)
    return pl.pallas_call(
        functools.partial(_dilated_kernel, n=n, dilation=dilation),
        grid=(tokens // seq, BAND_STEPS),
        in_specs=[blk, blk, blk, pl.BlockSpec(bias.shape, lambda i, j: (0, 0, 0))],
        out_specs=[blk, blk],
        out_shape=[jax.ShapeDtypeStruct(q.shape, F32)] * 2,
        compiler_params=_params(2),
        name=f"dilated_attention_d{dilation}",
    )(q, k, v, bias)


MERGE_SUB_ROWS = 512


def _merge_kernel(x_ref, g0_ref, b0_ref, wg_ref, bg_ref, oa_ref, o0_ref, o1_ref, o2_ref, l0_ref, l1_ref, l2_ref,
                  wa_ref, wb_ref, wo_ref, g1_ref, b1_ref, wr_hi_ref, wr_lo_ref,
                  h1_ref, h1b_ref, logit_ref, z_ref):
    i = pl.program_id(0)

    @pl.when(i == 0)
    def _():
        z_ref[1] = jnp.zeros(z_ref.shape[1:], F32)

    planes = lambda ref: jnp.concatenate([ref[j] for j in range(ref.shape[0])], axis=1)
    ya = jnp.dot(oa_ref[...], wa_ref[...], preferred_element_type=F32)
    h0 = _layer_norm(x_ref[...], g0_ref[...], b0_ref[...])
    gates = _sigmoid(jnp.dot(h0.astype(BF16), wg_ref[...], preferred_element_type=F32) + bg_ref[...])

    h1 = _layer_norm(z_ref[(i + 1) % 2], g1_ref[...], b1_ref[...])
    h1_ref[...] = h1
    h1b_ref[...] = h1.astype(BF16)
    hi, lo = _split_bf16(h1)
    logit_ref[0] = (lax.dot_general(wr_hi_ref[...], hi, NT_DIMS, preferred_element_type=F32)
                    + lax.dot_general(wr_hi_ref[...], lo, NT_DIMS, preferred_element_type=F32)
                    + lax.dot_general(wr_lo_ref[...], hi, NT_DIMS, preferred_element_type=F32))

    l0, l1, l2 = planes(l0_ref), planes(l1_ref), planes(l2_ref)
    mx = jnp.maximum(jnp.maximum(l0, l1), l2)
    e0, e1, e2 = jnp.exp2(l0 - mx), jnp.exp2(l1 - mx), jnp.exp2(l2 - mx)
    ob = (e0 * planes(o0_ref) + e1 * planes(o1_ref) + e2 * planes(o2_ref)) / (e0 + e1 + e2)
    yb = jnp.dot(ob.astype(BF16), wb_ref[...], preferred_element_type=F32)
    merged = gates[:, :D_MODEL] * ya + gates[:, D_MODEL:] * yb
    mix = jnp.dot(merged.astype(BF16), wo_ref[...], preferred_element_type=F32)
    z_ref[i % 2] = DEEPNORM_ALPHA * h0 + mix


def _merge(x2, ln0_g, ln0_b, w_gates, b_gate, oa, obs, lses, w_a, w_b, w_o, ln1_g, ln1_b, w_router, seq, tm):
    m = x2.shape[0]
    n_tiles = m // tm
    tiles_per_seq = seq // tm
    src = lambda i: jnp.minimum(i, n_tiles - 1)
    dst = lambda i: jnp.maximum(i - 1, 0)
    const = lambda i: (0, 0)
    wr_t = w_router.T
    wr_hi = wr_t.astype(BF16)
    wr_lo = (wr_t - wr_hi.astype(F32)).astype(BF16)
    vec = lambda v: v.reshape(1, -1)
    full = lambda a: pl.BlockSpec(a.shape, const)
    tile_in = lambda w: pl.BlockSpec((tm, w), lambda i: (src(i), 0))
    tile_out = lambda w: pl.BlockSpec((tm, w), lambda i: (dst(i), 0))
    plane_in = pl.BlockSpec((B_GROUP_W // LANES, tm, LANES), lambda i: (0, src(i), 0))
    args = [x2, vec(ln0_g), vec(ln0_b), w_gates, vec(b_gate), oa, *obs, *lses, w_a, w_b, w_o,
            vec(ln1_g), vec(ln1_b), wr_hi, wr_lo]
    in_specs = [tile_in(D_MODEL), full(args[1]), full(args[2]), full(w_gates), full(args[4]), tile_in(A_Q_W),
                *[plane_in] * 6, full(w_a), full(w_b), full(w_o), full(args[15]), full(args[16]),
                full(wr_hi), full(wr_lo)]
    return pl.pallas_call(
        _merge_kernel,
        grid=(n_tiles + 1,),
        in_specs=in_specs,
        out_specs=[tile_out(D_MODEL), tile_out(D_MODEL),
                   pl.BlockSpec((1, N_EXPERTS, tm),
                                lambda i: (dst(i) // tiles_per_seq, 0, dst(i) % tiles_per_seq))],
        out_shape=[jax.ShapeDtypeStruct((m, D_MODEL), F32), jax.ShapeDtypeStruct((m, D_MODEL), BF16),
                   jax.ShapeDtypeStruct((m // seq, N_EXPERTS, seq), F32)],
        scratch_shapes=[pltpu.VMEM((2, tm, D_MODEL), F32)],
        compiler_params=_params(1),
        name="merge_ln1_router",
    )(*args)


PREFIX_CHUNK = 256


def _prefix_count(mask, tri):
    ones = jnp.where(mask, 1.0, 0.0)
    carry = jnp.zeros((mask.shape[0], 1), F32)
    outs, bounds = [], [carry]
    for j in range(mask.shape[1] // PREFIX_CHUNK):
        chunk = ones[:, j * PREFIX_CHUNK:(j + 1) * PREFIX_CHUNK]
        outs.append(jnp.dot(chunk.astype(BF16), tri, preferred_element_type=F32) + carry)
        carry = carry + jnp.sum(chunk, axis=1, keepdims=True)
        bounds.append(carry)
    return jnp.concatenate(outs, axis=1), bounds


THRESHOLD_BITS = 31
THRESHOLD_REFINE = 16


def _route_kernel(logit_ref, slot_ref, aff_ref, starts_ref, *, cap):
    n_seq = logit_ref.shape[0]
    affs = []
    for b in range(n_seq):
        lg = logit_ref[b]
        ex = jnp.exp(lg - jnp.max(lg, axis=0, keepdims=True))
        affs.append(ex / jnp.sum(ex, axis=0, keepdims=True))
    aff = jnp.concatenate(affs, axis=0)

    def count_ge(t):
        return jnp.sum(jnp.where(aff >= t, 1.0, 0.0), axis=1, keepdims=True)

    thr = jnp.zeros((aff.shape[0], 1), jnp.int32)
    for bit in range(THRESHOLD_BITS - 1, -1, -1):
        cand = thr | (1 << bit)
        thr = jnp.where(count_ge(pltpu.bitcast(cand, F32)) >= cap, cand, thr)
    lo = pltpu.bitcast(thr, F32)
    hi = pltpu.bitcast(thr + 1, F32)
    for _ in range(THRESHOLD_REFINE):
        mid = 0.5 * (lo + hi)
        take = count_ge(mid) >= cap
        lo = jnp.where(take, mid, lo)
        hi = jnp.where(take, hi, mid)
    above = aff >= hi
    tied = (aff >= lo) & (aff < hi)
    r = lax.broadcasted_iota(jnp.int32, (PREFIX_CHUNK, PREFIX_CHUNK), 0)
    c = lax.broadcasted_iota(jnp.int32, (PREFIX_CHUNK, PREFIX_CHUNK), 1)
    tri = jnp.where(r < c, 1.0, 0.0).astype(BF16)
    need = cap - jnp.sum(jnp.where(above, 1.0, 0.0), axis=1, keepdims=True)
    sel = above | (tied & (_prefix_count(tied, tri)[0] < need))
    rank, bounds = _prefix_count(sel, tri)
    slot = jnp.where(sel, rank, -1.0).astype(jnp.int32)
    lane = lax.broadcasted_iota(jnp.int32, (1, LANES), 1)
    starts = jnp.zeros((aff.shape[0], LANES), F32)
    for j, count in enumerate(bounds):
        starts = jnp.where(lane == j, count, starts)
    for b in range(n_seq):
        rows = slice(b * N_EXPERTS, (b + 1) * N_EXPERTS)
        slot_ref[b] = slot[rows]
        aff_ref[b] = aff[rows]
        starts_ref[b] = starts[rows].astype(jnp.int32)


ROUTE_SEQS_PER_STEP = 4


def _route(logits, batch, seq, cap):
    nb = ROUTE_SEQS_PER_STEP if batch % ROUTE_SEQS_PER_STEP == 0 else 1
    blk = pl.BlockSpec((nb, N_EXPERTS, seq), lambda i: (i, 0, 0))
    return pl.pallas_call(
        functools.partial(_route_kernel, cap=cap),
        grid=(batch // nb,),
        in_specs=[blk],
        out_specs=[blk, blk, pl.BlockSpec((nb, N_EXPERTS, LANES), lambda i: (i, 0, 0))],
        out_shape=[jax.ShapeDtypeStruct((batch, N_EXPERTS, seq), jnp.int32),
                   jax.ShapeDtypeStruct((batch, N_EXPERTS, seq), F32),
                   jax.ShapeDtypeStruct((batch, N_EXPERTS, LANES), jnp.int32)],
        compiler_params=_params(1),
        name="route",
    )(logits)


BF16_SUBLANES = 16
DISPATCH_WIN = 64


def _slot_windows(starts_ref, b, j, n_bounds, win):
    base, n_win = [], 0
    for e in range(N_EXPERTS):
        at = (b * N_EXPERTS + e) * n_bounds + j
        first = (starts_ref[at] // BF16_SUBLANES) * BF16_SUBLANES
        base.append(first)
        n_win = jnp.maximum(n_win, (starts_ref[at + 1] - first + win - 1) // win)
    return base, n_win


def _window_rows(base_e, k, wrow, cap, win):
    lower = base_e + k * win
    off = pl.multiple_of(jnp.minimum(lower, cap - win), BF16_SUBLANES)
    return off, jnp.where(wrow + off >= lower, wrow + off, -2)


def _dispatch_kernel(starts_ref, h_ref, slot_ref, aff_ref, xe_ref, gate_ref, *, cap):
    b, j = pl.program_id(0), pl.program_id(1)
    win = DISPATCH_WIN

    @pl.when(j == 0)
    def _():
        xe_ref[...] = jnp.zeros_like(xe_ref)
        gate_ref[...] = jnp.zeros_like(gate_ref)

    base, n_win = _slot_windows(starts_ref, b, j, pl.num_programs(1) + 1, win)
    h = h_ref[0]
    wrow = lax.broadcasted_iota(jnp.int32, (win, h.shape[0]), 0)

    def window(k):
        offs, onehots = [], []
        for e in range(N_EXPERTS):
            off, target = _window_rows(base[e], k, wrow, cap, win)
            offs.append(off)
            onehots.append(slot_ref[0, e:e + 1, :] == target)
        sel = jnp.concatenate([jnp.where(o, 1.0, 0.0) for o in onehots], axis=0).astype(BF16)
        rows = jnp.dot(sel, h, preferred_element_type=F32).astype(BF16)
        for e in range(N_EXPERTS):
            at = pl.ds(offs[e], win)
            xe_ref[e, at, :] += rows[e * win:(e + 1) * win]
            gate = jnp.sum(jnp.where(onehots[e], aff_ref[0, e:e + 1, :], 0.0), axis=1, keepdims=True)
            gate_ref[e, at, :] += jnp.broadcast_to(gate, (win, LANES))

    window(0)
    lax.fori_loop(1, n_win, lambda k, carry: (window(k), carry)[1], 0)


def _dispatch(h1b, slot, aff, starts, cap):
    b, s, _ = h1b.shape
    n_chunks = s // PREFIX_CHUNK
    chunk = pl.BlockSpec((1, N_EXPERTS, PREFIX_CHUNK), lambda i, j, st: (i, 0, j))
    grid_spec = pltpu.PrefetchScalarGridSpec(
        num_scalar_prefetch=1,
        grid=(b, n_chunks),
        in_specs=[pl.BlockSpec((1, PREFIX_CHUNK, D_MODEL), lambda i, j, st: (i, j, 0)), chunk, chunk],
        out_specs=[pl.BlockSpec((N_EXPERTS, cap, D_MODEL), lambda i, j, st: (0, i, 0)),
                   pl.BlockSpec((N_EXPERTS, cap, LANES), lambda i, j, st: (0, i, 0))],
    )
    return pl.pallas_call(
        functools.partial(_dispatch_kernel, cap=cap),
        grid_spec=grid_spec,
        out_shape=[jax.ShapeDtypeStruct((N_EXPERTS, b * cap, D_MODEL), BF16),
                   jax.ShapeDtypeStruct((N_EXPERTS, b * cap, LANES), F32)],
        compiler_params=_params(2),
        name="dispatch",
    )(starts[:, :, :n_chunks + 1].reshape(-1), h1b, slot, aff)


FF_CHUNK = 512


def _expert_kernel(x_ref, gate_ref, wg_ref, wu_ref, wd_ref, y_ref):
    x = x_ref[0]
    acc = jnp.zeros((x.shape[0], D_MODEL), F32)
    for c in range(D_FF // FF_CHUNK):
        cols = slice(c * FF_CHUNK, (c + 1) * FF_CHUNK)
        g = jnp.dot(x, wg_ref[0, :, cols], preferred_element_type=F32)
        u = jnp.dot(x, wu_ref[0, :, cols], preferred_element_type=F32)
        act = (g * jax.nn.sigmoid(g) * u).astype(BF16)
        acc = acc + jnp.dot(act, wd_ref[0, cols, :], preferred_element_type=F32)
    y_ref[0] = (acc * gate_ref[0][:, :1]).astype(BF16)


def _expert_ffn(xe, gate, w_gate, w_up, w_down, tm):
    e, rows, _ = xe.shape
    tok = lambda w: pl.BlockSpec((1, tm, w), lambda i, j: (i, j, 0))
    wspec = lambda a: pl.BlockSpec((1,) + a.shape[1:], lambda i, j: (i, 0, 0))
    return pl.pallas_call(
        _expert_kernel,
        grid=(e, rows // tm),
        in_specs=[tok(D_MODEL), tok(LANES), wspec(w_gate), wspec(w_up), wspec(w_down)],
        out_specs=tok(D_MODEL),
        out_shape=jax.ShapeDtypeStruct((e, rows, D_MODEL), BF16),
        compiler_params=_params(2),
        name="expert_ffn",
    )(xe, gate, w_gate, w_up, w_down)


COMBINE_WIN = 64
TN_DIMS = (((0,), (0,)), ((), ()))


def _combine_kernel(starts_ref, h1_ref, slot_ref, y_ref, g2_ref, b2_ref, out_ref, acc_ref, *, cap, n_chunks):
    s = pl.program_id(0)
    cur = jnp.minimum(s, pl.num_programs(0) - 2)
    win = COMBINE_WIN

    @pl.when(s == 0)
    def _():
        acc_ref[1] = jnp.zeros(acc_ref.shape[1:], F32)

    base, n_win = _slot_windows(starts_ref, cur // n_chunks, cur % n_chunks, n_chunks + 1, win)
    wrow = lax.broadcasted_iota(jnp.int32, (win, slot_ref.shape[2]), 0)

    def gather(k):
        sels, rows = [], []
        for e in range(N_EXPERTS):
            off, target = _window_rows(base[e], k, wrow, cap, win)
            sels.append(jnp.where(slot_ref[0, e:e + 1, :] == target, 1.0, 0.0))
            rows.append(y_ref[e, pl.ds(off, win), :])
        sel = jnp.concatenate(sels, axis=0).astype(BF16)
        return lax.dot_general(sel, jnp.concatenate(rows, axis=0), TN_DIMS, preferred_element_type=F32)

    out_ref[0] = _layer_norm(DEEPNORM_ALPHA * h1_ref[0] + acc_ref[(s + 1) % 2], g2_ref[...], b2_ref[...])
    acc_ref[s % 2] = gather(0)

    def more(k, carry):
        acc_ref[s % 2] += gather(k)
        return carry

    lax.fori_loop(1, n_win, more, 0)


def _combine(h1, slot, starts, y, ln2_g, ln2_b, cap):
    b, s, _ = h1.shape
    n_chunks = s // PREFIX_CHUNK
    n_steps = b * n_chunks
    cur = lambda i: jnp.minimum(i, n_steps - 1)
    prev = lambda i: jnp.maximum(i - 1, 0)
    const = lambda i, st: (0, 0)
    grid_spec = pltpu.PrefetchScalarGridSpec(
        num_scalar_prefetch=1,
        grid=(n_steps + 1,),
        in_specs=[pl.BlockSpec((1, PREFIX_CHUNK, D_MODEL), lambda i, st: (prev(i) // n_chunks, prev(i) % n_chunks, 0)),
                  pl.BlockSpec((1, N_EXPERTS, PREFIX_CHUNK), lambda i, st: (cur(i) // n_chunks, 0, cur(i) % n_chunks)),
                  pl.BlockSpec((N_EXPERTS, cap, D_MODEL), lambda i, st: (0, cur(i) // n_chunks, 0)),
                  pl.BlockSpec((1, D_MODEL), const), pl.BlockSpec((1, D_MODEL), const)],
        out_specs=pl.BlockSpec((1, PREFIX_CHUNK, D_MODEL),
                               lambda i, st: (prev(i) // n_chunks, prev(i) % n_chunks, 0)),
        scratch_shapes=[pltpu.VMEM((2, PREFIX_CHUNK, D_MODEL), F32)],
    )
    return pl.pallas_call(
        functools.partial(_combine_kernel, cap=cap, n_chunks=n_chunks),
        grid_spec=grid_spec,
        out_shape=jax.ShapeDtypeStruct((b, s, D_MODEL), F32),
        compiler_params=_params(1),
        name="combine_ln2",
    )(starts[:, :, :n_chunks + 1].reshape(-1), h1, slot, y, ln2_g.reshape(1, D_MODEL), ln2_b.reshape(1, D_MODEL))


def kernel(x, ln0_g, ln0_b, w_in, b_gate, qn_g, kn_g, w_branch_a, w_branch_b, w_out, ln1_g, ln1_b, w_router,
           w_gate_e, w_up_e, w_down_e, ln2_g, ln2_b):
    batch, seq, _ = x.shape
    assert w_in.shape[0] == DEPTH
    cap = EC_CAPACITY_FACTOR * seq // N_EXPERTS
    tokens = batch * seq
    x2 = x.reshape(tokens, D_MODEL)
    w_in_b = w_in[0].astype(BF16)

    qkv = _in_projection(x2, ln0_g, ln0_b, w_in_b[:, :QKV_W], qn_g[0], kn_g[0], seq, tm=512)
    qa, ka, va = qkv[:3]
    qb, kb, vb = qkv[3:6], qkv[6:9], qkv[9:12]

    oa, (wg_b, wu_b, wd_b) = _attention_a(qa.reshape(batch, seq, A_Q_W), ka.reshape(batch, seq, A_KV_W),
                                          va.reshape(batch, seq, A_KV_W), (w_gate_e[0], w_up_e[0], w_down_e[0]),
                                          tq=512)
    oa = oa.reshape(tokens, A_Q_W)

    obs, lses = [], []
    for gi, (window, dilation) in enumerate(B_GROUPS):
        assert window // (2 * dilation) == BAND_HALF
        o, lse = _dilated_attention(qb[gi], kb[gi], vb[gi], seq, dilation)
        obs.append(o)
        lses.append(lse)

    h1, h1b, logits = _merge(x2, ln0_g, ln0_b, w_in_b[:, QKV_W:], b_gate[0], oa, obs, lses,
                             w_branch_a[0].astype(BF16), w_branch_b[0].astype(BF16), w_out[0].astype(BF16),
                             ln1_g[0], ln1_b[0], w_router[0], seq, tm=512)

    slot, aff, starts = _route(logits, batch, seq, cap)
    xe, gate = _dispatch(h1b.reshape(batch, seq, D_MODEL), slot, aff, starts, cap)
    y = _expert_ffn(xe, gate, wg_b, wu_b, wd_b, tm=min(1024, batch * cap))
    return _combine(h1.reshape(batch, seq, D_MODEL), slot, starts, y, ln2_g[0], ln2_b[0], cap)
```

```python
import functools

import jax
import jax.numpy as jnp
from jax import lax
from jax.experimental import pallas as pl
from jax.experimental.pallas import tpu as pltpu

F32 = jnp.float32
BF16 = jnp.bfloat16

D_MODEL = 1024
HEAD_DIM = 64
A_Q_HEADS = 8
A_KV_HEADS = 2
A_GROUP = A_Q_HEADS // A_KV_HEADS
B_GROUPS = ((128, 1), (512, 4), (2048, 16))
B_HEADS_PER_GROUP = 4
GRID_W = 64
A_ROPE_THETA = 10000.0
B_ROPE_THETA = 500000.0
B_ROPE_DIMS = HEAD_DIM // 4
N_EXPERTS = 16
EC_CAPACITY_FACTOR = 2
D_FF = 2 * D_MODEL
LN_EPS = 1e-5
QK_NORM_EPS = 1e-6
MASK_VALUE = -1e30
DEPTH = 1
DEEPNORM_ALPHA = (2.0 * DEPTH) ** 0.25
LOG2_E = 1.4426950408889634
Q_SCALE = HEAD_DIM ** -0.5 * LOG2_E

A_Q_W = A_Q_HEADS * HEAD_DIM
A_KV_W = A_KV_HEADS * HEAD_DIM
B_GROUP_W = B_HEADS_PER_GROUP * HEAD_DIM
B_W = B_GROUP_W * len(B_GROUPS)
QKV_W = A_Q_W + 2 * A_KV_W + 3 * B_W
BAND_HALF = 64

LANES = 128
VMEM_LIMIT = 56 * 1024 * 1024

NT_DIMS = (((1,), (1,)), ((), ()))


def _params(n_axes):
    return pltpu.CompilerParams(dimension_semantics=("arbitrary",) * n_axes, vmem_limit_bytes=VMEM_LIMIT)


def _layer_norm(x, g, b):
    mu = jnp.mean(x, axis=-1, keepdims=True)
    xc = x - mu
    var = jnp.mean(xc * xc, axis=-1, keepdims=True)
    return xc * lax.rsqrt(var + LN_EPS) * g + b


def _sigmoid(x):
    return 0.5 * jnp.tanh(0.5 * x) + 0.5


def _split_bf16(x):
    hi = x.astype(BF16)
    lo = (x - hi.astype(F32)).astype(BF16)
    return hi, lo


def _swap_halves(y, lane, period, half):
    fwd = pltpu.roll(y, LANES - half, 1)
    bwd = pltpu.roll(y, half, 1)
    return jnp.where((lane % period) < half, fwd, bwd)


def _inproj_kernel(x_ref, g0_ref, b0_ref, w_ref, qng_ref, kng_ref, cosa_ref, sina_ref, cosb_ref, sinb_ref,
                   qa_ref, ka_ref, va_ref, qb0_ref, qb1_ref, qb2_ref, kb0_ref, kb1_ref, kb2_ref,
                   vb0_ref, vb1_ref, vb2_ref, h_ref):
    step = pl.program_id(0)

    @pl.when(step == 0)
    def _():
        h_ref[1] = jnp.zeros(h_ref.shape[1:], BF16)

    h = h_ref[(step + 1) % 2]
    lane = lax.broadcasted_iota(jnp.int32, (1, LANES), 1)
    r = lax.broadcasted_iota(jnp.int32, (LANES, LANES), 0)
    c = lax.broadcasted_iota(jnp.int32, (LANES, LANES), 1)
    head_sum = jnp.where((r // HEAD_DIM) == (c // HEAD_DIM), 1.0, 0.0).astype(BF16)
    cosa, sina = cosa_ref[...], sina_ref[...]
    cosb, sinb = cosb_ref[...], sinb_ref[...]

    def proj(off):
        y = jnp.dot(h, w_ref[:, off:off + 2 * LANES], preferred_element_type=F32)
        return y[:, :LANES], y[:, LANES:]

    def norm_rope_a(y, gain, scale):
        hi, lo = _split_bf16(y * y)
        ss = (jnp.dot(hi, head_sum, preferred_element_type=F32)
              + jnp.dot(lo, head_sum, preferred_element_type=F32))
        y = y * lax.rsqrt(ss * (1.0 / HEAD_DIM) + QK_NORM_EPS) * gain
        y = y * cosa + _swap_halves(y, lane, HEAD_DIM // 2, HEAD_DIM // 4) * sina
        return (y * scale).astype(BF16)

    def rope_b(y, scale):
        y = y * cosb + _swap_halves(y, lane, HEAD_DIM, B_ROPE_DIMS // 2) * sinb
        return y * scale

    def project_b(gi, q_ref, k_ref, v_ref):
        off = A_Q_W + 2 * A_KV_W + gi * B_GROUP_W
        for j, (q, k, v) in enumerate(zip(proj(off), proj(off + B_W), proj(off + 2 * B_W))):
            q_ref[j] = rope_b(q, Q_SCALE)
            k_ref[j] = rope_b(k, 1.0)
            v_ref[j] = v

    qa_tiles = [y for j in range(A_Q_W // (2 * LANES)) for y in proj(2 * j * LANES)]
    ka, va = proj(A_Q_W)
    project_b(0, qb0_ref, kb0_ref, vb0_ref)
    for j, y in enumerate(qa_tiles):
        qa_ref[:, j * LANES:(j + 1) * LANES] = norm_rope_a(y, qng_ref[...], Q_SCALE)
    ka_ref[...] = norm_rope_a(ka, kng_ref[...], 1.0)
    va_ref[...] = va.astype(BF16)
    project_b(1, qb1_ref, kb1_ref, vb1_ref)
    project_b(2, qb2_ref, kb2_ref, vb2_ref)

    h_ref[step % 2] = _layer_norm(x_ref[...], g0_ref[...], b0_ref[...]).astype(BF16)


def _rope_tables(seq):
    lane = jnp.arange(LANES)
    d = lane % HEAD_DIM
    t = jnp.arange(seq)
    half_rot = HEAD_DIM // 2
    inv_a = A_ROPE_THETA ** (-jnp.arange(0, half_rot, 2, dtype=F32) / half_rot)
    j = d % half_rot
    pos_row = (t // GRID_W).astype(F32)
    pos_col = (t % GRID_W).astype(F32)
    freq_a = inv_a[j % (half_rot // 2)]
    ang_a = jnp.where((d < half_rot)[None, :], pos_row[:, None], pos_col[:, None]) * freq_a[None, :]
    sign_a = jnp.where(j < half_rot // 2, -1.0, 1.0).astype(F32)
    cosa = jnp.cos(ang_a)
    sina = jnp.sin(ang_a) * sign_a[None, :]
    inv_b = B_ROPE_THETA ** (-jnp.arange(0, B_ROPE_DIMS, 2, dtype=F32) / B_ROPE_DIMS)
    freq_b = inv_b[d % (B_ROPE_DIMS // 2)]
    ang_b = t.astype(F32)[:, None] * freq_b[None, :]
    rot = (d < B_ROPE_DIMS)[None, :]
    sign_b = jnp.where(d < B_ROPE_DIMS // 2, -1.0, 1.0).astype(F32)
    cosb = jnp.where(rot, jnp.cos(ang_b), 1.0)
    sinb = jnp.where(rot, jnp.sin(ang_b) * sign_b[None, :], 0.0)
    return cosa, sina, cosb, sinb


def _in_projection(x2, ln0_g, ln0_b, w_qkv, qn_g, kn_g, seq, tm):
    m = x2.shape[0]
    cosa, sina, cosb, sinb = _rope_tables(seq)
    n_tiles = m // tm
    tiles_per_seq = seq // tm
    dst = lambda i: jnp.maximum(i - 1, 0)
    row = lambda i: (dst(i), 0)
    const = lambda i: (0, 0)
    table = lambda i: (dst(i) % tiles_per_seq, 0)
    widths = (A_Q_W, A_KV_W, A_KV_W) + (B_GROUP_W,) * 9
    gain_tile = lambda g: jnp.tile(g.reshape(1, HEAD_DIM), (1, LANES // HEAD_DIM))
    return pl.pallas_call(
        _inproj_kernel,
        grid=(n_tiles + 1,),
        in_specs=[
            pl.BlockSpec((tm, D_MODEL), lambda i: (jnp.minimum(i, n_tiles - 1), 0)),
            pl.BlockSpec((1, D_MODEL), const),
            pl.BlockSpec((1, D_MODEL), const),
            pl.BlockSpec((D_MODEL, QKV_W), const),
            pl.BlockSpec((1, LANES), const),
            pl.BlockSpec((1, LANES), const),
            pl.BlockSpec((tm, LANES), table),
            pl.BlockSpec((tm, LANES), table),
            pl.BlockSpec((tm, LANES), table),
            pl.BlockSpec((tm, LANES), table),
        ],
        out_specs=([pl.BlockSpec((tm, w), row) for w in widths[:3]]
                   + [pl.BlockSpec((B_GROUP_W // LANES, tm, LANES), lambda i: (0, dst(i), 0))] * 9),
        out_shape=([jax.ShapeDtypeStruct((m, w), BF16) for w in widths[:3]]
                   + [jax.ShapeDtypeStruct((B_GROUP_W // LANES, m, LANES), F32)] * 9),
        scratch_shapes=[pltpu.VMEM((2, tm, D_MODEL), BF16)],
        compiler_params=_params(1),
        name="in_projection",
    )(x2, ln0_g.reshape(1, D_MODEL), ln0_b.reshape(1, D_MODEL), w_qkv, gain_tile(qn_g), gain_tile(kn_g),
      cosa, sina, cosb, sinb)


def _attn_a_kernel(q_ref, k_ref, v_ref, wg_ref, wu_ref, wd_ref, o_ref, wgb_ref, wub_ref, wdb_ref):
    wgb_ref[...] = wg_ref[...].astype(BF16)
    wub_ref[...] = wu_ref[...].astype(BF16)
    wdb_ref[...] = wd_ref[...].astype(BF16)
    def scores(hq):
        kh = hq // A_GROUP
        q = q_ref[0, :, hq * HEAD_DIM:(hq + 1) * HEAD_DIM]
        k = k_ref[0, :, kh * HEAD_DIM:(kh + 1) * HEAD_DIM]
        return lax.dot_general(q, k, NT_DIMS, preferred_element_type=F32)

    outs = []
    v_ones = []
    for kh in range(A_KV_HEADS):
        v = v_ref[0, :, kh * HEAD_DIM:(kh + 1) * HEAD_DIM]
        v_ones.append(jnp.concatenate([v, jnp.ones_like(v)], axis=1))
    s = scores(0)
    for hq in range(A_Q_HEADS):
        s_next = scores(hq + 1) if hq + 1 < A_Q_HEADS else None
        kh = hq // A_GROUP
        p = jnp.exp2(s - jnp.max(s, axis=-1, keepdims=True)).astype(BF16)
        o = jnp.dot(p, v_ones[hq // A_GROUP], preferred_element_type=F32)
        outs.append(o[:, :HEAD_DIM] / o[:, HEAD_DIM:HEAD_DIM + 1])
        s = s_next
    o_ref[0] = jnp.concatenate(outs, axis=1).astype(BF16)


def _attention_a(qa, ka, va, expert_weights, tq):
    b, s, _ = qa.shape
    steps = b * (s // tq)
    flat = [w.reshape(-1, w.shape[-1]) for w in expert_weights]
    wslice = lambda w: pl.BlockSpec((w.shape[0] // steps, w.shape[1]), lambda i, j: (i * (s // tq) + j, 0))
    o, *rounded = pl.pallas_call(
        _attn_a_kernel,
        grid=(b, s // tq),
        in_specs=[
            pl.BlockSpec((1, tq, A_Q_W), lambda i, j: (i, j, 0)),
            pl.BlockSpec((1, s, A_KV_W), lambda i, j: (i, 0, 0)),
            pl.BlockSpec((1, s, A_KV_W), lambda i, j: (i, 0, 0)),
            *map(wslice, flat),
        ],
        out_specs=[pl.BlockSpec((1, tq, A_Q_W), lambda i, j: (i, j, 0)), *map(wslice, flat)],
        out_shape=[jax.ShapeDtypeStruct((b, s, A_Q_W), BF16),
                   *[jax.ShapeDtypeStruct(w.shape, BF16) for w in flat]],
        compiler_params=_params(2),
        name="attention_a",
    )(qa, ka, va, *flat)
    return o, [r.reshape(w.shape) for r, w in zip(rounded, expert_weights)]


BAND_QB = 128
BAND_UNITS_PER_STEP = 8
BAND_STEPS = 2


def _band_bias(n):
    win = min(2 * BAND_QB, n)
    rows = B_HEADS_PER_GROUP * BAND_QB
    rel = (jnp.arange(rows)[:, None] % BAND_QB) - jnp.arange(win)[None, :]
    shifts = jnp.arange(2 * BAND_QB // BAND_HALF - 1) * BAND_HALF
    inside = jnp.abs(rel[None] + shifts[:, None, None]) <= BAND_HALF
    return jnp.where(inside, 0.0, MASK_VALUE).astype(F32)


def _dilated_kernel(q_ref, k_ref, v_ref, bias_ref, o_ref, lse_ref, *, n, dilation):
    win = min(2 * BAND_QB, n)
    n_blk = n // BAND_QB
    lane_head = lax.broadcasted_iota(jnp.int32, (1, B_GROUP_W), 1) // HEAD_DIM

    planes = B_GROUP_W // LANES

    def class_rows(r, start, size):
        if dilation == 1:
            return pl.ds(pl.multiple_of(start, BAND_HALF), size)
        return pl.ds(r + start * dilation, size, stride=dilation)

    def load(ref, rows_):
        return jnp.concatenate([ref[j, rows_, :] for j in range(planes)], axis=1)

    def scores(i):
        unit = pl.program_id(1) * BAND_UNITS_PER_STEP + i
        r = unit // n_blk
        qs = (unit % n_blk) * BAND_QB
        ks = jnp.clip(qs - BAND_HALF, 0, n - win)
        q = load(q_ref, class_rows(r, qs, BAND_QB))
        k = load(k_ref, class_rows(r, ks, win)).astype(BF16)
        q_stack = jnp.concatenate([jnp.where(lane_head == h, q, 0.0) for h in range(B_HEADS_PER_GROUP)], axis=0)
        s = lax.dot_general(q_stack.astype(BF16), k, NT_DIMS, preferred_element_type=F32)
        return r, qs, ks, s + bias_ref[(qs - ks) // BAND_HALF]

    nxt = scores(0)
    for i in range(BAND_UNITS_PER_STEP):
        r, qs, ks, s = nxt
        if i + 1 < BAND_UNITS_PER_STEP:
            nxt = scores(i + 1)
        v = load(v_ref, class_rows(r, ks, win)).astype(BF16)
        mx = jnp.max(s, axis=-1, keepdims=True)
        p = jnp.exp2(s - mx)
        denom = jnp.sum(p, axis=-1, keepdims=True)
        pv = jnp.dot(p.astype(BF16), v, preferred_element_type=F32) / denom
        lse = mx + jnp.log2(denom)
        o = jnp.zeros((BAND_QB, B_GROUP_W), F32)
        l = jnp.zeros((BAND_QB, B_GROUP_W), F32)
        for h in range(B_HEADS_PER_GROUP):
            blk = slice(h * BAND_QB, (h + 1) * BAND_QB)
            o = jnp.where(lane_head == h, pv[blk], o)
            l = jnp.where(lane_head == h, lse[blk], l)
        for j in range(planes):
            o_ref[j, class_rows(r, qs, BAND_QB), :] = o[:, j * LANES:(j + 1) * LANES]
            lse_ref[j, class_rows(r, qs, BAND_QB), :] = l[:, j * LANES:(j + 1) * LANES]


def _dilated_attention(q, k, v, seq, dilation):
    planes, tokens, _ = q.shape
    n = seq // dilation
    assert dilation * (n // BAND_QB) == BAND_UNITS_PER_STEP * BAND_STEPS
    blk = pl.BlockSpec((planes, seq, LANES), lambda i, j: (0, i, 0))
    bias = _band_bias(n)
    return pl.pallas_call(
        functools.partial(_dilated_kernel, n=n, dilation=dilation),
        grid=(tokens // seq, BAND_STEPS),
        in_specs=[blk, blk, blk, pl.BlockSpec(bias.shape, lambda i, j: (0, 0, 0))],
        out_specs=[blk, blk],
        out_shape=[jax.ShapeDtypeStruct(q.shape, F32)] * 2,
        compiler_params=_params(2),
        name=f"dilated_attention_d{dilation}",
    )(q, k, v, bias)


MERGE_SUB_ROWS = 512


def _merge_kernel(x_ref, g0_ref, b0_ref, wg_ref, bg_ref, oa_ref, o0_ref, o1_ref, o2_ref, l0_ref, l1_ref, l2_ref,
                  wa_ref, wb_ref, wo_ref, g1_ref, b1_ref, wr_hi_ref, wr_lo_ref,
                  h1_ref, h1b_ref, logit_ref, z_ref):
    i = pl.program_id(0)

    @pl.when(i == 0)
    def _():
        z_ref[1] = jnp.zeros(z_ref.shape[1:], F32)

    planes = lambda ref: jnp.concatenate([ref[j] for j in range(ref.shape[0])], axis=1)
    ya = jnp.dot(oa_ref[...], wa_ref[...], preferred_element_type=F32)
    h0 = _layer_norm(x_ref[...], g0_ref[...], b0_ref[...])
    gates = _sigmoid(jnp.dot(h0.astype(BF16), wg_ref[...], preferred_element_type=F32) + bg_ref[...])

    h1 = _layer_norm(z_ref[(i + 1) % 2], g1_ref[...], b1_ref[...])
    h1_ref[...] = h1
    h1b_ref[...] = h1.astype(BF16)
    hi, lo = _split_bf16(h1)
    logit_ref[0] = (lax.dot_general(wr_hi_ref[...], hi, NT_DIMS, preferred_element_type=F32)
                    + lax.dot_general(wr_hi_ref[...], lo, NT_DIMS, preferred_element_type=F32)
                    + lax.dot_general(wr_lo_ref[...], hi, NT_DIMS, preferred_element_type=F32))

    l0, l1, l2 = planes(l0_ref), planes(l1_ref), planes(l2_ref)
    mx = jnp.maximum(jnp.maximum(l0, l1), l2)
    e0, e1, e2 = jnp.exp2(l0 - mx), jnp.exp2(l1 - mx), jnp.exp2(l2 - mx)
    ob = (e0 * planes(o0_ref) + e1 * planes(o1_ref) + e2 * planes(o2_ref)) / (e0 + e1 + e2)
    yb = jnp.dot(ob.astype(BF16), wb_ref[...], preferred_element_type=F32)
    merged = gates[:, :D_MODEL] * ya + gates[:, D_MODEL:] * yb
    mix = jnp.dot(merged.astype(BF16), wo_ref[...], preferred_element_type=F32)
    z_ref[i % 2] = DEEPNORM_ALPHA * h0 + mix


def _merge(x2, ln0_g, ln0_b, w_gates, b_gate, oa, obs, lses, w_a, w_b, w_o, ln1_g, ln1_b, w_router, seq, tm):
    m = x2.shape[0]
    n_tiles = m // tm
    tiles_per_seq = seq // tm
    src = lambda i: jnp.minimum(i, n_tiles - 1)
    dst = lambda i: jnp.maximum(i - 1, 0)
    const = lambda i: (0, 0)
    wr_t = w_router.T
    wr_hi = wr_t.astype(BF16)
    wr_lo = (wr_t - wr_hi.astype(F32)).astype(BF16)
    vec = lambda v: v.reshape(1, -1)
    full = lambda a: pl.BlockSpec(a.shape, const)
    tile_in = lambda w: pl.BlockSpec((tm, w), lambda i: (src(i), 0))
    tile_out = lambda w: pl.BlockSpec((tm, w), lambda i: (dst(i), 0))
    plane_in = pl.BlockSpec((B_GROUP_W // LANES, tm, LANES), lambda i: (0, src(i), 0))
    args = [x2, vec(ln0_g), vec(ln0_b), w_gates, vec(b_gate), oa, *obs, *lses, w_a, w_b, w_o,
            vec(ln1_g), vec(ln1_b), wr_hi, wr_lo]
    in_specs = [tile_in(D_MODEL), full(args[1]), full(args[2]), full(w_gates), full(args[4]), tile_in(A_Q_W),
                *[plane_in] * 6, full(w_a), full(w_b), full(w_o), full(args[15]), full(args[16]),
                full(wr_hi), full(wr_lo)]
    return pl.pallas_call(
        _merge_kernel,
        grid=(n_tiles + 1,),
        in_specs=in_specs,
        out_specs=[tile_out(D_MODEL), tile_out(D_MODEL),
                   pl.BlockSpec((1, N_EXPERTS, tm),
                                lambda i: (dst(i) // tiles_per_seq, 0, dst(i) % tiles_per_seq))],
        out_shape=[jax.ShapeDtypeStruct((m, D_MODEL), F32), jax.ShapeDtypeStruct((m, D_MODEL), BF16),
                   jax.ShapeDtypeStruct((m // seq, N_EXPERTS, seq), F32)],
        scratch_shapes=[pltpu.VMEM((2, tm, D_MODEL), F32)],
        compiler_params=_params(1),
        name="merge_ln1_router",
    )(*args)


PREFIX_CHUNK = 256


def _prefix_count(mask, tri):
    ones = jnp.where(mask, 1.0, 0.0)
    carry = jnp.zeros((mask.shape[0], 1), F32)
    outs, bounds = [], [carry]
    for j in range(mask.shape[1] // PREFIX_CHUNK):
        chunk = ones[:, j * PREFIX_CHUNK:(j + 1) * PREFIX_CHUNK]
        outs.append(jnp.dot(chunk.astype(BF16), tri, preferred_element_type=F32) + carry)
        carry = carry + jnp.sum(chunk, axis=1, keepdims=True)
        bounds.append(carry)
    return jnp.concatenate(outs, axis=1), bounds


THRESHOLD_BITS = 31
THRESHOLD_REFINE = 16


def _route_kernel(logit_ref, slot_ref, aff_ref, starts_ref, *, cap):
    n_seq = logit_ref.shape[0]
    affs = []
    for b in range(n_seq):
        lg = logit_ref[b]
        ex = jnp.exp(lg - jnp.max(lg, axis=0, keepdims=True))
        affs.append(ex / jnp.sum(ex, axis=0, keepdims=True))
    aff = jnp.concatenate(affs, axis=0)

    def count_ge(t):
        return jnp.sum(jnp.where(aff >= t, 1.0, 0.0), axis=1, keepdims=True)

    thr = jnp.zeros((aff.shape[0], 1), jnp.int32)
    for bit in range(THRESHOLD_BITS - 1, -1, -1):
        cand = thr | (1 << bit)
        thr = jnp.where(count_ge(pltpu.bitcast(cand, F32)) >= cap, cand, thr)
    lo = pltpu.bitcast(thr, F32)
    hi = pltpu.bitcast(thr + 1, F32)
    for _ in range(THRESHOLD_REFINE):
        mid = 0.5 * (lo + hi)
        take = count_ge(mid) >= cap
        lo = jnp.where(take, mid, lo)
        hi = jnp.where(take, hi, mid)
    above = aff >= hi
    tied = (aff >= lo) & (aff < hi)
    r = lax.broadcasted_iota(jnp.int32, (PREFIX_CHUNK, PREFIX_CHUNK), 0)
    c = lax.broadcasted_iota(jnp.int32, (PREFIX_CHUNK, PREFIX_CHUNK), 1)
    tri = jnp.where(r < c, 1.0, 0.0).astype(BF16)
    need = cap - jnp.sum(jnp.where(above, 1.0, 0.0), axis=1, keepdims=True)
    sel = above | (tied & (_prefix_count(tied, tri)[0] < need))
    rank, bounds = _prefix_count(sel, tri)
    slot = jnp.where(sel, rank, -1.0).astype(jnp.int32)
    lane = lax.broadcasted_iota(jnp.int32, (1, LANES), 1)
    starts = jnp.zeros((aff.shape[0], LANES), F32)
    for j, count in enumerate(bounds):
        starts = jnp.where(lane == j, count, starts)
    for b in range(n_seq):
        rows = slice(b * N_EXPERTS, (b + 1) * N_EXPERTS)
        slot_ref[b] = slot[rows]
        aff_ref[b] = aff[rows]
        starts_ref[b] = starts[rows].astype(jnp.int32)


ROUTE_SEQS_PER_STEP = 4


def _route(logits, batch, seq, cap):
    nb = ROUTE_SEQS_PER_STEP if batch % ROUTE_SEQS_PER_STEP == 0 else 1
    blk = pl.BlockSpec((nb, N_EXPERTS, seq), lambda i: (i, 0, 0))
    return pl.pallas_call(
        functools.partial(_route_kernel, cap=cap),
        grid=(batch // nb,),
        in_specs=[blk],
        out_specs=[blk, blk, pl.BlockSpec((nb, N_EXPERTS, LANES), lambda i: (i, 0, 0))],
        out_shape=[jax.ShapeDtypeStruct((batch, N_EXPERTS, seq), jnp.int32),
                   jax.ShapeDtypeStruct((batch, N_EXPERTS, seq), F32),
                   jax.ShapeDtypeStruct((batch, N_EXPERTS, LANES), jnp.int32)],
        compiler_params=_params(1),
        name="route",
    )(logits)


BF16_SUBLANES = 16
DISPATCH_WIN = 64
MOE_CHUNKS_PER_STEP = 2


def _slot_windows(starts_ref, b, j, n_bounds, win):
    base, n_win = [], 0
    for e in range(N_EXPERTS):
        at = (b * N_EXPERTS + e) * n_bounds + j
        first = (starts_ref[at] // BF16_SUBLANES) * BF16_SUBLANES
        base.append(first)
        n_win = jnp.maximum(n_win, (starts_ref[at + 1] - first + win - 1) // win)
    return base, n_win


def _window_rows(base_e, k, wrow, cap, win):
    lower = base_e + k * win
    off = pl.multiple_of(jnp.minimum(lower, cap - win), BF16_SUBLANES)
    return off, jnp.where(wrow + off >= lower, wrow + off, -2)


def _dispatch_kernel(starts_ref, h_ref, slot_ref, aff_ref, xe_ref, gate_ref, *, cap):
    b, step = pl.program_id(0), pl.program_id(1)
    win = DISPATCH_WIN
    n_bounds = pl.num_programs(1) * MOE_CHUNKS_PER_STEP + 1

    @pl.when(step == 0)
    def _():
        xe_ref[...] = jnp.zeros_like(xe_ref)
        gate_ref[...] = jnp.zeros_like(gate_ref)

    wrow = lax.broadcasted_iota(jnp.int32, (win, PREFIX_CHUNK), 0)
    for c in range(MOE_CHUNKS_PER_STEP):
        tokens = slice(c * PREFIX_CHUNK, (c + 1) * PREFIX_CHUNK)
        base, n_win = _slot_windows(starts_ref, b, step * MOE_CHUNKS_PER_STEP + c, n_bounds, win)
        h = h_ref[0, tokens, :]

        def window(k):
            offs, onehots = [], []
            for e in range(N_EXPERTS):
                off, target = _window_rows(base[e], k, wrow, cap, win)
                offs.append(off)
                onehots.append(slot_ref[0, e:e + 1, tokens] == target)
            sel = jnp.concatenate([jnp.where(o, 1.0, 0.0) for o in onehots], axis=0).astype(BF16)
            rows = jnp.dot(sel, h, preferred_element_type=F32).astype(BF16)
            for e in range(N_EXPERTS):
                at = pl.ds(offs[e], win)
                xe_ref[e, at, :] += rows[e * win:(e + 1) * win]
                gate = jnp.sum(jnp.where(onehots[e], aff_ref[0, e:e + 1, tokens], 0.0), axis=1, keepdims=True)
                gate_ref[e, at, :] += jnp.broadcast_to(gate, (win, LANES))

        window(0)
        lax.fori_loop(1, n_win, lambda k, carry: (window(k), carry)[1], 0)


def _dispatch(h1b, slot, aff, starts, cap):
    b, s, _ = h1b.shape
    n_chunks = s // PREFIX_CHUNK
    span = MOE_CHUNKS_PER_STEP * PREFIX_CHUNK
    chunk = pl.BlockSpec((1, N_EXPERTS, span), lambda i, j, st: (i, 0, j))
    grid_spec = pltpu.PrefetchScalarGridSpec(
        num_scalar_prefetch=1,
        grid=(b, s // span),
        in_specs=[pl.BlockSpec((1, span, D_MODEL), lambda i, j, st: (i, j, 0)), chunk, chunk],
        out_specs=[pl.BlockSpec((N_EXPERTS, cap, D_MODEL), lambda i, j, st: (0, i, 0)),
                   pl.BlockSpec((N_EXPERTS, cap, LANES), lambda i, j, st: (0, i, 0))],
    )
    return pl.pallas_call(
        functools.partial(_dispatch_kernel, cap=cap),
        grid_spec=grid_spec,
        out_shape=[jax.ShapeDtypeStruct((N_EXPERTS, b * cap, D_MODEL), BF16),
                   jax.ShapeDtypeStruct((N_EXPERTS, b * cap, LANES), F32)],
        compiler_params=_params(2),
        name="dispatch",
    )(starts[:, :, :n_chunks + 1].reshape(-1), h1b, slot, aff)


FF_CHUNK = 512


def _expert_kernel(x_ref, gate_ref, wg_ref, wu_ref, wd_ref, y_ref):
    x = x_ref[0]
    acc = jnp.zeros((x.shape[0], D_MODEL), F32)
    for c in range(D_FF // FF_CHUNK):
        cols = slice(c * FF_CHUNK, (c + 1) * FF_CHUNK)
        g = jnp.dot(x, wg_ref[0, :, cols], preferred_element_type=F32)
        u = jnp.dot(x, wu_ref[0, :, cols], preferred_element_type=F32)
        act = (g * jax.nn.sigmoid(g) * u).astype(BF16)
        acc = acc + jnp.dot(act, wd_ref[0, cols, :], preferred_element_type=F32)
    y_ref[0] = (acc * gate_ref[0][:, :1]).astype(BF16)


def _expert_ffn(xe, gate, w_gate, w_up, w_down, tm):
    e, rows, _ = xe.shape
    tok = lambda w: pl.BlockSpec((1, tm, w), lambda i, j: (i, j, 0))
    wspec = lambda a: pl.BlockSpec((1,) + a.shape[1:], lambda i, j: (i, 0, 0))
    return pl.pallas_call(
        _expert_kernel,
        grid=(e, rows // tm),
        in_specs=[tok(D_MODEL), tok(LANES), wspec(w_gate), wspec(w_up), wspec(w_down)],
        out_specs=tok(D_MODEL),
        out_shape=jax.ShapeDtypeStruct((e, rows, D_MODEL), BF16),
        compiler_params=_params(2),
        name="expert_ffn",
    )(xe, gate, w_gate, w_up, w_down)


COMBINE_WIN = 64
TN_DIMS = (((0,), (0,)), ((), ()))


def _combine_kernel(starts_ref, h1_ref, slot_ref, y_ref, g2_ref, b2_ref, out_ref, acc_ref, *, cap, n_chunks):
    s = pl.program_id(0)
    cur = jnp.minimum(s, pl.num_programs(0) - 2)
    steps_per_seq = n_chunks // MOE_CHUNKS_PER_STEP
    win = COMBINE_WIN

    @pl.when(s == 0)
    def _():
        acc_ref[1] = jnp.zeros(acc_ref.shape[1:], F32)

    out_ref[0] = _layer_norm(DEEPNORM_ALPHA * h1_ref[0] + acc_ref[(s + 1) % 2], g2_ref[...], b2_ref[...])

    wrow = lax.broadcasted_iota(jnp.int32, (win, PREFIX_CHUNK), 0)
    for c in range(MOE_CHUNKS_PER_STEP):
        tokens = slice(c * PREFIX_CHUNK, (c + 1) * PREFIX_CHUNK)
        chunk = (cur % steps_per_seq) * MOE_CHUNKS_PER_STEP + c
        base, n_win = _slot_windows(starts_ref, cur // steps_per_seq, chunk, n_chunks + 1, win)

        def gather(k):
            sels, rows = [], []
            for e in range(N_EXPERTS):
                off, target = _window_rows(base[e], k, wrow, cap, win)
                sels.append(jnp.where(slot_ref[0, e:e + 1, tokens] == target, 1.0, 0.0))
                rows.append(y_ref[e, pl.ds(off, win), :])
            sel = jnp.concatenate(sels, axis=0).astype(BF16)
            return lax.dot_general(sel, jnp.concatenate(rows, axis=0), TN_DIMS, preferred_element_type=F32)

        acc_ref[s % 2, tokens, :] = gather(0)

        def more(k, carry):
            acc_ref[s % 2, tokens, :] += gather(k)
            return carry

        lax.fori_loop(1, n_win, more, 0)


def _combine(h1, slot, starts, y, ln2_g, ln2_b, cap):
    b, s, _ = h1.shape
    n_chunks = s // PREFIX_CHUNK
    span = MOE_CHUNKS_PER_STEP * PREFIX_CHUNK
    per_seq = s // span
    n_steps = b * per_seq
    cur = lambda i: jnp.minimum(i, n_steps - 1)
    prev = lambda i: jnp.maximum(i - 1, 0)
    const = lambda i, st: (0, 0)
    grid_spec = pltpu.PrefetchScalarGridSpec(
        num_scalar_prefetch=1,
        grid=(n_steps + 1,),
        in_specs=[pl.BlockSpec((1, span, D_MODEL), lambda i, st: (prev(i) // per_seq, prev(i) % per_seq, 0)),
                  pl.BlockSpec((1, N_EXPERTS, span), lambda i, st: (cur(i) // per_seq, 0, cur(i) % per_seq)),
                  pl.BlockSpec((N_EXPERTS, cap, D_MODEL), lambda i, st: (0, cur(i) // per_seq, 0)),
                  pl.BlockSpec((1, D_MODEL), const), pl.BlockSpec((1, D_MODEL), const)],
        out_specs=pl.BlockSpec((1, span, D_MODEL), lambda i, st: (prev(i) // per_seq, prev(i) % per_seq, 0)),
        scratch_shapes=[pltpu.VMEM((2, span, D_MODEL), F32)],
    )
    return pl.pallas_call(
        functools.partial(_combine_kernel, cap=cap, n_chunks=n_chunks),
        grid_spec=grid_spec,
        out_shape=jax.ShapeDtypeStruct((b, s, D_MODEL), F32),
        compiler_params=_params(1),
        name="combine_ln2",
    )(starts[:, :, :n_chunks + 1].reshape(-1), h1, slot, y, ln2_g.reshape(1, D_MODEL), ln2_b.reshape(1, D_MODEL))


def kernel(x, ln0_g, ln0_b, w_in, b_gate, qn_g, kn_g, w_branch_a, w_branch_b, w_out, ln1_g, ln1_b, w_router,
           w_gate_e, w_up_e, w_down_e, ln2_g, ln2_b):
    batch, seq, _ = x.shape
    assert w_in.shape[0] == DEPTH
    cap = EC_CAPACITY_FACTOR * seq // N_EXPERTS
    tokens = batch * seq
    x2 = x.reshape(tokens, D_MODEL)
    w_in_b = w_in[0].astype(BF16)

    qkv = _in_projection(x2, ln0_g, ln0_b, w_in_b[:, :QKV_W], qn_g[0], kn_g[0], seq, tm=512)
    qa, ka, va = qkv[:3]
    qb, kb, vb = qkv[3:6], qkv[6:9], qkv[9:12]

    oa, (wg_b, wu_b, wd_b) = _attention_a(qa.reshape(batch, seq, A_Q_W), ka.reshape(batch, seq, A_KV_W),
                                          va.reshape(batch, seq, A_KV_W), (w_gate_e[0], w_up_e[0], w_down_e[0]),
                                          tq=512)
    oa = oa.reshape(tokens, A_Q_W)

    obs, lses = [], []
    for gi, (window, dilation) in enumerate(B_GROUPS):
        assert window // (2 * dilation) == BAND_HALF
        o, lse = _dilated_attention(qb[gi], kb[gi], vb[gi], seq, dilation)
        obs.append(o)
        lses.append(lse)

    h1, h1b, logits = _merge(x2, ln0_g, ln0_b, w_in_b[:, QKV_W:], b_gate[0], oa, obs, lses,
                             w_branch_a[0].astype(BF16), w_branch_b[0].astype(BF16), w_out[0].astype(BF16),
                             ln1_g[0], ln1_b[0], w_router[0], seq, tm=512)

    slot, aff, starts = _route(logits, batch, seq, cap)
    xe, gate = _dispatch(h1b.reshape(batch, seq, D_MODEL), slot, aff, starts, cap)
    y = _expert_ffn(xe, gate, wg_b, wu_b, wd_b, tm=min(1024, batch * cap))
    return _combine(h1.reshape(batch, seq, D_MODEL), slot, starts, y, ln2_g[0], ln2_b[0], cap)
```

```python
import functools

import jax
import jax.numpy as jnp
import numpy as np
from jax import lax
from jax.experimental import pallas as pl
from jax.experimental.pallas import tpu as pltpu

F32 = jnp.float32
BF16 = jnp.bfloat16

D_MODEL = 1024
HEAD_DIM = 64
A_Q_HEADS = 8
A_KV_HEADS = 2
A_GROUP = A_Q_HEADS // A_KV_HEADS
B_GROUPS = ((128, 1), (512, 4), (2048, 16))
B_HEADS_PER_GROUP = 4
GRID_W = 64
A_ROPE_THETA = 10000.0
B_ROPE_THETA = 500000.0
B_ROPE_DIMS = HEAD_DIM // 4
N_EXPERTS = 16
EC_CAPACITY_FACTOR = 2
D_FF = 2 * D_MODEL
LN_EPS = 1e-5
QK_NORM_EPS = 1e-6
MASK_VALUE = -1e30
DEPTH = 1
DEEPNORM_ALPHA = (2.0 * DEPTH) ** 0.25
LOG2_E = 1.4426950408889634
Q_SCALE = HEAD_DIM ** -0.5 * LOG2_E

A_Q_W = A_Q_HEADS * HEAD_DIM
A_KV_W = A_KV_HEADS * HEAD_DIM
B_GROUP_W = B_HEADS_PER_GROUP * HEAD_DIM
B_W = B_GROUP_W * len(B_GROUPS)
QKV_W = A_Q_W + 2 * A_KV_W + 3 * B_W
BAND_HALF = 64

LANES = 128
VMEM_LIMIT = 56 * 1024 * 1024
ROW_TILE = 512
FFN_ROW_TILE = 1024

NT_DIMS = (((1,), (1,)), ((), ()))


def _params(n_axes):
    return pltpu.CompilerParams(dimension_semantics=("arbitrary",) * n_axes, vmem_limit_bytes=VMEM_LIMIT)


def _layer_norm(x, g, b):
    mu = jnp.mean(x, axis=-1, keepdims=True)
    xc = x - mu
    var = jnp.mean(xc * xc, axis=-1, keepdims=True)
    return xc * lax.rsqrt(var + LN_EPS) * g + b


def _sigmoid(x):
    return 0.5 * jnp.tanh(0.5 * x) + 0.5


def _split_bf16(x):
    hi = x.astype(BF16)
    lo = (x - hi.astype(F32)).astype(BF16)
    return hi, lo


def _swap_halves(y, lane, period, half):
    fwd = pltpu.roll(y, LANES - half, 1)
    bwd = pltpu.roll(y, half, 1)
    return jnp.where((lane % period) < half, fwd, bwd)


def _inproj_kernel(x_ref, g0_ref, b0_ref, w_ref, qng_ref, kng_ref, cosa_ref, sina_ref, cosb_ref, sinb_ref,
                   qa_ref, ka_ref, va_ref, qb0_ref, qb1_ref, qb2_ref, kb0_ref, kb1_ref, kb2_ref,
                   vb0_ref, vb1_ref, vb2_ref, h_ref):
    step = pl.program_id(0)

    @pl.when(step == 0)
    def _():
        h_ref[1] = jnp.zeros(h_ref.shape[1:], BF16)

    h = h_ref[(step + 1) % 2]
    lane = lax.broadcasted_iota(jnp.int32, (1, LANES), 1)
    r = lax.broadcasted_iota(jnp.int32, (LANES, LANES), 0)
    c = lax.broadcasted_iota(jnp.int32, (LANES, LANES), 1)
    head_sum = jnp.where((r // HEAD_DIM) == (c // HEAD_DIM), 1.0, 0.0).astype(BF16)
    cosa, sina = cosa_ref[...], sina_ref[...]
    cosb, sinb = cosb_ref[...], sinb_ref[...]

    def proj(off):
        y = jnp.dot(h, w_ref[:, off:off + 2 * LANES], preferred_element_type=F32)
        return y[:, :LANES], y[:, LANES:]

    def norm_rope_a(y, gain, scale):
        hi, lo = _split_bf16(y * y)
        ss = (jnp.dot(hi, head_sum, preferred_element_type=F32)
              + jnp.dot(lo, head_sum, preferred_element_type=F32))
        y = y * lax.rsqrt(ss * (1.0 / HEAD_DIM) + QK_NORM_EPS) * gain
        y = y * cosa + _swap_halves(y, lane, HEAD_DIM // 2, HEAD_DIM // 4) * sina
        return (y * scale).astype(BF16)

    def rope_b(y, scale):
        y = y * cosb + _swap_halves(y, lane, HEAD_DIM, B_ROPE_DIMS // 2) * sinb
        return y * scale

    def project_b(gi, q_ref, k_ref, v_ref):
        off = A_Q_W + 2 * A_KV_W + gi * B_GROUP_W
        for j, (q, k, v) in enumerate(zip(proj(off), proj(off + B_W), proj(off + 2 * B_W))):
            q_ref[j] = rope_b(q, Q_SCALE)
            k_ref[j] = rope_b(k, 1.0)
            v_ref[j] = v

    qa_tiles = [y for j in range(A_Q_W // (2 * LANES)) for y in proj(2 * j * LANES)]
    ka, va = proj(A_Q_W)
    project_b(0, qb0_ref, kb0_ref, vb0_ref)
    for j, y in enumerate(qa_tiles):
        qa_ref[:, j * LANES:(j + 1) * LANES] = norm_rope_a(y, qng_ref[...], Q_SCALE)
    ka_ref[...] = norm_rope_a(ka, kng_ref[...], 1.0)
    va_ref[...] = va.astype(BF16)
    project_b(1, qb1_ref, kb1_ref, vb1_ref)
    project_b(2, qb2_ref, kb2_ref, vb2_ref)

    h_ref[step % 2] = _layer_norm(x_ref[...], g0_ref[...], b0_ref[...]).astype(BF16)


def _rope_tables(seq):
    f32 = np.float32
    d = np.arange(LANES) % HEAD_DIM
    t = np.arange(seq)
    half_rot = HEAD_DIM // 2
    inv_a = f32(A_ROPE_THETA) ** (-np.arange(0, half_rot, 2, dtype=f32) / f32(half_rot))
    j = d % half_rot
    pos_row = (t // GRID_W).astype(f32)
    pos_col = (t % GRID_W).astype(f32)
    freq_a = inv_a[j % (half_rot // 2)].astype(f32)
    ang_a = np.where((d < half_rot)[None, :], pos_row[:, None], pos_col[:, None]) * freq_a[None, :]
    sign_a = np.where(j < half_rot // 2, -1.0, 1.0).astype(f32)
    cosa = np.cos(ang_a).astype(f32)
    sina = (np.sin(ang_a) * sign_a[None, :]).astype(f32)
    inv_b = f32(B_ROPE_THETA) ** (-np.arange(0, B_ROPE_DIMS, 2, dtype=f32) / f32(B_ROPE_DIMS))
    freq_b = inv_b[d % (B_ROPE_DIMS // 2)].astype(f32)
    ang_b = t.astype(f32)[:, None] * freq_b[None, :]
    rot = (d < B_ROPE_DIMS)[None, :]
    sign_b = np.where(d < B_ROPE_DIMS // 2, -1.0, 1.0).astype(f32)
    cosb = np.where(rot, np.cos(ang_b), 1.0).astype(f32)
    sinb = np.where(rot, np.sin(ang_b) * sign_b[None, :], 0.0).astype(f32)
    return tuple(jnp.asarray(a) for a in (cosa, sina, cosb, sinb))


def _in_projection(x2, ln0_g, ln0_b, w_qkv, qn_g, kn_g, seq, tm):
    m = x2.shape[0]
    cosa, sina, cosb, sinb = _rope_tables(seq)
    n_tiles = m // tm
    tiles_per_seq = seq // tm
    dst = lambda i: jnp.maximum(i - 1, 0)
    row = lambda i: (dst(i), 0)
    const = lambda i: (0, 0)
    table = lambda i: (dst(i) % tiles_per_seq, 0)
    widths = (A_Q_W, A_KV_W, A_KV_W) + (B_GROUP_W,) * 9
    gain_tile = lambda g: jnp.tile(g.reshape(1, HEAD_DIM), (1, LANES // HEAD_DIM))
    return pl.pallas_call(
        _inproj_kernel,
        grid=(n_tiles + 1,),
        in_specs=[
            pl.BlockSpec((tm, D_MODEL), lambda i: (jnp.minimum(i, n_tiles - 1), 0)),
            pl.BlockSpec((1, D_MODEL), const),
            pl.BlockSpec((1, D_MODEL), const),
            pl.BlockSpec((D_MODEL, QKV_W), const),
            pl.BlockSpec((1, LANES), const),
            pl.BlockSpec((1, LANES), const),
            pl.BlockSpec((tm, LANES), table),
            pl.BlockSpec((tm, LANES), table),
            pl.BlockSpec((tm, LANES), table),
            pl.BlockSpec((tm, LANES), table),
        ],
        out_specs=([pl.BlockSpec((tm, w), row) for w in widths[:3]]
                   + [pl.BlockSpec((B_GROUP_W // LANES, tm, LANES), lambda i: (0, dst(i), 0))] * 9),
        out_shape=([jax.ShapeDtypeStruct((m, w), BF16) for w in widths[:3]]
                   + [jax.ShapeDtypeStruct((B_GROUP_W // LANES, m, LANES), F32)] * 9),
        scratch_shapes=[pltpu.VMEM((2, tm, D_MODEL), BF16)],
        compiler_params=_params(1),
        name="in_projection",
    )(x2, ln0_g.reshape(1, D_MODEL), ln0_b.reshape(1, D_MODEL), w_qkv, gain_tile(qn_g), gain_tile(kn_g),
      cosa, sina, cosb, sinb)


def _attn_a_kernel(q_ref, k_ref, v_ref, wg_ref, wu_ref, wd_ref, o_ref, wgb_ref, wub_ref, wdb_ref):
    wgb_ref[...] = wg_ref[...].astype(BF16)
    wub_ref[...] = wu_ref[...].astype(BF16)
    wdb_ref[...] = wd_ref[...].astype(BF16)
    def scores(hq):
        kh = hq // A_GROUP
        q = q_ref[0, :, hq * HEAD_DIM:(hq + 1) * HEAD_DIM]
        k = k_ref[0, :, kh * HEAD_DIM:(kh + 1) * HEAD_DIM]
        return lax.dot_general(q, k, NT_DIMS, preferred_element_type=F32)

    outs = []
    v_ones = []
    for kh in range(A_KV_HEADS):
        v = v_ref[0, :, kh * HEAD_DIM:(kh + 1) * HEAD_DIM]
        v_ones.append(jnp.concatenate([v, jnp.ones_like(v)], axis=1))
    s = scores(0)
    for hq in range(A_Q_HEADS):
        s_next = scores(hq + 1) if hq + 1 < A_Q_HEADS else None
        kh = hq // A_GROUP
        p = jnp.exp2(s - jnp.max(s, axis=-1, keepdims=True)).astype(BF16)
        o = jnp.dot(p, v_ones[hq // A_GROUP], preferred_element_type=F32)
        outs.append(o[:, :HEAD_DIM] / o[:, HEAD_DIM:HEAD_DIM + 1])
        s = s_next
    o_ref[0] = jnp.concatenate(outs, axis=1).astype(BF16)


def _attention_a(qa, ka, va, expert_weights, tq):
    b, s, _ = qa.shape
    steps = b * (s // tq)
    flat = [w.reshape(-1, w.shape[-1]) for w in expert_weights]
    wslice = lambda w: pl.BlockSpec((w.shape[0] // steps, w.shape[1]), lambda i, j: (i * (s // tq) + j, 0))
    o, *rounded = pl.pallas_call(
        _attn_a_kernel,
        grid=(b, s // tq),
        in_specs=[
            pl.BlockSpec((1, tq, A_Q_W), lambda i, j: (i, j, 0)),
            pl.BlockSpec((1, s, A_KV_W), lambda i, j: (i, 0, 0)),
            pl.BlockSpec((1, s, A_KV_W), lambda i, j: (i, 0, 0)),
            *map(wslice, flat),
        ],
        out_specs=[pl.BlockSpec((1, tq, A_Q_W), lambda i, j: (i, j, 0)), *map(wslice, flat)],
        out_shape=[jax.ShapeDtypeStruct((b, s, A_Q_W), BF16),
                   *[jax.ShapeDtypeStruct(w.shape, BF16) for w in flat]],
        compiler_params=_params(2),
        name="attention_a",
    )(qa, ka, va, *flat)
    return o, [r.reshape(w.shape) for r, w in zip(rounded, expert_weights)]


BAND_QB = 128
BAND_UNITS_PER_STEP = 16
BAND_STEPS = 1


def _band_bias(n):
    win = min(2 * BAND_QB, n)
    rows = B_HEADS_PER_GROUP * BAND_QB
    rel = (np.arange(rows)[:, None] % BAND_QB) - np.arange(win)[None, :]
    shifts = np.arange(2 * BAND_QB // BAND_HALF - 1) * BAND_HALF
    inside = np.abs(rel[None] + shifts[:, None, None]) <= BAND_HALF
    return jnp.asarray(np.where(inside, 0.0, MASK_VALUE).astype(np.float32))


def _dilated_kernel(q_ref, k_ref, v_ref, bias_ref, o_ref, lse_ref, *, n, dilation):
    win = min(2 * BAND_QB, n)
    n_blk = n // BAND_QB
    lane_head = lax.broadcasted_iota(jnp.int32, (1, B_GROUP_W), 1) // HEAD_DIM

    planes = B_GROUP_W // LANES

    def class_rows(r, start, size):
        if dilation == 1:
            return pl.ds(pl.multiple_of(start, BAND_HALF), size)
        return pl.ds(r + start * dilation, size, stride=dilation)

    def load(ref, rows_):
        return jnp.concatenate([ref[j, rows_, :] for j in range(planes)], axis=1)

    def scores(i):
        unit = pl.program_id(1) * BAND_UNITS_PER_STEP + i
        r = unit // n_blk
        qs = (unit % n_blk) * BAND_QB
        ks = jnp.clip(qs - BAND_HALF, 0, n - win)
        q = load(q_ref, class_rows(r, qs, BAND_QB))
        k = load(k_ref, class_rows(r, ks, win)).astype(BF16)
        q_stack = jnp.concatenate([jnp.where(lane_head == h, q, 0.0) for h in range(B_HEADS_PER_GROUP)], axis=0)
        s = lax.dot_general(q_stack.astype(BF16), k, NT_DIMS, preferred_element_type=F32)
        return r, qs, ks, s + bias_ref[(qs - ks) // BAND_HALF]

    nxt = scores(0)
    for i in range(BAND_UNITS_PER_STEP):
        r, qs, ks, s = nxt
        if i + 1 < BAND_UNITS_PER_STEP:
            nxt = scores(i + 1)
        v = load(v_ref, class_rows(r, ks, win)).astype(BF16)
        mx = jnp.max(s, axis=-1, keepdims=True)
        p = jnp.exp2(s - mx)
        denom = jnp.sum(p, axis=-1, keepdims=True)
        pv = jnp.dot(p.astype(BF16), v, preferred_element_type=F32) / denom
        lse = mx + jnp.log2(denom)
        o = jnp.zeros((BAND_QB, B_GROUP_W), F32)
        l = jnp.zeros((BAND_QB, B_GROUP_W), F32)
        for h in range(B_HEADS_PER_GROUP):
            blk = slice(h * BAND_QB, (h + 1) * BAND_QB)
            o = jnp.where(lane_head == h, pv[blk], o)
            l = jnp.where(lane_head == h, lse[blk], l)
        for j in range(planes):
            o_ref[j, class_rows(r, qs, BAND_QB), :] = o[:, j * LANES:(j + 1) * LANES]
            lse_ref[j, class_rows(r, qs, BAND_QB), :] = l[:, j * LANES:(j + 1) * LANES]


def _dilated_attention(q, k, v, seq, dilation):
    planes, tokens, _ = q.shape
    n = seq // dilation
    assert dilation * (n // BAND_QB) == BAND_UNITS_PER_STEP * BAND_STEPS
    blk = pl.BlockSpec((planes, seq, LANES), lambda i, j: (0, i, 0))
    bias = _band_bias(n)
    return pl.pallas_call(
        functools.partial(_dilated_kernel, n=n, dilation=dilation),
        grid=(tokens // seq, BAND_STEPS),
        in_specs=[blk, blk, blk, pl.BlockSpec(bias.shape, lambda i, j: (0, 0, 0))],
        out_specs=[blk, blk],
        out_shape=[jax.ShapeDtypeStruct(q.shape, F32)] * 2,
        compiler_params=_params(2),
        name=f"dilated_attention_d{dilation}",
    )(q, k, v, bias)


def _merge_kernel(x_ref, g0_ref, b0_ref, wg_ref, bg_ref, oa_ref, o0_ref, o1_ref, o2_ref, l0_ref, l1_ref, l2_ref,
                  wa_ref, wb_ref, wo_ref, g1_ref, b1_ref, wr_hi_ref, wr_lo_ref,
                  h1_ref, h1b_ref, logit_ref, z_ref):
    i = pl.program_id(0)

    @pl.when(i == 0)
    def _():
        z_ref[1] = jnp.zeros(z_ref.shape[1:], F32)

    planes = lambda ref: jnp.concatenate([ref[j] for j in range(ref.shape[0])], axis=1)
    ya = jnp.dot(oa_ref[...], wa_ref[...], preferred_element_type=F32)
    h0 = _layer_norm(x_ref[...], g0_ref[...], b0_ref[...])
    gates = _sigmoid(jnp.dot(h0.astype(BF16), wg_ref[...], preferred_element_type=F32) + bg_ref[...])

    h1 = _layer_norm(z_ref[(i + 1) % 2], g1_ref[...], b1_ref[...])
    h1_ref[...] = h1
    h1b_ref[...] = h1.astype(BF16)
    hi, lo = _split_bf16(h1)
    logit_ref[0] = (lax.dot_general(wr_hi_ref[...], hi, NT_DIMS, preferred_element_type=F32)
                    + lax.dot_general(wr_hi_ref[...], lo, NT_DIMS, preferred_element_type=F32)
                    + lax.dot_general(wr_lo_ref[...], hi, NT_DIMS, preferred_element_type=F32))

    l0, l1, l2 = planes(l0_ref), planes(l1_ref), planes(l2_ref)
    mx = jnp.maximum(jnp.maximum(l0, l1), l2)
    e0, e1, e2 = jnp.exp2(l0 - mx), jnp.exp2(l1 - mx), jnp.exp2(l2 - mx)
    ob = (e0 * planes(o0_ref) + e1 * planes(o1_ref) + e2 * planes(o2_ref)) / (e0 + e1 + e2)
    yb = jnp.dot(ob.astype(BF16), wb_ref[...], preferred_element_type=F32)
    merged = gates[:, :D_MODEL] * ya + gates[:, D_MODEL:] * yb
    mix = jnp.dot(merged.astype(BF16), wo_ref[...], preferred_element_type=F32)
    z_ref[i % 2] = DEEPNORM_ALPHA * h0 + mix


def _merge(x2, ln0_g, ln0_b, w_gates, b_gate, oa, obs, lses, w_a, w_b, w_o, ln1_g, ln1_b, w_router, seq, tm):
    m = x2.shape[0]
    n_tiles = m // tm
    tiles_per_seq = seq // tm
    src = lambda i: jnp.minimum(i, n_tiles - 1)
    dst = lambda i: jnp.maximum(i - 1, 0)
    const = lambda i: (0, 0)
    wr_t = w_router.T
    wr_hi = wr_t.astype(BF16)
    wr_lo = (wr_t - wr_hi.astype(F32)).astype(BF16)
    vec = lambda v: v.reshape(1, -1)
    full = lambda a: pl.BlockSpec(a.shape, const)
    tile_in = lambda w: pl.BlockSpec((tm, w), lambda i: (src(i), 0))
    tile_out = lambda w: pl.BlockSpec((tm, w), lambda i: (dst(i), 0))
    plane_in = pl.BlockSpec((B_GROUP_W // LANES, tm, LANES), lambda i: (0, src(i), 0))
    args = [x2, vec(ln0_g), vec(ln0_b), w_gates, vec(b_gate), oa, *obs, *lses, w_a, w_b, w_o,
            vec(ln1_g), vec(ln1_b), wr_hi, wr_lo]
    in_specs = [tile_in(D_MODEL), full(args[1]), full(args[2]), full(w_gates), full(args[4]), tile_in(A_Q_W),
                *[plane_in] * 6, full(w_a), full(w_b), full(w_o), full(args[15]), full(args[16]),
                full(wr_hi), full(wr_lo)]
    return pl.pallas_call(
        _merge_kernel,
        grid=(n_tiles + 1,),
        in_specs=in_specs,
        out_specs=[tile_out(D_MODEL), tile_out(D_MODEL),
                   pl.BlockSpec((1, N_EXPERTS, tm),
                                lambda i: (dst(i) // tiles_per_seq, 0, dst(i) % tiles_per_seq))],
        out_shape=[jax.ShapeDtypeStruct((m, D_MODEL), F32), jax.ShapeDtypeStruct((m, D_MODEL), BF16),
                   jax.ShapeDtypeStruct((m // seq, N_EXPERTS, seq), F32)],
        scratch_shapes=[pltpu.VMEM((2, tm, D_MODEL), F32)],
        compiler_params=_params(1),
        name="merge_ln1_router",
    )(*args)


PREFIX_CHUNK = 256


def _prefix_count(mask, tri):
    ones = jnp.where(mask, 1.0, 0.0)
    carry = jnp.zeros((mask.shape[0], 1), F32)
    outs, bounds = [], [carry]
    for j in range(mask.shape[1] // PREFIX_CHUNK):
        chunk = ones[:, j * PREFIX_CHUNK:(j + 1) * PREFIX_CHUNK]
        outs.append(jnp.dot(chunk.astype(BF16), tri, preferred_element_type=F32) + carry)
        carry = carry + jnp.sum(chunk, axis=1, keepdims=True)
        bounds.append(carry)
    return jnp.concatenate(outs, axis=1), bounds


THRESHOLD_BITS = 31
THRESHOLD_REFINE = 16


def _route_kernel(logit_ref, slot_ref, aff_ref, starts_ref, *, cap):
    n_seq = logit_ref.shape[0]
    affs = []
    for b in range(n_seq):
        lg = logit_ref[b]
        ex = jnp.exp(lg - jnp.max(lg, axis=0, keepdims=True))
        affs.append(ex / jnp.sum(ex, axis=0, keepdims=True))
    aff = jnp.concatenate(affs, axis=0)

    def count_ge(t):
        return jnp.sum(jnp.where(aff >= t, 1.0, 0.0), axis=1, keepdims=True)

    thr = jnp.zeros((aff.shape[0], 1), jnp.int32)
    for bit in range(THRESHOLD_BITS - 1, -1, -1):
        cand = thr | (1 << bit)
        thr = jnp.where(count_ge(pltpu.bitcast(cand, F32)) >= cap, cand, thr)
    lo = pltpu.bitcast(thr, F32)
    hi = pltpu.bitcast(thr + 1, F32)
    for _ in range(THRESHOLD_REFINE):
        mid = 0.5 * (lo + hi)
        take = count_ge(mid) >= cap
        lo = jnp.where(take, mid, lo)
        hi = jnp.where(take, hi, mid)
    above = aff >= hi
    tied = (aff >= lo) & (aff < hi)
    r = lax.broadcasted_iota(jnp.int32, (PREFIX_CHUNK, PREFIX_CHUNK), 0)
    c = lax.broadcasted_iota(jnp.int32, (PREFIX_CHUNK, PREFIX_CHUNK), 1)
    tri = jnp.where(r < c, 1.0, 0.0).astype(BF16)
    need = cap - jnp.sum(jnp.where(above, 1.0, 0.0), axis=1, keepdims=True)
    sel = above | (tied & (_prefix_count(tied, tri)[0] < need))
    rank, bounds = _prefix_count(sel, tri)
    slot = jnp.where(sel, rank, -1.0).astype(jnp.int32)
    lane = lax.broadcasted_iota(jnp.int32, (1, LANES), 1)
    starts = jnp.zeros((aff.shape[0], LANES), F32)
    for j, count in enumerate(bounds):
        starts = jnp.where(lane == j, count, starts)
    for b in range(n_seq):
        rows = slice(b * N_EXPERTS, (b + 1) * N_EXPERTS)
        slot_ref[b] = slot[rows]
        aff_ref[b] = aff[rows]
        starts_ref[b] = starts[rows].astype(jnp.int32)


ROUTE_SEQS_PER_STEP = 4


def _route(logits, batch, seq, cap):
    nb = ROUTE_SEQS_PER_STEP if batch % ROUTE_SEQS_PER_STEP == 0 else 1
    blk = pl.BlockSpec((nb, N_EXPERTS, seq), lambda i: (i, 0, 0))
    return pl.pallas_call(
        functools.partial(_route_kernel, cap=cap),
        grid=(batch // nb,),
        in_specs=[blk],
        out_specs=[blk, blk, pl.BlockSpec((nb, N_EXPERTS, LANES), lambda i: (i, 0, 0))],
        out_shape=[jax.ShapeDtypeStruct((batch, N_EXPERTS, seq), jnp.int32),
                   jax.ShapeDtypeStruct((batch, N_EXPERTS, seq), F32),
                   jax.ShapeDtypeStruct((batch, N_EXPERTS, LANES), jnp.int32)],
        compiler_params=_params(1),
        name="route",
    )(logits)


BF16_SUBLANES = 16
DISPATCH_WIN = 64
MOE_CHUNKS_PER_STEP = 4


def _slot_windows(starts_ref, b, j, n_bounds, win):
    base, n_win = [], 0
    for e in range(N_EXPERTS):
        at = (b * N_EXPERTS + e) * n_bounds + j
        first = (starts_ref[at] // BF16_SUBLANES) * BF16_SUBLANES
        base.append(first)
        n_win = jnp.maximum(n_win, (starts_ref[at + 1] - first + win - 1) // win)
    return base, n_win


def _window_rows(base_e, k, wrow, cap, win):
    lower = base_e + k * win
    off = pl.multiple_of(jnp.minimum(lower, cap - win), BF16_SUBLANES)
    return off, jnp.where(wrow + off >= lower, wrow + off, -2)


def _dispatch_kernel(starts_ref, h_ref, slot_ref, aff_ref, xe_ref, gate_ref, *, cap):
    b, step = pl.program_id(0), pl.program_id(1)
    win = DISPATCH_WIN
    n_bounds = pl.num_programs(1) * MOE_CHUNKS_PER_STEP + 1

    @pl.when(step == 0)
    def _():
        xe_ref[...] = jnp.zeros_like(xe_ref)
        gate_ref[...] = jnp.zeros_like(gate_ref)

    wrow = lax.broadcasted_iota(jnp.int32, (win, PREFIX_CHUNK), 0)
    for c in range(MOE_CHUNKS_PER_STEP):
        tokens = slice(c * PREFIX_CHUNK, (c + 1) * PREFIX_CHUNK)
        base, n_win = _slot_windows(starts_ref, b, step * MOE_CHUNKS_PER_STEP + c, n_bounds, win)
        h = h_ref[0, tokens, :]

        def window(k):
            offs, onehots = [], []
            for e in range(N_EXPERTS):
                off, target = _window_rows(base[e], k, wrow, cap, win)
                offs.append(off)
                onehots.append(slot_ref[0, e:e + 1, tokens] == target)
            sel = jnp.concatenate([jnp.where(o, 1.0, 0.0) for o in onehots], axis=0).astype(BF16)
            rows = jnp.dot(sel, h, preferred_element_type=F32).astype(BF16)
            for e in range(N_EXPERTS):
                at = pl.ds(offs[e], win)
                xe_ref[e, at, :] += rows[e * win:(e + 1) * win]
                gate = jnp.sum(jnp.where(onehots[e], aff_ref[0, e:e + 1, tokens], 0.0), axis=1, keepdims=True)
                gate_ref[e, at, :] += jnp.broadcast_to(gate, (win, LANES))

        window(0)
        lax.fori_loop(1, n_win, lambda k, carry: (window(k), carry)[1], 0)


def _dispatch(h1b, slot, aff, starts, cap):
    b, s, _ = h1b.shape
    n_chunks = s // PREFIX_CHUNK
    span = MOE_CHUNKS_PER_STEP * PREFIX_CHUNK
    chunk = pl.BlockSpec((1, N_EXPERTS, span), lambda i, j, st: (i, 0, j))
    grid_spec = pltpu.PrefetchScalarGridSpec(
        num_scalar_prefetch=1,
        grid=(b, s // span),
        in_specs=[pl.BlockSpec((1, span, D_MODEL), lambda i, j, st: (i, j, 0)), chunk, chunk],
        out_specs=[pl.BlockSpec((N_EXPERTS, cap, D_MODEL), lambda i, j, st: (0, i, 0)),
                   pl.BlockSpec((N_EXPERTS, cap, LANES), lambda i, j, st: (0, i, 0))],
    )
    return pl.pallas_call(
        functools.partial(_dispatch_kernel, cap=cap),
        grid_spec=grid_spec,
        out_shape=[jax.ShapeDtypeStruct((N_EXPERTS, b * cap, D_MODEL), BF16),
                   jax.ShapeDtypeStruct((N_EXPERTS, b * cap, LANES), F32)],
        compiler_params=_params(2),
        name="dispatch",
    )(starts[:, :, :n_chunks + 1].reshape(-1), h1b, slot, aff)


FF_CHUNK = 512


def _expert_kernel(x_ref, gate_ref, wg_ref, wu_ref, wd_ref, y_ref):
    x = x_ref[0]
    acc = jnp.zeros((x.shape[0], D_MODEL), F32)
    for c in range(D_FF // FF_CHUNK):
        cols = slice(c * FF_CHUNK, (c + 1) * FF_CHUNK)
        g = jnp.dot(x, wg_ref[0, :, cols], preferred_element_type=F32)
        u = jnp.dot(x, wu_ref[0, :, cols], preferred_element_type=F32)
        act = (g * jax.nn.sigmoid(g) * u).astype(BF16)
        acc = acc + jnp.dot(act, wd_ref[0, cols, :], preferred_element_type=F32)
    y_ref[0] = (acc * gate_ref[0][:, :1]).astype(BF16)


def _expert_ffn(xe, gate, w_gate, w_up, w_down, tm):
    e, rows, _ = xe.shape
    tok = lambda w: pl.BlockSpec((1, tm, w), lambda i, j: (i, j, 0))
    wspec = lambda a: pl.BlockSpec((1,) + a.shape[1:], lambda i, j: (i, 0, 0))
    return pl.pallas_call(
        _expert_kernel,
        grid=(e, rows // tm),
        in_specs=[tok(D_MODEL), tok(LANES), wspec(w_gate), wspec(w_up), wspec(w_down)],
        out_specs=tok(D_MODEL),
        out_shape=jax.ShapeDtypeStruct((e, rows, D_MODEL), BF16),
        compiler_params=_params(2),
        name="expert_ffn",
    )(xe, gate, w_gate, w_up, w_down)


COMBINE_WIN = 64
TN_DIMS = (((0,), (0,)), ((), ()))


def _combine_kernel(starts_ref, h1_ref, slot_ref, y_ref, g2_ref, b2_ref, out_ref, acc_ref, *, cap, n_chunks):
    s = pl.program_id(0)
    cur = jnp.minimum(s, pl.num_programs(0) - 2)
    steps_per_seq = n_chunks // MOE_CHUNKS_PER_STEP
    win = COMBINE_WIN

    @pl.when(s == 0)
    def _():
        acc_ref[1] = jnp.zeros(acc_ref.shape[1:], F32)

    out_ref[0] = _layer_norm(DEEPNORM_ALPHA * h1_ref[0] + acc_ref[(s + 1) % 2], g2_ref[...], b2_ref[...])

    wrow = lax.broadcasted_iota(jnp.int32, (win, PREFIX_CHUNK), 0)
    for c in range(MOE_CHUNKS_PER_STEP):
        tokens = slice(c * PREFIX_CHUNK, (c + 1) * PREFIX_CHUNK)
        chunk = (cur % steps_per_seq) * MOE_CHUNKS_PER_STEP + c
        base, n_win = _slot_windows(starts_ref, cur // steps_per_seq, chunk, n_chunks + 1, win)

        def gather(k):
            sels, rows = [], []
            for e in range(N_EXPERTS):
                off, target = _window_rows(base[e], k, wrow, cap, win)
                sels.append(jnp.where(slot_ref[0, e:e + 1, tokens] == target, 1.0, 0.0))
                rows.append(y_ref[e, pl.ds(off, win), :])
            sel = jnp.concatenate(sels, axis=0).astype(BF16)
            return lax.dot_general(sel, jnp.concatenate(rows, axis=0), TN_DIMS, preferred_element_type=F32)

        acc_ref[s % 2, tokens, :] = gather(0)

        def more(k, carry):
            acc_ref[s % 2, tokens, :] += gather(k)
            return carry

        lax.fori_loop(1, n_win, more, 0)


def _combine(h1, slot, starts, y, ln2_g, ln2_b, cap):
    b, s, _ = h1.shape
    n_chunks = s // PREFIX_CHUNK
    span = MOE_CHUNKS_PER_STEP * PREFIX_CHUNK
    per_seq = s // span
    n_steps = b * per_seq
    cur = lambda i: jnp.minimum(i, n_steps - 1)
    prev = lambda i: jnp.maximum(i - 1, 0)
    const = lambda i, st: (0, 0)
    grid_spec = pltpu.PrefetchScalarGridSpec(
        num_scalar_prefetch=1,
        grid=(n_steps + 1,),
        in_specs=[pl.BlockSpec((1, span, D_MODEL), lambda i, st: (prev(i) // per_seq, prev(i) % per_seq, 0)),
                  pl.BlockSpec((1, N_EXPERTS, span), lambda i, st: (cur(i) // per_seq, 0, cur(i) % per_seq)),
                  pl.BlockSpec((N_EXPERTS, cap, D_MODEL), lambda i, st: (0, cur(i) // per_seq, 0)),
                  pl.BlockSpec((1, D_MODEL), const), pl.BlockSpec((1, D_MODEL), const)],
        out_specs=pl.BlockSpec((1, span, D_MODEL), lambda i, st: (prev(i) // per_seq, prev(i) % per_seq, 0)),
        scratch_shapes=[pltpu.VMEM((2, span, D_MODEL), F32)],
    )
    return pl.pallas_call(
        functools.partial(_combine_kernel, cap=cap, n_chunks=n_chunks),
        grid_spec=grid_spec,
        out_shape=jax.ShapeDtypeStruct((b, s, D_MODEL), F32),
        compiler_params=_params(1),
        name="combine_ln2",
    )(starts[:, :, :n_chunks + 1].reshape(-1), h1, slot, y, ln2_g.reshape(1, D_MODEL), ln2_b.reshape(1, D_MODEL))


def kernel(x, ln0_g, ln0_b, w_in, b_gate, qn_g, kn_g, w_branch_a, w_branch_b, w_out, ln1_g, ln1_b, w_router,
           w_gate_e, w_up_e, w_down_e, ln2_g, ln2_b):
    batch, seq, _ = x.shape
    assert w_in.shape[0] == DEPTH
    cap = EC_CAPACITY_FACTOR * seq // N_EXPERTS
    tokens = batch * seq
    x2 = x.reshape(tokens, D_MODEL)
    w_qkv = w_in[0, :, :QKV_W].astype(BF16)
    w_gates = w_in[0, :, QKV_W:].astype(BF16)

    qkv = _in_projection(x2, ln0_g, ln0_b, w_qkv, qn_g[0], kn_g[0], seq, tm=ROW_TILE)
    qa, ka, va = qkv[:3]
    qb, kb, vb = qkv[3:6], qkv[6:9], qkv[9:12]

    oa, (wg_b, wu_b, wd_b) = _attention_a(qa.reshape(batch, seq, A_Q_W), ka.reshape(batch, seq, A_KV_W),
                                          va.reshape(batch, seq, A_KV_W), (w_gate_e[0], w_up_e[0], w_down_e[0]),
                                          tq=ROW_TILE)
    oa = oa.reshape(tokens, A_Q_W)

    obs, lses = [], []
    for gi, (window, dilation) in enumerate(B_GROUPS):
        assert window // (2 * dilation) == BAND_HALF
        o, lse = _dilated_attention(qb[gi], kb[gi], vb[gi], seq, dilation)
        obs.append(o)
        lses.append(lse)

    h1, h1b, logits = _merge(x2, ln0_g, ln0_b, w_gates, b_gate[0], oa, obs, lses,
                             w_branch_a[0].astype(BF16), w_branch_b[0].astype(BF16), w_out[0].astype(BF16),
                             ln1_g[0], ln1_b[0], w_router[0], seq, tm=ROW_TILE)

    slot, aff, starts = _route(logits, batch, seq, cap)
    xe, gate = _dispatch(h1b.reshape(batch, seq, D_MODEL), slot, aff, starts, cap)
    y = _expert_ffn(xe, gate, wg_b, wu_b, wd_b, tm=min(FFN_ROW_TILE, batch * cap))
    return _combine(h1.reshape(batch, seq, D_MODEL), slot, starts, y, ln2_g[0], ln2_b[0], cap)
```

```python
import functools

import jax
import jax.numpy as jnp
import numpy as np
from jax import lax
from jax.experimental import pallas as pl
from jax.experimental.pallas import tpu as pltpu

F32 = jnp.float32
BF16 = jnp.bfloat16

D_MODEL = 1024
HEAD_DIM = 64
A_Q_HEADS = 8
A_KV_HEADS = 2
A_GROUP = A_Q_HEADS // A_KV_HEADS
B_GROUPS = ((128, 1), (512, 4), (2048, 16))
B_HEADS_PER_GROUP = 4
GRID_W = 64
A_ROPE_THETA = 10000.0
B_ROPE_THETA = 500000.0
B_ROPE_DIMS = HEAD_DIM // 4
N_EXPERTS = 16
EC_CAPACITY_FACTOR = 2
D_FF = 2 * D_MODEL
LN_EPS = 1e-5
QK_NORM_EPS = 1e-6
MASK_VALUE = -1e30
DEPTH = 1
DEEPNORM_ALPHA = (2.0 * DEPTH) ** 0.25
LOG2_E = 1.4426950408889634
Q_SCALE = HEAD_DIM ** -0.5 * LOG2_E

A_Q_W = A_Q_HEADS * HEAD_DIM
A_KV_W = A_KV_HEADS * HEAD_DIM
B_GROUP_W = B_HEADS_PER_GROUP * HEAD_DIM
B_W = B_GROUP_W * len(B_GROUPS)
QKV_W = A_Q_W + 2 * A_KV_W + 3 * B_W
BAND_HALF = 64

LANES = 128
VMEM_LIMIT = 56 * 1024 * 1024
ROW_TILE = 512
IN_PROJ_ROW_TILE = 1024
FFN_ROW_TILE = 1024

NT_DIMS = (((1,), (1,)), ((), ()))


def _params(n_axes):
    return pltpu.CompilerParams(dimension_semantics=("arbitrary",) * n_axes, vmem_limit_bytes=VMEM_LIMIT)


def _layer_norm(x, g, b):
    mu = jnp.mean(x, axis=-1, keepdims=True)
    xc = x - mu
    var = jnp.mean(xc * xc, axis=-1, keepdims=True)
    return xc * lax.rsqrt(var + LN_EPS) * g + b


def _split_bf16(x):
    hi = x.astype(BF16)
    lo = (x - hi.astype(F32)).astype(BF16)
    return hi, lo


def _swap_halves(y, lane, period, half):
    fwd = pltpu.roll(y, LANES - half, 1)
    bwd = pltpu.roll(y, half, 1)
    return jnp.where((lane % period) < half, fwd, bwd)


def _inproj_kernel(x_ref, g0_ref, b0_ref, w_ref, qng_ref, kng_ref, cosa_ref, sina_ref, cosb_ref, sinb_ref,
                   qa_ref, ka_ref, va_ref, qb0_ref, qb1_ref, qb2_ref, kb0_ref, kb1_ref, kb2_ref,
                   vb0_ref, vb1_ref, vb2_ref, h_ref):
    step = pl.program_id(0)

    @pl.when(step == 0)
    def _():
        h_ref[1] = jnp.zeros(h_ref.shape[1:], BF16)

    h = h_ref[(step + 1) % 2]
    lane = lax.broadcasted_iota(jnp.int32, (1, LANES), 1)
    r = lax.broadcasted_iota(jnp.int32, (LANES, LANES), 0)
    c = lax.broadcasted_iota(jnp.int32, (LANES, LANES), 1)
    head_sum = jnp.where((r // HEAD_DIM) == (c // HEAD_DIM), 1.0, 0.0).astype(BF16)
    cosa, sina = cosa_ref[...], sina_ref[...]
    cosb, sinb = cosb_ref[...], sinb_ref[...]

    def proj(off):
        y = jnp.dot(h, w_ref[:, off:off + 2 * LANES], preferred_element_type=F32)
        return y[:, :LANES], y[:, LANES:]

    def norm_rope_a(y, gain, scale):
        hi, lo = _split_bf16(y * y)
        ss = (jnp.dot(hi, head_sum, preferred_element_type=F32)
              + jnp.dot(lo, head_sum, preferred_element_type=F32))
        y = y * lax.rsqrt(ss * (1.0 / HEAD_DIM) + QK_NORM_EPS) * gain
        y = y * cosa + _swap_halves(y, lane, HEAD_DIM // 2, HEAD_DIM // 4) * sina
        return (y * scale).astype(BF16)

    def rope_b(y, scale):
        y = y * cosb + _swap_halves(y, lane, HEAD_DIM, B_ROPE_DIMS // 2) * sinb
        return y * scale

    def project_b(gi, q_ref, k_ref, v_ref):
        off = A_Q_W + 2 * A_KV_W + gi * B_GROUP_W
        for j, (q, k, v) in enumerate(zip(proj(off), proj(off + B_W), proj(off + 2 * B_W))):
            q_ref[j] = rope_b(q, Q_SCALE)
            k_ref[j] = rope_b(k, 1.0)
            v_ref[j] = v

    qa_tiles = [y for j in range(A_Q_W // (2 * LANES)) for y in proj(2 * j * LANES)]
    ka, va = proj(A_Q_W)
    project_b(0, qb0_ref, kb0_ref, vb0_ref)
    for j, y in enumerate(qa_tiles):
        qa_ref[:, j * LANES:(j + 1) * LANES] = norm_rope_a(y, qng_ref[...], Q_SCALE)
    ka_ref[...] = norm_rope_a(ka, kng_ref[...], 1.0)
    va_ref[...] = va.astype(BF16)
    project_b(1, qb1_ref, kb1_ref, vb1_ref)
    project_b(2, qb2_ref, kb2_ref, vb2_ref)

    h_ref[step % 2] = _layer_norm(x_ref[...], g0_ref[...], b0_ref[...]).astype(BF16)


def _rope_tables(seq):
    f32 = np.float32
    d = np.arange(LANES) % HEAD_DIM
    t = np.arange(seq)
    half_rot = HEAD_DIM // 2
    inv_a = f32(A_ROPE_THETA) ** (-np.arange(0, half_rot, 2, dtype=f32) / f32(half_rot))
    j = d % half_rot
    pos_row = (t // GRID_W).astype(f32)
    pos_col = (t % GRID_W).astype(f32)
    freq_a = inv_a[j % (half_rot // 2)].astype(f32)
    ang_a = np.where((d < half_rot)[None, :], pos_row[:, None], pos_col[:, None]) * freq_a[None, :]
    sign_a = np.where(j < half_rot // 2, -1.0, 1.0).astype(f32)
    cosa = np.cos(ang_a).astype(f32)
    sina = (np.sin(ang_a) * sign_a[None, :]).astype(f32)
    inv_b = f32(B_ROPE_THETA) ** (-np.arange(0, B_ROPE_DIMS, 2, dtype=f32) / f32(B_ROPE_DIMS))
    freq_b = inv_b[d % (B_ROPE_DIMS // 2)].astype(f32)
    ang_b = t.astype(f32)[:, None] * freq_b[None, :]
    rot = (d < B_ROPE_DIMS)[None, :]
    sign_b = np.where(d < B_ROPE_DIMS // 2, -1.0, 1.0).astype(f32)
    cosb = np.where(rot, np.cos(ang_b), 1.0).astype(f32)
    sinb = np.where(rot, np.sin(ang_b) * sign_b[None, :], 0.0).astype(f32)
    return tuple(jnp.asarray(a) for a in (cosa, sina, cosb, sinb))


def _in_projection(x2, ln0_g, ln0_b, w_qkv, qn_g, kn_g, seq, tm):
    m = x2.shape[0]
    cosa, sina, cosb, sinb = _rope_tables(seq)
    n_tiles = m // tm
    tiles_per_seq = seq // tm
    dst = lambda i: jnp.maximum(i - 1, 0)
    row = lambda i: (dst(i), 0)
    const = lambda i: (0, 0)
    table = lambda i: (dst(i) % tiles_per_seq, 0)
    widths = (A_Q_W, A_KV_W, A_KV_W) + (B_GROUP_W,) * 9
    gain_tile = lambda g: jnp.tile(g.reshape(1, HEAD_DIM), (1, LANES // HEAD_DIM))
    return pl.pallas_call(
        _inproj_kernel,
        grid=(n_tiles + 1,),
        in_specs=[
            pl.BlockSpec((tm, D_MODEL), lambda i: (jnp.minimum(i, n_tiles - 1), 0)),
            pl.BlockSpec((1, D_MODEL), const),
            pl.BlockSpec((1, D_MODEL), const),
            pl.BlockSpec((D_MODEL, QKV_W), const),
            pl.BlockSpec((1, LANES), const),
            pl.BlockSpec((1, LANES), const),
            pl.BlockSpec((tm, LANES), table),
            pl.BlockSpec((tm, LANES), table),
            pl.BlockSpec((tm, LANES), table),
            pl.BlockSpec((tm, LANES), table),
        ],
        out_specs=([pl.BlockSpec((tm, w), row) for w in widths[:3]]
                   + [pl.BlockSpec((B_GROUP_W // LANES, tm, LANES), lambda i: (0, dst(i), 0))] * 9),
        out_shape=([jax.ShapeDtypeStruct((m, w), BF16) for w in widths[:3]]
                   + [jax.ShapeDtypeStruct((B_GROUP_W // LANES, m, LANES), F32)] * 9),
        scratch_shapes=[pltpu.VMEM((2, tm, D_MODEL), BF16)],
        compiler_params=_params(1),
        name="in_projection",
    )(x2, ln0_g.reshape(1, D_MODEL), ln0_b.reshape(1, D_MODEL), w_qkv, gain_tile(qn_g), gain_tile(kn_g),
      cosa, sina, cosb, sinb)


def _attn_a_kernel(q_ref, k_ref, v_ref, wg_ref, wu_ref, wd_ref, o_ref, wgb_ref, wub_ref, wdb_ref):
    wgb_ref[...] = wg_ref[...].astype(BF16)
    wub_ref[...] = wu_ref[...].astype(BF16)
    wdb_ref[...] = wd_ref[...].astype(BF16)
    def scores(hq):
        kh = hq // A_GROUP
        q = q_ref[0, :, hq * HEAD_DIM:(hq + 1) * HEAD_DIM]
        k = k_ref[0, :, kh * HEAD_DIM:(kh + 1) * HEAD_DIM]
        return lax.dot_general(q, k, NT_DIMS, preferred_element_type=F32)

    outs = []
    v_ones = []
    for kh in range(A_KV_HEADS):
        v = v_ref[0, :, kh * HEAD_DIM:(kh + 1) * HEAD_DIM]
        v_ones.append(jnp.concatenate([v, jnp.ones_like(v)], axis=1))
    s = scores(0)
    for hq in range(A_Q_HEADS):
        s_next = scores(hq + 1) if hq + 1 < A_Q_HEADS else None
        kh = hq // A_GROUP
        p = jnp.exp2(s - jnp.max(s, axis=-1, keepdims=True)).astype(BF16)
        o = jnp.dot(p, v_ones[hq // A_GROUP], preferred_element_type=F32)
        outs.append(o[:, :HEAD_DIM] / o[:, HEAD_DIM:HEAD_DIM + 1])
        s = s_next
    o_ref[0] = jnp.concatenate(outs, axis=1).astype(BF16)


def _attention_a(qa, ka, va, expert_weights, tq):
    b, s, _ = qa.shape
    steps = b * (s // tq)
    flat = [w.reshape(-1, w.shape[-1]) for w in expert_weights]
    wslice = lambda w: pl.BlockSpec((w.shape[0] // steps, w.shape[1]), lambda i, j: (i * (s // tq) + j, 0))
    o, *rounded = pl.pallas_call(
        _attn_a_kernel,
        grid=(b, s // tq),
        in_specs=[
            pl.BlockSpec((1, tq, A_Q_W), lambda i, j: (i, j, 0)),
            pl.BlockSpec((1, s, A_KV_W), lambda i, j: (i, 0, 0)),
            pl.BlockSpec((1, s, A_KV_W), lambda i, j: (i, 0, 0)),
            *map(wslice, flat),
        ],
        out_specs=[pl.BlockSpec((1, tq, A_Q_W), lambda i, j: (i, j, 0)), *map(wslice, flat)],
        out_shape=[jax.ShapeDtypeStruct((b, s, A_Q_W), BF16),
                   *[jax.ShapeDtypeStruct(w.shape, BF16) for w in flat]],
        compiler_params=_params(2),
        name="attention_a",
    )(qa, ka, va, *flat)
    return o, [r.reshape(w.shape) for r, w in zip(rounded, expert_weights)]


BAND_QB = 128
BAND_UNITS_PER_STEP = 16
BAND_STEPS = 1


def _band_bias(n):
    win = min(2 * BAND_QB, n)
    rows = B_HEADS_PER_GROUP * BAND_QB
    rel = (np.arange(rows)[:, None] % BAND_QB) - np.arange(win)[None, :]
    shifts = np.arange(2 * BAND_QB // BAND_HALF - 1) * BAND_HALF
    inside = np.abs(rel[None] + shifts[:, None, None]) <= BAND_HALF
    return jnp.asarray(np.where(inside, 0.0, MASK_VALUE).astype(np.float32))


def _dilated_kernel(q_ref, k_ref, v_ref, bias_ref, o_ref, lse_ref, *, n, dilation):
    win = min(2 * BAND_QB, n)
    n_blk = n // BAND_QB
    lane_head = lax.broadcasted_iota(jnp.int32, (1, B_GROUP_W), 1) // HEAD_DIM

    planes = B_GROUP_W // LANES

    def class_rows(r, start, size):
        if dilation == 1:
            return pl.ds(pl.multiple_of(start, BAND_HALF), size)
        return pl.ds(r + start * dilation, size, stride=dilation)

    def load(ref, rows_):
        return jnp.concatenate([ref[j, rows_, :] for j in range(planes)], axis=1)

    def scores(i):
        unit = pl.program_id(1) * BAND_UNITS_PER_STEP + i
        r = unit // n_blk
        qs = (unit % n_blk) * BAND_QB
        ks = jnp.clip(qs - BAND_HALF, 0, n - win)
        q = load(q_ref, class_rows(r, qs, BAND_QB))
        k = load(k_ref, class_rows(r, ks, win)).astype(BF16)
        q_stack = jnp.concatenate([jnp.where(lane_head == h, q, 0.0) for h in range(B_HEADS_PER_GROUP)], axis=0)
        s = lax.dot_general(q_stack.astype(BF16), k, NT_DIMS, preferred_element_type=F32)
        return r, qs, ks, s + bias_ref[(qs - ks) // BAND_HALF]

    nxt = scores(0)
    for i in range(BAND_UNITS_PER_STEP):
        r, qs, ks, s = nxt
        if i + 1 < BAND_UNITS_PER_STEP:
            nxt = scores(i + 1)
        v = load(v_ref, class_rows(r, ks, win)).astype(BF16)
        mx = jnp.max(s, axis=-1, keepdims=True)
        p = jnp.exp2(s - mx)
        denom = jnp.sum(p, axis=-1, keepdims=True)
        pv = jnp.dot(p.astype(BF16), v, preferred_element_type=F32) / denom
        lse = mx + jnp.log2(denom)
        o = jnp.zeros((BAND_QB, B_GROUP_W), F32)
        l = jnp.zeros((BAND_QB, B_GROUP_W), F32)
        for h in range(B_HEADS_PER_GROUP):
            blk = slice(h * BAND_QB, (h + 1) * BAND_QB)
            o = jnp.where(lane_head == h, pv[blk], o)
            l = jnp.where(lane_head == h, lse[blk], l)
        for j in range(planes):
            o_ref[j, class_rows(r, qs, BAND_QB), :] = o[:, j * LANES:(j + 1) * LANES]
            lse_ref[j, class_rows(r, qs, BAND_QB), :] = l[:, j * LANES:(j + 1) * LANES]


def _dilated_attention(q, k, v, seq, dilation):
    planes, tokens, _ = q.shape
    n = seq // dilation
    assert dilation * (n // BAND_QB) == BAND_UNITS_PER_STEP * BAND_STEPS
    blk = pl.BlockSpec((planes, seq, LANES), lambda i, j: (0, i, 0))
    bias = _band_bias(n)
    return pl.pallas_call(
        functools.partial(_dilated_kernel, n=n, dilation=dilation),
        grid=(tokens // seq, BAND_STEPS),
        in_specs=[blk, blk, blk, pl.BlockSpec(bias.shape, lambda i, j: (0, 0, 0))],
        out_specs=[blk, blk],
        out_shape=[jax.ShapeDtypeStruct(q.shape, F32)] * 2,
        compiler_params=_params(2),
        name=f"dilated_attention_d{dilation}",
    )(q, k, v, bias)


def _merge_kernel(x_ref, g0_ref, b0_ref, wg_ref, bg_ref, oa_ref, o0_ref, o1_ref, o2_ref, l0_ref, l1_ref, l2_ref,
                  wa_ref, wb_ref, wo_ref, g1_ref, b1_ref, wr_hi_ref, wr_lo_ref,
                  h1_ref, h1b_ref, logit_ref, z_ref):
    i = pl.program_id(0)

    @pl.when(i == 0)
    def _():
        z_ref[1] = jnp.zeros(z_ref.shape[1:], F32)

    planes = lambda ref: jnp.concatenate([ref[j] for j in range(ref.shape[0])], axis=1)
    ya = jnp.dot(oa_ref[...], wa_ref[...], preferred_element_type=F32)
    h0 = _layer_norm(x_ref[...], g0_ref[...], b0_ref[...])
    gate_t = jnp.tanh(jnp.dot(h0.astype(BF16), wg_ref[...], preferred_element_type=F32) + bg_ref[...])

    h1 = _layer_norm(z_ref[(i + 1) % 2], g1_ref[...], b1_ref[...])
    h1_ref[...] = h1
    h1b_ref[...] = h1.astype(BF16)
    hi, lo = _split_bf16(h1)
    logit_ref[0] = (lax.dot_general(wr_hi_ref[...], hi, NT_DIMS, preferred_element_type=F32)
                    + lax.dot_general(wr_hi_ref[...], lo, NT_DIMS, preferred_element_type=F32)
                    + lax.dot_general(wr_lo_ref[...], hi, NT_DIMS, preferred_element_type=F32))

    l0, l1, l2 = planes(l0_ref), planes(l1_ref), planes(l2_ref)
    mx = jnp.maximum(jnp.maximum(l0, l1), l2)
    e0, e1, e2 = jnp.exp2(l0 - mx), jnp.exp2(l1 - mx), jnp.exp2(l2 - mx)
    ob = (e0 * planes(o0_ref) + e1 * planes(o1_ref) + e2 * planes(o2_ref)) / (e0 + e1 + e2)
    yb = jnp.dot(ob.astype(BF16), wb_ref[...], preferred_element_type=F32)
    merged = (ya + yb) + (gate_t[:, :D_MODEL] * ya + gate_t[:, D_MODEL:] * yb)
    mix = jnp.dot(merged.astype(BF16), wo_ref[...], preferred_element_type=F32)
    z_ref[i % 2] = DEEPNORM_ALPHA * h0 + mix


def _merge(x2, ln0_g, ln0_b, w_gates, b_gate, oa, obs, lses, w_a, w_b, w_o, ln1_g, ln1_b, w_router, seq, tm):
    m = x2.shape[0]
    n_tiles = m // tm
    tiles_per_seq = seq // tm
    src = lambda i: jnp.minimum(i, n_tiles - 1)
    dst = lambda i: jnp.maximum(i - 1, 0)
    const = lambda i: (0, 0)
    wr_t = w_router.T
    wr_hi = wr_t.astype(BF16)
    wr_lo = (wr_t - wr_hi.astype(F32)).astype(BF16)
    vec = lambda v: v.reshape(1, -1)
    full = lambda a: pl.BlockSpec(a.shape, const)
    tile_in = lambda w: pl.BlockSpec((tm, w), lambda i: (src(i), 0))
    tile_out = lambda w: pl.BlockSpec((tm, w), lambda i: (dst(i), 0))
    plane_in = pl.BlockSpec((B_GROUP_W // LANES, tm, LANES), lambda i: (0, src(i), 0))
    args = [x2, vec(ln0_g), vec(ln0_b), w_gates, vec(b_gate), oa, *obs, *lses, w_a, w_b, w_o,
            vec(ln1_g), vec(ln1_b), wr_hi, wr_lo]
    in_specs = [tile_in(D_MODEL), full(args[1]), full(args[2]), full(w_gates), full(args[4]), tile_in(A_Q_W),
                *[plane_in] * 6, full(w_a), full(w_b), full(w_o), full(args[15]), full(args[16]),
                full(wr_hi), full(wr_lo)]
    return pl.pallas_call(
        _merge_kernel,
        grid=(n_tiles + 1,),
        in_specs=in_specs,
        out_specs=[tile_out(D_MODEL), tile_out(D_MODEL),
                   pl.BlockSpec((1, N_EXPERTS, tm),
                                lambda i: (dst(i) // tiles_per_seq, 0, dst(i) % tiles_per_seq))],
        out_shape=[jax.ShapeDtypeStruct((m, D_MODEL), F32), jax.ShapeDtypeStruct((m, D_MODEL), BF16),
                   jax.ShapeDtypeStruct((m // seq, N_EXPERTS, seq), F32)],
        scratch_shapes=[pltpu.VMEM((2, tm, D_MODEL), F32)],
        compiler_params=_params(1),
        name="merge_ln1_router",
    )(*args)


PREFIX_CHUNK = 256


def _prefix_count(mask, tri):
    ones = jnp.where(mask, 1.0, 0.0)
    carry = jnp.zeros((mask.shape[0], 1), F32)
    outs, bounds = [], [carry]
    for j in range(mask.shape[1] // PREFIX_CHUNK):
        chunk = ones[:, j * PREFIX_CHUNK:(j + 1) * PREFIX_CHUNK]
        outs.append(jnp.dot(chunk.astype(BF16), tri, preferred_element_type=F32) + carry)
        carry = carry + jnp.sum(chunk, axis=1, keepdims=True)
        bounds.append(carry)
    return jnp.concatenate(outs, axis=1), bounds


THRESHOLD_BITS = 31
THRESHOLD_REFINE = 16


def _route_kernel(logit_ref, slot_ref, aff_ref, starts_ref, *, cap):
    n_seq = logit_ref.shape[0]
    affs = []
    for b in range(n_seq):
        lg = logit_ref[b]
        ex = jnp.exp(lg - jnp.max(lg, axis=0, keepdims=True))
        affs.append(ex / jnp.sum(ex, axis=0, keepdims=True))
    aff = jnp.concatenate(affs, axis=0)

    def count_ge(t):
        return jnp.sum(jnp.where(aff >= t, 1.0, 0.0), axis=1, keepdims=True)

    thr = jnp.zeros((aff.shape[0], 1), jnp.int32)
    for bit in range(THRESHOLD_BITS - 1, -1, -1):
        cand = thr | (1 << bit)
        thr = jnp.where(count_ge(pltpu.bitcast(cand, F32)) >= cap, cand, thr)
    lo = pltpu.bitcast(thr, F32)
    hi = pltpu.bitcast(thr + 1, F32)
    for _ in range(THRESHOLD_REFINE):
        mid = 0.5 * (lo + hi)
        take = count_ge(mid) >= cap
        lo = jnp.where(take, mid, lo)
        hi = jnp.where(take, hi, mid)
    above = aff >= hi
    tied = (aff >= lo) & (aff < hi)
    r = lax.broadcasted_iota(jnp.int32, (PREFIX_CHUNK, PREFIX_CHUNK), 0)
    c = lax.broadcasted_iota(jnp.int32, (PREFIX_CHUNK, PREFIX_CHUNK), 1)
    tri = jnp.where(r < c, 1.0, 0.0).astype(BF16)
    need = cap - jnp.sum(jnp.where(above, 1.0, 0.0), axis=1, keepdims=True)
    sel = above | (tied & (_prefix_count(tied, tri)[0] < need))
    rank, bounds = _prefix_count(sel, tri)
    slot = jnp.where(sel, rank, -1.0).astype(jnp.int32)
    lane = lax.broadcasted_iota(jnp.int32, (1, LANES), 1)
    starts = jnp.zeros((aff.shape[0], LANES), F32)
    for j, count in enumerate(bounds):
        starts = jnp.where(lane == j, count, starts)
    for b in range(n_seq):
        rows = slice(b * N_EXPERTS, (b + 1) * N_EXPERTS)
        slot_ref[b] = slot[rows]
        aff_ref[b] = aff[rows]
        starts_ref[b] = starts[rows].astype(jnp.int32)


ROUTE_SEQS_PER_STEP = 4


def _route(logits, batch, seq, cap):
    nb = ROUTE_SEQS_PER_STEP if batch % ROUTE_SEQS_PER_STEP == 0 else 1
    blk = pl.BlockSpec((nb, N_EXPERTS, seq), lambda i: (i, 0, 0))
    return pl.pallas_call(
        functools.partial(_route_kernel, cap=cap),
        grid=(batch // nb,),
        in_specs=[blk],
        out_specs=[blk, blk, pl.BlockSpec((nb, N_EXPERTS, LANES), lambda i: (i, 0, 0))],
        out_shape=[jax.ShapeDtypeStruct((batch, N_EXPERTS, seq), jnp.int32),
                   jax.ShapeDtypeStruct((batch, N_EXPERTS, seq), F32),
                   jax.ShapeDtypeStruct((batch, N_EXPERTS, LANES), jnp.int32)],
        compiler_params=_params(1),
        name="route",
    )(logits)


BF16_SUBLANES = 16
DISPATCH_WIN = 64
MOE_CHUNKS_PER_STEP = 4


def _slot_windows(starts_ref, b, j, n_bounds, win):
    base, n_win = [], 0
    for e in range(N_EXPERTS):
        at = (b * N_EXPERTS + e) * n_bounds + j
        first = (starts_ref[at] // BF16_SUBLANES) * BF16_SUBLANES
        base.append(first)
        n_win = jnp.maximum(n_win, (starts_ref[at + 1] - first + win - 1) // win)
    return base, n_win


def _window_rows(base_e, k, wrow, cap, win):
    lower = base_e + k * win
    off = pl.multiple_of(jnp.minimum(lower, cap - win), BF16_SUBLANES)
    return off, jnp.where(wrow + off >= lower, wrow + off, -2)


def _dispatch_kernel(starts_ref, h_ref, slot_ref, aff_ref, xe_ref, gate_ref, *, cap):
    b, step = pl.program_id(0), pl.program_id(1)
    win = DISPATCH_WIN
    n_bounds = pl.num_programs(1) * MOE_CHUNKS_PER_STEP + 1

    @pl.when(step == 0)
    def _():
        xe_ref[...] = jnp.zeros_like(xe_ref)
        gate_ref[...] = jnp.zeros_like(gate_ref)

    wrow = lax.broadcasted_iota(jnp.int32, (win, PREFIX_CHUNK), 0)
    for c in range(MOE_CHUNKS_PER_STEP):
        tokens = slice(c * PREFIX_CHUNK, (c + 1) * PREFIX_CHUNK)
        base, n_win = _slot_windows(starts_ref, b, step * MOE_CHUNKS_PER_STEP + c, n_bounds, win)
        h = h_ref[0, tokens, :]

        def window(k):
            offs, onehots = [], []
            for e in range(N_EXPERTS):
                off, target = _window_rows(base[e], k, wrow, cap, win)
                offs.append(off)
                onehots.append(slot_ref[0, e:e + 1, tokens] == target)
            sel = jnp.concatenate([jnp.where(o, 1.0, 0.0) for o in onehots], axis=0).astype(BF16)
            rows = jnp.dot(sel, h, preferred_element_type=F32).astype(BF16)
            for e in range(N_EXPERTS):
                at = pl.ds(offs[e], win)
                xe_ref[e, at, :] += rows[e * win:(e + 1) * win]
                gate = jnp.sum(jnp.where(onehots[e], aff_ref[0, e:e + 1, tokens], 0.0), axis=1, keepdims=True)
                gate_ref[e, at, :] += jnp.broadcast_to(gate, (win, LANES))

        window(0)
        lax.fori_loop(1, n_win, lambda k, carry: (window(k), carry)[1], 0)


def _dispatch(h1b, slot, aff, starts, cap):
    b, s, _ = h1b.shape
    n_chunks = s // PREFIX_CHUNK
    span = MOE_CHUNKS_PER_STEP * PREFIX_CHUNK
    chunk = pl.BlockSpec((1, N_EXPERTS, span), lambda i, j, st: (i, 0, j))
    grid_spec = pltpu.PrefetchScalarGridSpec(
        num_scalar_prefetch=1,
        grid=(b, s // span),
        in_specs=[pl.BlockSpec((1, span, D_MODEL), lambda i, j, st: (i, j, 0)), chunk, chunk],
        out_specs=[pl.BlockSpec((N_EXPERTS, cap, D_MODEL), lambda i, j, st: (0, i, 0)),
                   pl.BlockSpec((N_EXPERTS, cap, LANES), lambda i, j, st: (0, i, 0))],
    )
    return pl.pallas_call(
        functools.partial(_dispatch_kernel, cap=cap),
        grid_spec=grid_spec,
        out_shape=[jax.ShapeDtypeStruct((N_EXPERTS, b * cap, D_MODEL), BF16),
                   jax.ShapeDtypeStruct((N_EXPERTS, b * cap, LANES), F32)],
        compiler_params=_params(2),
        name="dispatch",
    )(starts[:, :, :n_chunks + 1].reshape(-1), h1b, slot, aff)


FF_CHUNK = 512


def _expert_kernel(x_ref, gate_ref, wg_ref, wu_ref, wd_ref, y_ref):
    x = x_ref[0]
    acc = jnp.zeros((x.shape[0], D_MODEL), F32)
    for c in range(D_FF // FF_CHUNK):
        cols = slice(c * FF_CHUNK, (c + 1) * FF_CHUNK)
        g = jnp.dot(x, wg_ref[0, :, cols], preferred_element_type=F32)
        u = jnp.dot(x, wu_ref[0, :, cols], preferred_element_type=F32)
        act = (g * jax.nn.sigmoid(g) * u).astype(BF16)
        acc = acc + jnp.dot(act, wd_ref[0, cols, :], preferred_element_type=F32)
    y_ref[0] = (acc * gate_ref[0][:, :1]).astype(BF16)


def _expert_ffn(xe, gate, w_gate, w_up, w_down, tm):
    e, rows, _ = xe.shape
    tok = lambda w: pl.BlockSpec((1, tm, w), lambda i, j: (i, j, 0))
    wspec = lambda a: pl.BlockSpec((1,) + a.shape[1:], lambda i, j: (i, 0, 0))
    return pl.pallas_call(
        _expert_kernel,
        grid=(e, rows // tm),
        in_specs=[tok(D_MODEL), tok(LANES), wspec(w_gate), wspec(w_up), wspec(w_down)],
        out_specs=tok(D_MODEL),
        out_shape=jax.ShapeDtypeStruct((e, rows, D_MODEL), BF16),
        compiler_params=_params(2),
        name="expert_ffn",
    )(xe, gate, w_gate, w_up, w_down)


COMBINE_WIN = 64
TN_DIMS = (((0,), (0,)), ((), ()))


def _combine_kernel(starts_ref, h1_ref, slot_ref, y_ref, g2_ref, b2_ref, out_ref, acc_ref, *, cap, n_chunks):
    s = pl.program_id(0)
    cur = jnp.minimum(s, pl.num_programs(0) - 2)
    steps_per_seq = n_chunks // MOE_CHUNKS_PER_STEP
    win = COMBINE_WIN

    @pl.when(s == 0)
    def _():
        acc_ref[1] = jnp.zeros(acc_ref.shape[1:], F32)

    out_ref[0] = _layer_norm(DEEPNORM_ALPHA * h1_ref[0] + acc_ref[(s + 1) % 2], g2_ref[...], b2_ref[...])

    wrow = lax.broadcasted_iota(jnp.int32, (win, PREFIX_CHUNK), 0)
    for c in range(MOE_CHUNKS_PER_STEP):
        tokens = slice(c * PREFIX_CHUNK, (c + 1) * PREFIX_CHUNK)
        chunk = (cur % steps_per_seq) * MOE_CHUNKS_PER_STEP + c
        base, n_win = _slot_windows(starts_ref, cur // steps_per_seq, chunk, n_chunks + 1, win)

        def gather(k):
            sels, rows = [], []
            for e in range(N_EXPERTS):
                off, target = _window_rows(base[e], k, wrow, cap, win)
                sels.append(jnp.where(slot_ref[0, e:e + 1, tokens] == target, 1.0, 0.0))
                rows.append(y_ref[e, pl.ds(off, win), :])
            sel = jnp.concatenate(sels, axis=0).astype(BF16)
            return lax.dot_general(sel, jnp.concatenate(rows, axis=0), TN_DIMS, preferred_element_type=F32)

        acc_ref[s % 2, tokens, :] = gather(0)

        def more(k, carry):
            acc_ref[s % 2, tokens, :] += gather(k)
            return carry

        lax.fori_loop(1, n_win, more, 0)


def _combine(h1, slot, starts, y, ln2_g, ln2_b, cap):
    b, s, _ = h1.shape
    n_chunks = s // PREFIX_CHUNK
    span = MOE_CHUNKS_PER_STEP * PREFIX_CHUNK
    per_seq = s // span
    n_steps = b * per_seq
    cur = lambda i: jnp.minimum(i, n_steps - 1)
    prev = lambda i: jnp.maximum(i - 1, 0)
    const = lambda i, st: (0, 0)
    grid_spec = pltpu.PrefetchScalarGridSpec(
        num_scalar_prefetch=1,
        grid=(n_steps + 1,),
        in_specs=[pl.BlockSpec((1, span, D_MODEL), lambda i, st: (prev(i) // per_seq, prev(i) % per_seq, 0)),
                  pl.BlockSpec((1, N_EXPERTS, span), lambda i, st: (cur(i) // per_seq, 0, cur(i) % per_seq)),
                  pl.BlockSpec((N_EXPERTS, cap, D_MODEL), lambda i, st: (0, cur(i) // per_seq, 0)),
                  pl.BlockSpec((1, D_MODEL), const), pl.BlockSpec((1, D_MODEL), const)],
        out_specs=pl.BlockSpec((1, span, D_MODEL), lambda i, st: (prev(i) // per_seq, prev(i) % per_seq, 0)),
        scratch_shapes=[pltpu.VMEM((2, span, D_MODEL), F32)],
    )
    return pl.pallas_call(
        functools.partial(_combine_kernel, cap=cap, n_chunks=n_chunks),
        grid_spec=grid_spec,
        out_shape=jax.ShapeDtypeStruct((b, s, D_MODEL), F32),
        compiler_params=_params(1),
        name="combine_ln2",
    )(starts[:, :, :n_chunks + 1].reshape(-1), h1, slot, y, ln2_g.reshape(1, D_MODEL), ln2_b.reshape(1, D_MODEL))


def kernel(x, ln0_g, ln0_b, w_in, b_gate, qn_g, kn_g, w_branch_a, w_branch_b, w_out, ln1_g, ln1_b, w_router,
           w_gate_e, w_up_e, w_down_e, ln2_g, ln2_b):
    batch, seq, _ = x.shape
    assert w_in.shape[0] == DEPTH
    cap = EC_CAPACITY_FACTOR * seq // N_EXPERTS
    tokens = batch * seq
    x2 = x.reshape(tokens, D_MODEL)
    w_qkv = w_in[0, :, :QKV_W].astype(BF16)
    w_gates = (0.5 * w_in[0, :, QKV_W:]).astype(BF16)

    qkv = _in_projection(x2, ln0_g, ln0_b, w_qkv, qn_g[0], kn_g[0], seq, tm=IN_PROJ_ROW_TILE)
    qa, ka, va = qkv[:3]
    qb, kb, vb = qkv[3:6], qkv[6:9], qkv[9:12]

    oa, (wg_b, wu_b, wd_b) = _attention_a(qa.reshape(batch, seq, A_Q_W), ka.reshape(batch, seq, A_KV_W),
                                          va.reshape(batch, seq, A_KV_W), (w_gate_e[0], w_up_e[0], w_down_e[0]),
                                          tq=ROW_TILE)
    oa = oa.reshape(tokens, A_Q_W)

    obs, lses = [], []
    for gi, (window, dilation) in enumerate(B_GROUPS):
        assert window // (2 * dilation) == BAND_HALF
        o, lse = _dilated_attention(qb[gi], kb[gi], vb[gi], seq, dilation)
        obs.append(o)
        lses.append(lse)

    h1, h1b, logits = _merge(x2, ln0_g, ln0_b, w_gates, 0.5 * b_gate[0], oa, obs, lses,
                             w_branch_a[0].astype(BF16), w_branch_b[0].astype(BF16), (0.5 * w_out[0]).astype(BF16),
                             ln1_g[0], ln1_b[0], w_router[0], seq, tm=ROW_TILE)

    slot, aff, starts = _route(logits, batch, seq, cap)
    xe, gate = _dispatch(h1b.reshape(batch, seq, D_MODEL), slot, aff, starts, cap)
    y = _expert_ffn(xe, gate, wg_b, wu_b, wd_b, tm=min(FFN_ROW_TILE, batch * cap))
    return _combine(h1.reshape(batch, seq, D_MODEL), slot, starts, y, ln2_g[0], ln2_b[0], cap)
```

```python
import functools

import jax
import jax.numpy as jnp
import numpy as np
from jax import lax
from jax.experimental import pallas as pl
from jax.experimental.pallas import tpu as pltpu

F32 = jnp.float32
BF16 = jnp.bfloat16

D_MODEL = 1024
HEAD_DIM = 64
A_Q_HEADS = 8
A_KV_HEADS = 2
A_GROUP = A_Q_HEADS // A_KV_HEADS
B_GROUPS = ((128, 1), (512, 4), (2048, 16))
B_HEADS_PER_GROUP = 4
GRID_W = 64
A_ROPE_THETA = 10000.0
B_ROPE_THETA = 500000.0
B_ROPE_DIMS = HEAD_DIM // 4
N_EXPERTS = 16
EC_CAPACITY_FACTOR = 2
D_FF = 2 * D_MODEL
LN_EPS = 1e-5
QK_NORM_EPS = 1e-6
MASK_VALUE = -1e30
DEPTH = 1
DEEPNORM_ALPHA = (2.0 * DEPTH) ** 0.25
LOG2_E = 1.4426950408889634
Q_SCALE = HEAD_DIM ** -0.5 * LOG2_E

A_Q_W = A_Q_HEADS * HEAD_DIM
A_KV_W = A_KV_HEADS * HEAD_DIM
B_GROUP_W = B_HEADS_PER_GROUP * HEAD_DIM
B_W = B_GROUP_W * len(B_GROUPS)
QKV_W = A_Q_W + 2 * A_KV_W + 3 * B_W
BAND_HALF = 64

LANES = 128
VMEM_LIMIT = 56 * 1024 * 1024
ROW_TILE = 512
IN_PROJ_ROW_TILE = 1024
FFN_ROW_TILE = 1024

NT_DIMS = (((1,), (1,)), ((), ()))


def _params(n_axes):
    return pltpu.CompilerParams(dimension_semantics=("arbitrary",) * n_axes, vmem_limit_bytes=VMEM_LIMIT)


def _layer_norm(x, g, b):
    mu = jnp.mean(x, axis=-1, keepdims=True)
    xc = x - mu
    var = jnp.mean(xc * xc, axis=-1, keepdims=True)
    return xc * lax.rsqrt(var + LN_EPS) * g + b


def _split_bf16(x):
    hi = x.astype(BF16)
    lo = (x - hi.astype(F32)).astype(BF16)
    return hi, lo


def _swap_halves(y, lane, period, half):
    fwd = pltpu.roll(y, LANES - half, 1)
    bwd = pltpu.roll(y, half, 1)
    return jnp.where((lane % period) < half, fwd, bwd)


def _inproj_kernel(x_ref, g0_ref, b0_ref, w_ref, qng_ref, kng_ref, cosa_ref, sina_ref, cosb_ref, sinb_ref,
                   qa_ref, ka_ref, va_ref, qb0_ref, qb1_ref, qb2_ref, kb0_ref, kb1_ref, kb2_ref,
                   vb0_ref, vb1_ref, vb2_ref, h_ref):
    step = pl.program_id(0)

    @pl.when(step == 0)
    def _():
        h_ref[1] = jnp.zeros(h_ref.shape[1:], BF16)

    h = h_ref[(step + 1) % 2]
    lane = lax.broadcasted_iota(jnp.int32, (1, LANES), 1)
    r = lax.broadcasted_iota(jnp.int32, (LANES, LANES), 0)
    c = lax.broadcasted_iota(jnp.int32, (LANES, LANES), 1)
    head_sum = jnp.where((r // HEAD_DIM) == (c // HEAD_DIM), 1.0, 0.0).astype(BF16)
    cosa, sina = cosa_ref[...], sina_ref[...]
    cosb, sinb = cosb_ref[...], sinb_ref[...]

    def proj(off):
        y = jnp.dot(h, w_ref[:, off:off + 2 * LANES], preferred_element_type=F32)
        return y[:, :LANES], y[:, LANES:]

    def norm_rope_a(y, gain, scale):
        hi, lo = _split_bf16(y * y)
        ss = (jnp.dot(hi, head_sum, preferred_element_type=F32)
              + jnp.dot(lo, head_sum, preferred_element_type=F32))
        y = y * lax.rsqrt(ss * (1.0 / HEAD_DIM) + QK_NORM_EPS) * gain
        y = y * cosa + _swap_halves(y, lane, HEAD_DIM // 2, HEAD_DIM // 4) * sina
        return (y * scale).astype(BF16)

    def rope_b(y, scale):
        y = y * cosb + _swap_halves(y, lane, HEAD_DIM, B_ROPE_DIMS // 2) * sinb
        return y * scale

    b_base = A_Q_W + 2 * A_KV_W

    def project_b_qk(gi, q_ref, k_ref):
        off = b_base + gi * B_GROUP_W
        for j, (q, k) in enumerate(zip(proj(off), proj(off + B_W))):
            q_ref[j] = rope_b(q, Q_SCALE)
            k_ref[j] = rope_b(k, 1.0)

    qa_tiles = [y for j in range(A_Q_W // (2 * LANES)) for y in proj(2 * j * LANES)]
    ka, va = proj(A_Q_W)
    h_next = _layer_norm(x_ref[...], g0_ref[...], b0_ref[...]).astype(BF16)
    project_b_qk(0, qb0_ref, kb0_ref)
    for j, y in enumerate(qa_tiles):
        qa_ref[:, j * LANES:(j + 1) * LANES] = norm_rope_a(y, qng_ref[...], Q_SCALE)
    ka_ref[...] = norm_rope_a(ka, kng_ref[...], 1.0)
    va_ref[...] = va.astype(BF16)
    project_b_qk(1, qb1_ref, kb1_ref)
    project_b_qk(2, qb2_ref, kb2_ref)
    for gi, v_ref in enumerate((vb0_ref, vb1_ref, vb2_ref)):
        for j, v in enumerate(proj(b_base + 2 * B_W + gi * B_GROUP_W)):
            v_ref[j] = v

    h_ref[step % 2] = h_next


def _rope_tables(seq):
    f32 = np.float32
    d = np.arange(LANES) % HEAD_DIM
    t = np.arange(seq)
    half_rot = HEAD_DIM // 2
    inv_a = f32(A_ROPE_THETA) ** (-np.arange(0, half_rot, 2, dtype=f32) / f32(half_rot))
    j = d % half_rot
    pos_row = (t // GRID_W).astype(f32)
    pos_col = (t % GRID_W).astype(f32)
    freq_a = inv_a[j % (half_rot // 2)].astype(f32)
    ang_a = np.where((d < half_rot)[None, :], pos_row[:, None], pos_col[:, None]) * freq_a[None, :]
    sign_a = np.where(j < half_rot // 2, -1.0, 1.0).astype(f32)
    cosa = np.cos(ang_a).astype(f32)
    sina = (np.sin(ang_a) * sign_a[None, :]).astype(f32)
    inv_b = f32(B_ROPE_THETA) ** (-np.arange(0, B_ROPE_DIMS, 2, dtype=f32) / f32(B_ROPE_DIMS))
    freq_b = inv_b[d % (B_ROPE_DIMS // 2)].astype(f32)
    ang_b = t.astype(f32)[:, None] * freq_b[None, :]
    rot = (d < B_ROPE_DIMS)[None, :]
    sign_b = np.where(d < B_ROPE_DIMS // 2, -1.0, 1.0).astype(f32)
    cosb = np.where(rot, np.cos(ang_b), 1.0).astype(f32)
    sinb = np.where(rot, np.sin(ang_b) * sign_b[None, :], 0.0).astype(f32)
    return tuple(jnp.asarray(a) for a in (cosa, sina, cosb, sinb))


def _in_projection(x2, ln0_g, ln0_b, w_qkv, qn_g, kn_g, seq, tm):
    m = x2.shape[0]
    cosa, sina, cosb, sinb = _rope_tables(seq)
    n_tiles = m // tm
    tiles_per_seq = seq // tm
    dst = lambda i: jnp.maximum(i - 1, 0)
    row = lambda i: (dst(i), 0)
    const = lambda i: (0, 0)
    table = lambda i: (dst(i) % tiles_per_seq, 0)
    widths = (A_Q_W, A_KV_W, A_KV_W) + (B_GROUP_W,) * 9
    gain_tile = lambda g: jnp.tile(g.reshape(1, HEAD_DIM), (1, LANES // HEAD_DIM))
    return pl.pallas_call(
        _inproj_kernel,
        grid=(n_tiles + 1,),
        in_specs=[
            pl.BlockSpec((tm, D_MODEL), lambda i: (jnp.minimum(i, n_tiles - 1), 0)),
            pl.BlockSpec((1, D_MODEL), const),
            pl.BlockSpec((1, D_MODEL), const),
            pl.BlockSpec((D_MODEL, QKV_W), const),
            pl.BlockSpec((1, LANES), const),
            pl.BlockSpec((1, LANES), const),
            pl.BlockSpec((tm, LANES), table),
            pl.BlockSpec((tm, LANES), table),
            pl.BlockSpec((tm, LANES), table),
            pl.BlockSpec((tm, LANES), table),
        ],
        out_specs=([pl.BlockSpec((tm, w), row) for w in widths[:3]]
                   + [pl.BlockSpec((B_GROUP_W // LANES, tm, LANES), lambda i: (0, dst(i), 0))] * 9),
        out_shape=([jax.ShapeDtypeStruct((m, w), BF16) for w in widths[:3]]
                   + [jax.ShapeDtypeStruct((B_GROUP_W // LANES, m, LANES), F32)] * 9),
        scratch_shapes=[pltpu.VMEM((2, tm, D_MODEL), BF16)],
        compiler_params=_params(1),
        name="in_projection",
    )(x2, ln0_g.reshape(1, D_MODEL), ln0_b.reshape(1, D_MODEL), w_qkv, gain_tile(qn_g), gain_tile(kn_g),
      cosa, sina, cosb, sinb)


def _attn_a_kernel(q_ref, k_ref, v_ref, wg_ref, wu_ref, wd_ref, o_ref, wgb_ref, wub_ref, wdb_ref):
    wgb_ref[...] = wg_ref[...].astype(BF16)
    wub_ref[...] = wu_ref[...].astype(BF16)
    wdb_ref[...] = wd_ref[...].astype(BF16)
    def scores(hq):
        kh = hq // A_GROUP
        q = q_ref[0, :, hq * HEAD_DIM:(hq + 1) * HEAD_DIM]
        k = k_ref[0, :, kh * HEAD_DIM:(kh + 1) * HEAD_DIM]
        return lax.dot_general(q, k, NT_DIMS, preferred_element_type=F32)

    outs = []
    v_ones = []
    for kh in range(A_KV_HEADS):
        v = v_ref[0, :, kh * HEAD_DIM:(kh + 1) * HEAD_DIM]
        v_ones.append(jnp.concatenate([v, jnp.ones_like(v)], axis=1))
    s = scores(0)
    for hq in range(A_Q_HEADS):
        s_next = scores(hq + 1) if hq + 1 < A_Q_HEADS else None
        kh = hq // A_GROUP
        p = jnp.exp2(s - jnp.max(s, axis=-1, keepdims=True)).astype(BF16)
        o = jnp.dot(p, v_ones[hq // A_GROUP], preferred_element_type=F32)
        outs.append(o[:, :HEAD_DIM] / o[:, HEAD_DIM:HEAD_DIM + 1])
        s = s_next
    o_ref[0] = jnp.concatenate(outs, axis=1).astype(BF16)


def _attention_a(qa, ka, va, expert_weights, tq):
    b, s, _ = qa.shape
    steps = b * (s // tq)
    flat = [w.reshape(-1, w.shape[-1]) for w in expert_weights]
    wslice = lambda w: pl.BlockSpec((w.shape[0] // steps, w.shape[1]), lambda i, j: (i * (s // tq) + j, 0))
    o, *rounded = pl.pallas_call(
        _attn_a_kernel,
        grid=(b, s // tq),
        in_specs=[
            pl.BlockSpec((1, tq, A_Q_W), lambda i, j: (i, j, 0)),
            pl.BlockSpec((1, s, A_KV_W), lambda i, j: (i, 0, 0)),
            pl.BlockSpec((1, s, A_KV_W), lambda i, j: (i, 0, 0)),
            *map(wslice, flat),
        ],
        out_specs=[pl.BlockSpec((1, tq, A_Q_W), lambda i, j: (i, j, 0)), *map(wslice, flat)],
        out_shape=[jax.ShapeDtypeStruct((b, s, A_Q_W), BF16),
                   *[jax.ShapeDtypeStruct(w.shape, BF16) for w in flat]],
        compiler_params=_params(2),
        name="attention_a",
    )(qa, ka, va, *flat)
    return o, [r.reshape(w.shape) for r, w in zip(rounded, expert_weights)]


BAND_QB = 128
BAND_UNITS_PER_STEP = 16
BAND_STEPS = 1


def _band_bias(n):
    win = min(2 * BAND_QB, n)
    rows = B_HEADS_PER_GROUP * BAND_QB
    rel = (np.arange(rows)[:, None] % BAND_QB) - np.arange(win)[None, :]
    shifts = np.arange(2 * BAND_QB // BAND_HALF - 1) * BAND_HALF
    inside = np.abs(rel[None] + shifts[:, None, None]) <= BAND_HALF
    return jnp.asarray(np.where(inside, 0.0, MASK_VALUE).astype(np.float32))


def _dilated_kernel(q_ref, k_ref, v_ref, bias_ref, o_ref, lse_ref, *, n, dilation):
    win = min(2 * BAND_QB, n)
    n_blk = n // BAND_QB
    lane_head = lax.broadcasted_iota(jnp.int32, (1, B_GROUP_W), 1) // HEAD_DIM

    planes = B_GROUP_W // LANES

    def class_rows(r, start, size):
        if dilation == 1:
            return pl.ds(pl.multiple_of(start, BAND_HALF), size)
        return pl.ds(r + start * dilation, size, stride=dilation)

    def load(ref, rows_):
        return jnp.concatenate([ref[j, rows_, :] for j in range(planes)], axis=1)

    def scores(i):
        unit = pl.program_id(1) * BAND_UNITS_PER_STEP + i
        r = unit // n_blk
        qs = (unit % n_blk) * BAND_QB
        ks = jnp.clip(qs - BAND_HALF, 0, n - win)
        q = load(q_ref, class_rows(r, qs, BAND_QB))
        k = load(k_ref, class_rows(r, ks, win)).astype(BF16)
        q_stack = jnp.concatenate([jnp.where(lane_head == h, q, 0.0) for h in range(B_HEADS_PER_GROUP)], axis=0)
        s = lax.dot_general(q_stack.astype(BF16), k, NT_DIMS, preferred_element_type=F32)
        return r, qs, ks, s + bias_ref[(qs - ks) // BAND_HALF]

    nxt = scores(0)
    for i in range(BAND_UNITS_PER_STEP):
        r, qs, ks, s = nxt
        if i + 1 < BAND_UNITS_PER_STEP:
            nxt = scores(i + 1)
        v = load(v_ref, class_rows(r, ks, win)).astype(BF16)
        mx = jnp.max(s, axis=-1, keepdims=True)
        p = jnp.exp2(s - mx)
        denom = jnp.sum(p, axis=-1, keepdims=True)
        pv = jnp.dot(p.astype(BF16), v, preferred_element_type=F32) / denom
        lse = mx + jnp.log2(denom)
        o = jnp.zeros((BAND_QB, B_GROUP_W), F32)
        l = jnp.zeros((BAND_QB, B_GROUP_W), F32)
        for h in range(B_HEADS_PER_GROUP):
            blk = slice(h * BAND_QB, (h + 1) * BAND_QB)
            o = jnp.where(lane_head == h, pv[blk], o)
            l = jnp.where(lane_head == h, lse[blk], l)
        for j in range(planes):
            o_ref[j, class_rows(r, qs, BAND_QB), :] = o[:, j * LANES:(j + 1) * LANES]
            lse_ref[j, class_rows(r, qs, BAND_QB), :] = l[:, j * LANES:(j + 1) * LANES]


def _dilated_attention(q, k, v, seq, dilation):
    planes, tokens, _ = q.shape
    n = seq // dilation
    assert dilation * (n // BAND_QB) == BAND_UNITS_PER_STEP * BAND_STEPS
    blk = pl.BlockSpec((planes, seq, LANES), lambda i, j: (0, i, 0))
    bias = _band_bias(n)
    return pl.pallas_call(
        functools.partial(_dilated_kernel, n=n, dilation=dilation),
        grid=(tokens // seq, BAND_STEPS),
        in_specs=[blk, blk, blk, pl.BlockSpec(bias.shape, lambda i, j: (0, 0, 0))],
        out_specs=[blk, blk],
        out_shape=[jax.ShapeDtypeStruct(q.shape, F32)] * 2,
        compiler_params=_params(2),
        name=f"dilated_attention_d{dilation}",
    )(q, k, v, bias)


def _merge_kernel(x_ref, g0_ref, b0_ref, wg_ref, bg_ref, oa_ref, o0_ref, o1_ref, o2_ref, l0_ref, l1_ref, l2_ref,
                  wa_ref, wb_ref, wo_ref, g1_ref, b1_ref, wr_hi_ref, wr_lo_ref,
                  h1_ref, h1b_ref, logit_ref, z_ref):
    i = pl.program_id(0)

    @pl.when(i == 0)
    def _():
        z_ref[1] = jnp.zeros(z_ref.shape[1:], F32)

    planes = lambda ref: jnp.concatenate([ref[j] for j in range(ref.shape[0])], axis=1)
    ya = jnp.dot(oa_ref[...], wa_ref[...], preferred_element_type=F32)
    h0 = _layer_norm(x_ref[...], g0_ref[...], b0_ref[...])
    gate_t = jnp.tanh(jnp.dot(h0.astype(BF16), wg_ref[...], preferred_element_type=F32) + bg_ref[...])

    h1 = _layer_norm(z_ref[(i + 1) % 2], g1_ref[...], b1_ref[...])
    h1_ref[...] = h1
    h1b_ref[...] = h1.astype(BF16)
    hi, lo = _split_bf16(h1)
    logit_ref[0] = (lax.dot_general(wr_hi_ref[...], hi, NT_DIMS, preferred_element_type=F32)
                    + lax.dot_general(wr_hi_ref[...], lo, NT_DIMS, preferred_element_type=F32)
                    + lax.dot_general(wr_lo_ref[...], hi, NT_DIMS, preferred_element_type=F32))

    l0, l1, l2 = planes(l0_ref), planes(l1_ref), planes(l2_ref)
    mx = jnp.maximum(jnp.maximum(l0, l1), l2)
    e0, e1, e2 = jnp.exp2(l0 - mx), jnp.exp2(l1 - mx), jnp.exp2(l2 - mx)
    ob = (e0 * planes(o0_ref) + e1 * planes(o1_ref) + e2 * planes(o2_ref)) / (e0 + e1 + e2)
    yb = jnp.dot(ob.astype(BF16), wb_ref[...], preferred_element_type=F32)
    merged = (ya + yb) + (gate_t[:, :D_MODEL] * ya + gate_t[:, D_MODEL:] * yb)
    mix = jnp.dot(merged.astype(BF16), wo_ref[...], preferred_element_type=F32)
    z_ref[i % 2] = DEEPNORM_ALPHA * h0 + mix


def _merge(x2, ln0_g, ln0_b, w_gates, b_gate, oa, obs, lses, w_a, w_b, w_o, ln1_g, ln1_b, w_router, seq, tm):
    m = x2.shape[0]
    n_tiles = m // tm
    tiles_per_seq = seq // tm
    src = lambda i: jnp.minimum(i, n_tiles - 1)
    dst = lambda i: jnp.maximum(i - 1, 0)
    const = lambda i: (0, 0)
    wr_t = w_router.T
    wr_hi = wr_t.astype(BF16)
    wr_lo = (wr_t - wr_hi.astype(F32)).astype(BF16)
    vec = lambda v: v.reshape(1, -1)
    full = lambda a: pl.BlockSpec(a.shape, const)
    tile_in = lambda w: pl.BlockSpec((tm, w), lambda i: (src(i), 0))
    tile_out = lambda w: pl.BlockSpec((tm, w), lambda i: (dst(i), 0))
    plane_in = pl.BlockSpec((B_GROUP_W // LANES, tm, LANES), lambda i: (0, src(i), 0))
    args = [x2, vec(ln0_g), vec(ln0_b), w_gates, vec(b_gate), oa, *obs, *lses, w_a, w_b, w_o,
            vec(ln1_g), vec(ln1_b), wr_hi, wr_lo]
    in_specs = [tile_in(D_MODEL), full(args[1]), full(args[2]), full(w_gates), full(args[4]), tile_in(A_Q_W),
                *[plane_in] * 6, full(w_a), full(w_b), full(w_o), full(args[15]), full(args[16]),
                full(wr_hi), full(wr_lo)]
    return pl.pallas_call(
        _merge_kernel,
        grid=(n_tiles + 1,),
        in_specs=in_specs,
        out_specs=[tile_out(D_MODEL), tile_out(D_MODEL),
                   pl.BlockSpec((1, N_EXPERTS, tm),
                                lambda i: (dst(i) // tiles_per_seq, 0, dst(i) % tiles_per_seq))],
        out_shape=[jax.ShapeDtypeStruct((m, D_MODEL), F32), jax.ShapeDtypeStruct((m, D_MODEL), BF16),
                   jax.ShapeDtypeStruct((m // seq, N_EXPERTS, seq), F32)],
        scratch_shapes=[pltpu.VMEM((2, tm, D_MODEL), F32)],
        compiler_params=_params(1),
        name="merge_ln1_router",
    )(*args)


PREFIX_CHUNK = 256


def _prefix_count(mask, tri):
    ones = jnp.where(mask, 1.0, 0.0)
    carry = jnp.zeros((mask.shape[0], 1), F32)
    outs, bounds = [], [carry]
    for j in range(mask.shape[1] // PREFIX_CHUNK):
        chunk = ones[:, j * PREFIX_CHUNK:(j + 1) * PREFIX_CHUNK]
        outs.append(jnp.dot(chunk.astype(BF16), tri, preferred_element_type=F32) + carry)
        carry = carry + jnp.sum(chunk, axis=1, keepdims=True)
        bounds.append(carry)
    return jnp.concatenate(outs, axis=1), bounds


THRESHOLD_BITS = 31
THRESHOLD_REFINE = 16


def _route_kernel(logit_ref, slot_ref, aff_ref, starts_ref, *, cap):
    n_seq = logit_ref.shape[0]
    affs = []
    for b in range(n_seq):
        lg = logit_ref[b]
        ex = jnp.exp(lg - jnp.max(lg, axis=0, keepdims=True))
        affs.append(ex / jnp.sum(ex, axis=0, keepdims=True))
    aff = jnp.concatenate(affs, axis=0)

    def count_ge(t):
        return jnp.sum(jnp.where(aff >= t, 1.0, 0.0), axis=1, keepdims=True)

    thr = jnp.zeros((aff.shape[0], 1), jnp.int32)
    for bit in range(THRESHOLD_BITS - 1, -1, -1):
        cand = thr | (1 << bit)
        thr = jnp.where(count_ge(pltpu.bitcast(cand, F32)) >= cap, cand, thr)
    lo = pltpu.bitcast(thr, F32)
    hi = pltpu.bitcast(thr + 1, F32)
    for _ in range(THRESHOLD_REFINE):
        mid = 0.5 * (lo + hi)
        take = count_ge(mid) >= cap
        lo = jnp.where(take, mid, lo)
        hi = jnp.where(take, hi, mid)
    above = aff >= hi
    tied = (aff >= lo) & (aff < hi)
    r = lax.broadcasted_iota(jnp.int32, (PREFIX_CHUNK, PREFIX_CHUNK), 0)
    c = lax.broadcasted_iota(jnp.int32, (PREFIX_CHUNK, PREFIX_CHUNK), 1)
    tri = jnp.where(r < c, 1.0, 0.0).astype(BF16)
    need = cap - jnp.sum(jnp.where(above, 1.0, 0.0), axis=1, keepdims=True)
    sel = above | (tied & (_prefix_count(tied, tri)[0] < need))
    rank, bounds = _prefix_count(sel, tri)
    slot = jnp.where(sel, rank, -1.0).astype(jnp.int32)
    lane = lax.broadcasted_iota(jnp.int32, (1, LANES), 1)
    starts = jnp.zeros((aff.shape[0], LANES), F32)
    for j, count in enumerate(bounds):
        starts = jnp.where(lane == j, count, starts)
    for b in range(n_seq):
        rows = slice(b * N_EXPERTS, (b + 1) * N_EXPERTS)
        slot_ref[b] = slot[rows]
        aff_ref[b] = aff[rows]
        starts_ref[b] = starts[rows].astype(jnp.int32)


ROUTE_SEQS_PER_STEP = 4


def _route(logits, batch, seq, cap):
    nb = ROUTE_SEQS_PER_STEP if batch % ROUTE_SEQS_PER_STEP == 0 else 1
    blk = pl.BlockSpec((nb, N_EXPERTS, seq), lambda i: (i, 0, 0))
    return pl.pallas_call(
        functools.partial(_route_kernel, cap=cap),
        grid=(batch // nb,),
        in_specs=[blk],
        out_specs=[blk, blk, pl.BlockSpec((nb, N_EXPERTS, LANES), lambda i: (i, 0, 0))],
        out_shape=[jax.ShapeDtypeStruct((batch, N_EXPERTS, seq), jnp.int32),
                   jax.ShapeDtypeStruct((batch, N_EXPERTS, seq), F32),
                   jax.ShapeDtypeStruct((batch, N_EXPERTS, LANES), jnp.int32)],
        compiler_params=_params(1),
        name="route",
    )(logits)


BF16_SUBLANES = 16
DISPATCH_WIN = 64
MOE_CHUNKS_PER_STEP = 4


def _slot_windows(starts_ref, b, j, n_bounds, win):
    base, n_win = [], 0
    for e in range(N_EXPERTS):
        at = (b * N_EXPERTS + e) * n_bounds + j
        first = (starts_ref[at] // BF16_SUBLANES) * BF16_SUBLANES
        base.append(first)
        n_win = jnp.maximum(n_win, (starts_ref[at + 1] - first + win - 1) // win)
    return base, n_win


def _window_rows(base_e, k, wrow, cap, win):
    lower = base_e + k * win
    off = pl.multiple_of(jnp.minimum(lower, cap - win), BF16_SUBLANES)
    return off, jnp.where(wrow + off >= lower, wrow + off, -2)


def _dispatch_kernel(starts_ref, h_ref, slot_ref, aff_ref, xe_ref, gate_ref, *, cap):
    b, step = pl.program_id(0), pl.program_id(1)
    win = DISPATCH_WIN
    n_bounds = pl.num_programs(1) * MOE_CHUNKS_PER_STEP + 1

    @pl.when(step == 0)
    def _():
        xe_ref[...] = jnp.zeros_like(xe_ref)
        gate_ref[...] = jnp.zeros_like(gate_ref)

    wrow = lax.broadcasted_iota(jnp.int32, (win, PREFIX_CHUNK), 0)
    for c in range(MOE_CHUNKS_PER_STEP):
        tokens = slice(c * PREFIX_CHUNK, (c + 1) * PREFIX_CHUNK)
        base, n_win = _slot_windows(starts_ref, b, step * MOE_CHUNKS_PER_STEP + c, n_bounds, win)
        h = h_ref[0, tokens, :]

        def window(k):
            offs, onehots = [], []
            for e in range(N_EXPERTS):
                off, target = _window_rows(base[e], k, wrow, cap, win)
                offs.append(off)
                onehots.append(slot_ref[0, e:e + 1, tokens] == target)
            sel = jnp.concatenate([jnp.where(o, 1.0, 0.0) for o in onehots], axis=0).astype(BF16)
            rows = jnp.dot(sel, h, preferred_element_type=F32).astype(BF16)
            for e in range(N_EXPERTS):
                at = pl.ds(offs[e], win)
                xe_ref[e, at, :] += rows[e * win:(e + 1) * win]
                gate = jnp.sum(jnp.where(onehots[e], aff_ref[0, e:e + 1, tokens], 0.0), axis=1, keepdims=True)
                gate_ref[e, at, :] += jnp.broadcast_to(gate, (win, LANES))

        window(0)
        lax.fori_loop(1, n_win, lambda k, carry: (window(k), carry)[1], 0)


def _dispatch(h1b, slot, aff, starts, cap):
    b, s, _ = h1b.shape
    n_chunks = s // PREFIX_CHUNK
    span = MOE_CHUNKS_PER_STEP * PREFIX_CHUNK
    chunk = pl.BlockSpec((1, N_EXPERTS, span), lambda i, j, st: (i, 0, j))
    grid_spec = pltpu.PrefetchScalarGridSpec(
        num_scalar_prefetch=1,
        grid=(b, s // span),
        in_specs=[pl.BlockSpec((1, span, D_MODEL), lambda i, j, st: (i, j, 0)), chunk, chunk],
        out_specs=[pl.BlockSpec((N_EXPERTS, cap, D_MODEL), lambda i, j, st: (0, i, 0)),
                   pl.BlockSpec((N_EXPERTS, cap, LANES), lambda i, j, st: (0, i, 0))],
    )
    return pl.pallas_call(
        functools.partial(_dispatch_kernel, cap=cap),
        grid_spec=grid_spec,
        out_shape=[jax.ShapeDtypeStruct((N_EXPERTS, b * cap, D_MODEL), BF16),
                   jax.ShapeDtypeStruct((N_EXPERTS, b * cap, LANES), F32)],
        compiler_params=_params(2),
        name="dispatch",
    )(starts[:, :, :n_chunks + 1].reshape(-1), h1b, slot, aff)


FF_CHUNK = 512


def _expert_kernel(x_ref, gate_ref, wg_ref, wu_ref, wd_ref, y_ref):
    x = x_ref[0]
    acc = jnp.zeros((x.shape[0], D_MODEL), F32)
    for c in range(D_FF // FF_CHUNK):
        cols = slice(c * FF_CHUNK, (c + 1) * FF_CHUNK)
        g = jnp.dot(x, wg_ref[0, :, cols], preferred_element_type=F32)
        u = jnp.dot(x, wu_ref[0, :, cols], preferred_element_type=F32)
        act = (g * jax.nn.sigmoid(g) * u).astype(BF16)
        acc = acc + jnp.dot(act, wd_ref[0, cols, :], preferred_element_type=F32)
    y_ref[0] = (acc * gate_ref[0][:, :1]).astype(BF16)


def _expert_ffn(xe, gate, w_gate, w_up, w_down, tm):
    e, rows, _ = xe.shape
    tok = lambda w: pl.BlockSpec((1, tm, w), lambda i, j: (i, j, 0))
    wspec = lambda a: pl.BlockSpec((1,) + a.shape[1:], lambda i, j: (i, 0, 0))
    return pl.pallas_call(
        _expert_kernel,
        grid=(e, rows // tm),
        in_specs=[tok(D_MODEL), tok(LANES), wspec(w_gate), wspec(w_up), wspec(w_down)],
        out_specs=tok(D_MODEL),
        out_shape=jax.ShapeDtypeStruct((e, rows, D_MODEL), BF16),
        compiler_params=_params(2),
        name="expert_ffn",
    )(xe, gate, w_gate, w_up, w_down)


COMBINE_WIN = 64
TN_DIMS = (((0,), (0,)), ((), ()))


def _combine_kernel(starts_ref, h1_ref, slot_ref, y_ref, g2_ref, b2_ref, out_ref, acc_ref, *, cap, n_chunks):
    s = pl.program_id(0)
    cur = jnp.minimum(s, pl.num_programs(0) - 2)
    steps_per_seq = n_chunks // MOE_CHUNKS_PER_STEP
    win = COMBINE_WIN

    @pl.when(s == 0)
    def _():
        acc_ref[1] = jnp.zeros(acc_ref.shape[1:], F32)

    out_ref[0] = _layer_norm(DEEPNORM_ALPHA * h1_ref[0] + acc_ref[(s + 1) % 2], g2_ref[...], b2_ref[...])

    wrow = lax.broadcasted_iota(jnp.int32, (win, PREFIX_CHUNK), 0)
    for c in range(MOE_CHUNKS_PER_STEP):
        tokens = slice(c * PREFIX_CHUNK, (c + 1) * PREFIX_CHUNK)
        chunk = (cur % steps_per_seq) * MOE_CHUNKS_PER_STEP + c
        base, n_win = _slot_windows(starts_ref, cur // steps_per_seq, chunk, n_chunks + 1, win)

        def gather(k):
            sels, rows = [], []
            for e in range(N_EXPERTS):
                off, target = _window_rows(base[e], k, wrow, cap, win)
                sels.append(jnp.where(slot_ref[0, e:e + 1, tokens] == target, 1.0, 0.0))
                rows.append(y_ref[e, pl.ds(off, win), :])
            sel = jnp.concatenate(sels, axis=0).astype(BF16)
            return lax.dot_general(sel, jnp.concatenate(rows, axis=0), TN_DIMS, preferred_element_type=F32)

        acc_ref[s % 2, tokens, :] = gather(0)

        def more(k, carry):
            acc_ref[s % 2, tokens, :] += gather(k)
            return carry

        lax.fori_loop(1, n_win, more, 0)


def _combine(h1, slot, starts, y, ln2_g, ln2_b, cap):
    b, s, _ = h1.shape
    n_chunks = s // PREFIX_CHUNK
    span = MOE_CHUNKS_PER_STEP * PREFIX_CHUNK
    per_seq = s // span
    n_steps = b * per_seq
    cur = lambda i: jnp.minimum(i, n_steps - 1)
    prev = lambda i: jnp.maximum(i - 1, 0)
    const = lambda i, st: (0, 0)
    grid_spec = pltpu.PrefetchScalarGridSpec(
        num_scalar_prefetch=1,
        grid=(n_steps + 1,),
        in_specs=[pl.BlockSpec((1, span, D_MODEL), lambda i, st: (prev(i) // per_seq, prev(i) % per_seq, 0)),
                  pl.BlockSpec((1, N_EXPERTS, span), lambda i, st: (cur(i) // per_seq, 0, cur(i) % per_seq)),
                  pl.BlockSpec((N_EXPERTS, cap, D_MODEL), lambda i, st: (0, cur(i) // per_seq, 0)),
                  pl.BlockSpec((1, D_MODEL), const), pl.BlockSpec((1, D_MODEL), const)],
        out_specs=pl.BlockSpec((1, span, D_MODEL), lambda i, st: (prev(i) // per_seq, prev(i) % per_seq, 0)),
        scratch_shapes=[pltpu.VMEM((2, span, D_MODEL), F32)],
    )
    return pl.pallas_call(
        functools.partial(_combine_kernel, cap=cap, n_chunks=n_chunks),
        grid_spec=grid_spec,
        out_shape=jax.ShapeDtypeStruct((b, s, D_MODEL), F32),
        compiler_params=_params(1),
        name="combine_ln2",
    )(starts[:, :, :n_chunks + 1].reshape(-1), h1, slot, y, ln2_g.reshape(1, D_MODEL), ln2_b.reshape(1, D_MODEL))


def kernel(x, ln0_g, ln0_b, w_in, b_gate, qn_g, kn_g, w_branch_a, w_branch_b, w_out, ln1_g, ln1_b, w_router,
           w_gate_e, w_up_e, w_down_e, ln2_g, ln2_b):
    batch, seq, _ = x.shape
    assert w_in.shape[0] == DEPTH
    cap = EC_CAPACITY_FACTOR * seq // N_EXPERTS
    tokens = batch * seq
    x2 = x.reshape(tokens, D_MODEL)
    w_qkv = w_in[0, :, :QKV_W].astype(BF16)
    w_gates = (0.5 * w_in[0, :, QKV_W:]).astype(BF16)

    qkv = _in_projection(x2, ln0_g, ln0_b, w_qkv, qn_g[0], kn_g[0], seq, tm=IN_PROJ_ROW_TILE)
    qa, ka, va = qkv[:3]
    qb, kb, vb = qkv[3:6], qkv[6:9], qkv[9:12]

    oa, (wg_b, wu_b, wd_b) = _attention_a(qa.reshape(batch, seq, A_Q_W), ka.reshape(batch, seq, A_KV_W),
                                          va.reshape(batch, seq, A_KV_W), (w_gate_e[0], w_up_e[0], w_down_e[0]),
                                          tq=ROW_TILE)
    oa = oa.reshape(tokens, A_Q_W)

    obs, lses = [], []
    for gi, (window, dilation) in enumerate(B_GROUPS):
        assert window // (2 * dilation) == BAND_HALF
        o, lse = _dilated_attention(qb[gi], kb[gi], vb[gi], seq, dilation)
        obs.append(o)
        lses.append(lse)

    h1, h1b, logits = _merge(x2, ln0_g, ln0_b, w_gates, 0.5 * b_gate[0], oa, obs, lses,
                             w_branch_a[0].astype(BF16), w_branch_b[0].astype(BF16), (0.5 * w_out[0]).astype(BF16),
                             ln1_g[0], ln1_b[0], w_router[0], seq, tm=ROW_TILE)

    slot, aff, starts = _route(logits, batch, seq, cap)
    xe, gate = _dispatch(h1b.reshape(batch, seq, D_MODEL), slot, aff, starts, cap)
    y = _expert_ffn(xe, gate, wg_b, wu_b, wd_b, tm=min(FFN_ROW_TILE, batch * cap))
    return _combine(h1.reshape(batch, seq, D_MODEL), slot, starts, y, ln2_g[0], ln2_b[0], cap)
```

```python
import functools

import jax
import jax.numpy as jnp
import numpy as np
from jax import lax
from jax.experimental import pallas as pl
from jax.experimental.pallas import tpu as pltpu

F32 = jnp.float32
BF16 = jnp.bfloat16

D_MODEL = 1024
HEAD_DIM = 64
A_Q_HEADS = 8
A_KV_HEADS = 2
A_GROUP = A_Q_HEADS // A_KV_HEADS
B_GROUPS = ((128, 1), (512, 4), (2048, 16))
B_HEADS_PER_GROUP = 4
GRID_W = 64
A_ROPE_THETA = 10000.0
B_ROPE_THETA = 500000.0
B_ROPE_DIMS = HEAD_DIM // 4
N_EXPERTS = 16
EC_CAPACITY_FACTOR = 2
D_FF = 2 * D_MODEL
LN_EPS = 1e-5
QK_NORM_EPS = 1e-6
MASK_VALUE = -1e30
DEPTH = 1
DEEPNORM_ALPHA = (2.0 * DEPTH) ** 0.25
LOG2_E = 1.4426950408889634
Q_SCALE = HEAD_DIM ** -0.5 * LOG2_E

A_Q_W = A_Q_HEADS * HEAD_DIM
A_KV_W = A_KV_HEADS * HEAD_DIM
B_GROUP_W = B_HEADS_PER_GROUP * HEAD_DIM
B_W = B_GROUP_W * len(B_GROUPS)
QKV_W = A_Q_W + 2 * A_KV_W + 3 * B_W
BAND_HALF = 64

LANES = 128
VMEM_LIMIT = 56 * 1024 * 1024
ROW_TILE = 512
IN_PROJ_ROW_TILE = 1024
FFN_ROW_TILE = 1024

NT_DIMS = (((1,), (1,)), ((), ()))


def _params(n_axes):
    return pltpu.CompilerParams(dimension_semantics=("arbitrary",) * n_axes, vmem_limit_bytes=VMEM_LIMIT)


def _layer_norm(x, g, b):
    mu = jnp.mean(x, axis=-1, keepdims=True)
    xc = x - mu
    var = jnp.mean(xc * xc, axis=-1, keepdims=True)
    return xc * lax.rsqrt(var + LN_EPS) * g + b


def _split_bf16(x):
    hi = x.astype(BF16)
    lo = (x - hi.astype(F32)).astype(BF16)
    return hi, lo


def _swap_halves(y, lane, period, half):
    fwd = pltpu.roll(y, LANES - half, 1)
    bwd = pltpu.roll(y, half, 1)
    return jnp.where((lane % period) < half, fwd, bwd)


def _inproj_kernel(x_ref, g0_ref, b0_ref, w_ref, qng_ref, kng_ref, cosa_ref, sina_ref, cosb_ref, sinb_ref,
                   qa_ref, ka_ref, va_ref, qb0_ref, qb1_ref, qb2_ref, kb0_ref, kb1_ref, kb2_ref,
                   vb0_ref, vb1_ref, vb2_ref, h_ref):
    step = pl.program_id(0)

    @pl.when(step == 0)
    def _():
        h_ref[1] = jnp.zeros(h_ref.shape[1:], BF16)

    h = h_ref[(step + 1) % 2]
    lane = lax.broadcasted_iota(jnp.int32, (1, LANES), 1)
    r = lax.broadcasted_iota(jnp.int32, (LANES, LANES), 0)
    c = lax.broadcasted_iota(jnp.int32, (LANES, LANES), 1)
    head_sum = jnp.where((r // HEAD_DIM) == (c // HEAD_DIM), 1.0, 0.0).astype(BF16)
    cosa, sina = cosa_ref[...], sina_ref[...]
    cosb, sinb = cosb_ref[...], sinb_ref[...]

    def proj(off):
        y = jnp.dot(h, w_ref[:, off:off + 2 * LANES], preferred_element_type=F32)
        return y[:, :LANES], y[:, LANES:]

    def norm_rope_a(y, gain, scale):
        hi, lo = _split_bf16(y * y)
        ss = (jnp.dot(hi, head_sum, preferred_element_type=F32)
              + jnp.dot(lo, head_sum, preferred_element_type=F32))
        y = y * lax.rsqrt(ss * (1.0 / HEAD_DIM) + QK_NORM_EPS) * gain
        y = y * cosa + _swap_halves(y, lane, HEAD_DIM // 2, HEAD_DIM // 4) * sina
        return (y * scale).astype(BF16)

    def rope_b(y, scale):
        y = y * cosb + _swap_halves(y, lane, HEAD_DIM, B_ROPE_DIMS // 2) * sinb
        return y * scale

    b_base = A_Q_W + 2 * A_KV_W

    def project_b_qk(gi, q_ref, k_ref):
        off = b_base + gi * B_GROUP_W
        for j, (q, k) in enumerate(zip(proj(off), proj(off + B_W))):
            q_ref[j] = rope_b(q, Q_SCALE)
            k_ref[j] = rope_b(k, 1.0)

    qa_tiles = [y for j in range(A_Q_W // (2 * LANES)) for y in proj(2 * j * LANES)]
    ka, va = proj(A_Q_W)
    h_next = _layer_norm(x_ref[...], g0_ref[...], b0_ref[...]).astype(BF16)
    project_b_qk(0, qb0_ref, kb0_ref)
    for j, y in enumerate(qa_tiles):
        qa_ref[:, j * LANES:(j + 1) * LANES] = norm_rope_a(y, qng_ref[...], Q_SCALE)
    ka_ref[...] = norm_rope_a(ka, kng_ref[...], 1.0)
    va_ref[...] = va.astype(BF16)
    project_b_qk(1, qb1_ref, kb1_ref)
    project_b_qk(2, qb2_ref, kb2_ref)
    for gi, v_ref in enumerate((vb0_ref, vb1_ref, vb2_ref)):
        for j, v in enumerate(proj(b_base + 2 * B_W + gi * B_GROUP_W)):
            v_ref[j] = v

    h_ref[step % 2] = h_next


def _rope_tables(seq):
    f32 = np.float32
    d = np.arange(LANES) % HEAD_DIM
    t = np.arange(seq)
    half_rot = HEAD_DIM // 2
    inv_a = f32(A_ROPE_THETA) ** (-np.arange(0, half_rot, 2, dtype=f32) / f32(half_rot))
    j = d % half_rot
    pos_row = (t // GRID_W).astype(f32)
    pos_col = (t % GRID_W).astype(f32)
    freq_a = inv_a[j % (half_rot // 2)].astype(f32)
    ang_a = np.where((d < half_rot)[None, :], pos_row[:, None], pos_col[:, None]) * freq_a[None, :]
    sign_a = np.where(j < half_rot // 2, -1.0, 1.0).astype(f32)
    cosa = np.cos(ang_a).astype(f32)
    sina = (np.sin(ang_a) * sign_a[None, :]).astype(f32)
    inv_b = f32(B_ROPE_THETA) ** (-np.arange(0, B_ROPE_DIMS, 2, dtype=f32) / f32(B_ROPE_DIMS))
    freq_b = inv_b[d % (B_ROPE_DIMS // 2)].astype(f32)
    ang_b = t.astype(f32)[:, None] * freq_b[None, :]
    rot = (d < B_ROPE_DIMS)[None, :]
    sign_b = np.where(d < B_ROPE_DIMS // 2, -1.0, 1.0).astype(f32)
    cosb = np.where(rot, np.cos(ang_b), 1.0).astype(f32)
    sinb = np.where(rot, np.sin(ang_b) * sign_b[None, :], 0.0).astype(f32)
    return tuple(jnp.asarray(a) for a in (cosa, sina, cosb, sinb))


def _in_projection(x2, ln0_g, ln0_b, w_qkv, qn_g, kn_g, seq, tm):
    m = x2.shape[0]
    cosa, sina, cosb, sinb = _rope_tables(seq)
    n_tiles = m // tm
    tiles_per_seq = seq // tm
    dst = lambda i: jnp.maximum(i - 1, 0)
    row = lambda i: (dst(i), 0)
    const = lambda i: (0, 0)
    table = lambda i: (dst(i) % tiles_per_seq, 0)
    widths = (A_Q_W, A_KV_W, A_KV_W) + (B_GROUP_W,) * 9
    gain_tile = lambda g: jnp.tile(g.reshape(1, HEAD_DIM), (1, LANES // HEAD_DIM))
    return pl.pallas_call(
        _inproj_kernel,
        grid=(n_tiles + 1,),
        in_specs=[
            pl.BlockSpec((tm, D_MODEL), lambda i: (jnp.minimum(i, n_tiles - 1), 0)),
            pl.BlockSpec((1, D_MODEL), const),
            pl.BlockSpec((1, D_MODEL), const),
            pl.BlockSpec((D_MODEL, QKV_W), const),
            pl.BlockSpec((1, LANES), const),
            pl.BlockSpec((1, LANES), const),
            pl.BlockSpec((tm, LANES), table),
            pl.BlockSpec((tm, LANES), table),
            pl.BlockSpec((tm, LANES), table),
            pl.BlockSpec((tm, LANES), table),
        ],
        out_specs=([pl.BlockSpec((tm, w), row) for w in widths[:3]]
                   + [pl.BlockSpec((B_GROUP_W // LANES, tm, LANES), lambda i: (0, dst(i), 0))] * 9),
        out_shape=([jax.ShapeDtypeStruct((m, w), BF16) for w in widths[:3]]
                   + [jax.ShapeDtypeStruct((B_GROUP_W // LANES, m, LANES), F32)] * 9),
        scratch_shapes=[pltpu.VMEM((2, tm, D_MODEL), BF16)],
        compiler_params=_params(1),
        name="in_projection",
    )(x2, ln0_g.reshape(1, D_MODEL), ln0_b.reshape(1, D_MODEL), w_qkv, gain_tile(qn_g), gain_tile(kn_g),
      cosa, sina, cosb, sinb)


A_HEADS_PER_UNIT = 2


def _attn_a_kernel(q_ref, k_ref, v_ref, wg_ref, wu_ref, wd_ref, o_ref, wgb_ref, wub_ref, wdb_ref):
    wgb_ref[...] = wg_ref[...].astype(BF16)
    wub_ref[...] = wu_ref[...].astype(BF16)
    wdb_ref[...] = wd_ref[...].astype(BF16)
    tq = q_ref.shape[1]
    n_units = A_Q_HEADS // A_HEADS_PER_UNIT

    def scores(u):
        heads = range(u * A_HEADS_PER_UNIT, (u + 1) * A_HEADS_PER_UNIT)
        kh = heads[0] // A_GROUP
        q = jnp.concatenate([q_ref[0, :, h * HEAD_DIM:(h + 1) * HEAD_DIM] for h in heads], axis=0)
        k = k_ref[0, :, kh * HEAD_DIM:(kh + 1) * HEAD_DIM]
        return lax.dot_general(q, k, NT_DIMS, preferred_element_type=F32)

    outs = []
    v_ones = []
    for kh in range(A_KV_HEADS):
        v = v_ref[0, :, kh * HEAD_DIM:(kh + 1) * HEAD_DIM]
        v_ones.append(jnp.concatenate([v, jnp.ones_like(v)], axis=1))
    s = scores(0)
    for u in range(n_units):
        s_next = scores(u + 1) if u + 1 < n_units else None
        p = jnp.exp2(s - jnp.max(s, axis=-1, keepdims=True)).astype(BF16)
        o = jnp.dot(p, v_ones[u * A_HEADS_PER_UNIT // A_GROUP], preferred_element_type=F32)
        o = o[:, :HEAD_DIM] / o[:, HEAD_DIM:HEAD_DIM + 1]
        outs.extend(o[i * tq:(i + 1) * tq] for i in range(A_HEADS_PER_UNIT))
        s = s_next
    o_ref[0] = jnp.concatenate(outs, axis=1).astype(BF16)


def _attention_a(qa, ka, va, expert_weights, tq):
    b, s, _ = qa.shape
    steps = b * (s // tq)
    flat = [w.reshape(-1, w.shape[-1]) for w in expert_weights]
    wslice = lambda w: pl.BlockSpec((w.shape[0] // steps, w.shape[1]), lambda i, j: (i * (s // tq) + j, 0))
    o, *rounded = pl.pallas_call(
        _attn_a_kernel,
        grid=(b, s // tq),
        in_specs=[
            pl.BlockSpec((1, tq, A_Q_W), lambda i, j: (i, j, 0)),
            pl.BlockSpec((1, s, A_KV_W), lambda i, j: (i, 0, 0)),
            pl.BlockSpec((1, s, A_KV_W), lambda i, j: (i, 0, 0)),
            *map(wslice, flat),
        ],
        out_specs=[pl.BlockSpec((1, tq, A_Q_W), lambda i, j: (i, j, 0)), *map(wslice, flat)],
        out_shape=[jax.ShapeDtypeStruct((b, s, A_Q_W), BF16),
                   *[jax.ShapeDtypeStruct(w.shape, BF16) for w in flat]],
        compiler_params=_params(2),
        name="attention_a",
    )(qa, ka, va, *flat)
    return o, [r.reshape(w.shape) for r, w in zip(rounded, expert_weights)]


BAND_QB = 128
BAND_UNITS_PER_STEP = 16
BAND_STEPS = 1


def _band_bias(n):
    win = min(2 * BAND_QB, n)
    rows = B_HEADS_PER_GROUP * BAND_QB
    rel = (np.arange(rows)[:, None] % BAND_QB) - np.arange(win)[None, :]
    shifts = np.arange(2 * BAND_QB // BAND_HALF - 1) * BAND_HALF
    inside = np.abs(rel[None] + shifts[:, None, None]) <= BAND_HALF
    return jnp.asarray(np.where(inside, 0.0, MASK_VALUE).astype(np.float32))


def _dilated_kernel(q_ref, k_ref, v_ref, bias_ref, o_ref, lse_ref, *, n, dilation):
    win = min(2 * BAND_QB, n)
    n_blk = n // BAND_QB
    lane_head = lax.broadcasted_iota(jnp.int32, (1, B_GROUP_W), 1) // HEAD_DIM

    planes = B_GROUP_W // LANES

    def class_rows(r, start, size):
        if dilation == 1:
            return pl.ds(pl.multiple_of(start, BAND_HALF), size)
        return pl.ds(r + start * dilation, size, stride=dilation)

    def load(ref, rows_):
        return jnp.concatenate([ref[j, rows_, :] for j in range(planes)], axis=1)

    def scores(i):
        unit = pl.program_id(1) * BAND_UNITS_PER_STEP + i
        r = unit // n_blk
        qs = (unit % n_blk) * BAND_QB
        ks = jnp.clip(qs - BAND_HALF, 0, n - win)
        q = load(q_ref, class_rows(r, qs, BAND_QB))
        k = load(k_ref, class_rows(r, ks, win)).astype(BF16)
        q_stack = jnp.concatenate([jnp.where(lane_head == h, q, 0.0) for h in range(B_HEADS_PER_GROUP)], axis=0)
        s = lax.dot_general(q_stack.astype(BF16), k, NT_DIMS, preferred_element_type=F32)
        return r, qs, ks, s + bias_ref[(qs - ks) // BAND_HALF]

    nxt = scores(0)
    for i in range(BAND_UNITS_PER_STEP):
        r, qs, ks, s = nxt
        if i + 1 < BAND_UNITS_PER_STEP:
            nxt = scores(i + 1)
        v = load(v_ref, class_rows(r, ks, win)).astype(BF16)
        mx = jnp.max(s, axis=-1, keepdims=True)
        p = jnp.exp2(s - mx)
        denom = jnp.sum(p, axis=-1, keepdims=True)
        pv = jnp.dot(p.astype(BF16), v, preferred_element_type=F32) / denom
        lse = mx + jnp.log2(denom)
        o = jnp.zeros((BAND_QB, B_GROUP_W), F32)
        l = jnp.zeros((BAND_QB, B_GROUP_W), F32)
        for h in range(B_HEADS_PER_GROUP):
            blk = slice(h * BAND_QB, (h + 1) * BAND_QB)
            o = jnp.where(lane_head == h, pv[blk], o)
            l = jnp.where(lane_head == h, lse[blk], l)
        for j in range(planes):
            o_ref[j, class_rows(r, qs, BAND_QB), :] = o[:, j * LANES:(j + 1) * LANES]
            lse_ref[j, class_rows(r, qs, BAND_QB), :] = l[:, j * LANES:(j + 1) * LANES]


def _dilated_attention(q, k, v, seq, dilation):
    planes, tokens, _ = q.shape
    n = seq // dilation
    assert dilation * (n // BAND_QB) == BAND_UNITS_PER_STEP * BAND_STEPS
    blk = pl.BlockSpec((planes, seq, LANES), lambda i, j: (0, i, 0))
    bias = _band_bias(n)
    return pl.pallas_call(
        functools.partial(_dilated_kernel, n=n, dilation=dilation),
        grid=(tokens // seq, BAND_STEPS),
        in_specs=[blk, blk, blk, pl.BlockSpec(bias.shape, lambda i, j: (0, 0, 0))],
        out_specs=[blk, blk],
        out_shape=[jax.ShapeDtypeStruct(q.shape, F32)] * 2,
        compiler_params=_params(2),
        name=f"dilated_attention_d{dilation}",
    )(q, k, v, bias)


def _merge_kernel(x_ref, g0_ref, b0_ref, wg_ref, bg_ref, oa_ref, o0_ref, o1_ref, o2_ref, l0_ref, l1_ref, l2_ref,
                  wa_ref, wb_ref, wo_ref, g1_ref, b1_ref, wr_hi_ref, wr_lo_ref,
                  h1_ref, h1b_ref, logit_ref, z_ref):
    i = pl.program_id(0)

    @pl.when(i == 0)
    def _():
        z_ref[1] = jnp.zeros(z_ref.shape[1:], F32)

    planes = lambda ref: jnp.concatenate([ref[j] for j in range(ref.shape[0])], axis=1)
    ya = jnp.dot(oa_ref[...], wa_ref[...], preferred_element_type=F32)
    h0 = _layer_norm(x_ref[...], g0_ref[...], b0_ref[...])
    gate_t = jnp.tanh(jnp.dot(h0.astype(BF16), wg_ref[...], preferred_element_type=F32) + bg_ref[...])

    h1 = _layer_norm(z_ref[(i + 1) % 2], g1_ref[...], b1_ref[...])
    h1_ref[...] = h1
    h1b_ref[...] = h1.astype(BF16)
    hi, lo = _split_bf16(h1)
    logit_ref[0] = (lax.dot_general(wr_hi_ref[...], hi, NT_DIMS, preferred_element_type=F32)
                    + lax.dot_general(wr_hi_ref[...], lo, NT_DIMS, preferred_element_type=F32)
                    + lax.dot_general(wr_lo_ref[...], hi, NT_DIMS, preferred_element_type=F32))

    l0, l1, l2 = planes(l0_ref), planes(l1_ref), planes(l2_ref)
    mx = jnp.maximum(jnp.maximum(l0, l1), l2)
    e0, e1, e2 = jnp.exp2(l0 - mx), jnp.exp2(l1 - mx), jnp.exp2(l2 - mx)
    ob = (e0 * planes(o0_ref) + e1 * planes(o1_ref) + e2 * planes(o2_ref)) / (e0 + e1 + e2)
    yb = jnp.dot(ob.astype(BF16), wb_ref[...], preferred_element_type=F32)
    merged = (ya + yb) + (gate_t[:, :D_MODEL] * ya + gate_t[:, D_MODEL:] * yb)
    mix = jnp.dot(merged.astype(BF16), wo_ref[...], preferred_element_type=F32)
    z_ref[i % 2] = DEEPNORM_ALPHA * h0 + mix


def _merge(x2, ln0_g, ln0_b, w_gates, b_gate, oa, obs, lses, w_a, w_b, w_o, ln1_g, ln1_b, w_router, seq, tm):
    m = x2.shape[0]
    n_tiles = m // tm
    tiles_per_seq = seq // tm
    src = lambda i: jnp.minimum(i, n_tiles - 1)
    dst = lambda i: jnp.maximum(i - 1, 0)
    const = lambda i: (0, 0)
    wr_t = w_router.T
    wr_hi = wr_t.astype(BF16)
    wr_lo = (wr_t - wr_hi.astype(F32)).astype(BF16)
    vec = lambda v: v.reshape(1, -1)
    full = lambda a: pl.BlockSpec(a.shape, const)
    tile_in = lambda w: pl.BlockSpec((tm, w), lambda i: (src(i), 0))
    tile_out = lambda w: pl.BlockSpec((tm, w), lambda i: (dst(i), 0))
    plane_in = pl.BlockSpec((B_GROUP_W // LANES, tm, LANES), lambda i: (0, src(i), 0))
    args = [x2, vec(ln0_g), vec(ln0_b), w_gates, vec(b_gate), oa, *obs, *lses, w_a, w_b, w_o,
            vec(ln1_g), vec(ln1_b), wr_hi, wr_lo]
    in_specs = [tile_in(D_MODEL), full(args[1]), full(args[2]), full(w_gates), full(args[4]), tile_in(A_Q_W),
                *[plane_in] * 6, full(w_a), full(w_b), full(w_o), full(args[15]), full(args[16]),
                full(wr_hi), full(wr_lo)]
    return pl.pallas_call(
        _merge_kernel,
        grid=(n_tiles + 1,),
        in_specs=in_specs,
        out_specs=[tile_out(D_MODEL), tile_out(D_MODEL),
                   pl.BlockSpec((1, N_EXPERTS, tm),
                                lambda i: (dst(i) // tiles_per_seq, 0, dst(i) % tiles_per_seq))],
        out_shape=[jax.ShapeDtypeStruct((m, D_MODEL), F32), jax.ShapeDtypeStruct((m, D_MODEL), BF16),
                   jax.ShapeDtypeStruct((m // seq, N_EXPERTS, seq), F32)],
        scratch_shapes=[pltpu.VMEM((2, tm, D_MODEL), F32)],
        compiler_params=_params(1),
        name="merge_ln1_router",
    )(*args)


PREFIX_CHUNK = 256


def _prefix_count(mask, tri):
    ones = jnp.where(mask, 1.0, 0.0)
    carry = jnp.zeros((mask.shape[0], 1), F32)
    outs, bounds = [], [carry]
    for j in range(mask.shape[1] // PREFIX_CHUNK):
        chunk = ones[:, j * PREFIX_CHUNK:(j + 1) * PREFIX_CHUNK]
        outs.append(jnp.dot(chunk.astype(BF16), tri, preferred_element_type=F32) + carry)
        carry = carry + jnp.sum(chunk, axis=1, keepdims=True)
        bounds.append(carry)
    return jnp.concatenate(outs, axis=1), bounds


THRESHOLD_BITS = 31
THRESHOLD_REFINE = 16


def _route_kernel(logit_ref, slot_ref, aff_ref, starts_ref, *, cap):
    n_seq = logit_ref.shape[0]
    affs = []
    for b in range(n_seq):
        lg = logit_ref[b]
        ex = jnp.exp(lg - jnp.max(lg, axis=0, keepdims=True))
        affs.append(ex / jnp.sum(ex, axis=0, keepdims=True))
    aff = jnp.concatenate(affs, axis=0)

    def count_ge(t):
        return jnp.sum(jnp.where(aff >= t, 1.0, 0.0), axis=1, keepdims=True)

    thr = jnp.zeros((aff.shape[0], 1), jnp.int32)
    for bit in range(THRESHOLD_BITS - 1, -1, -1):
        cand = thr | (1 << bit)
        thr = jnp.where(count_ge(pltpu.bitcast(cand, F32)) >= cap, cand, thr)
    lo = pltpu.bitcast(thr, F32)
    hi = pltpu.bitcast(thr + 1, F32)
    for _ in range(THRESHOLD_REFINE):
        mid = 0.5 * (lo + hi)
        take = count_ge(mid) >= cap
        lo = jnp.where(take, mid, lo)
        hi = jnp.where(take, hi, mid)
    above = aff >= hi
    tied = (aff >= lo) & (aff < hi)
    r = lax.broadcasted_iota(jnp.int32, (PREFIX_CHUNK, PREFIX_CHUNK), 0)
    c = lax.broadcasted_iota(jnp.int32, (PREFIX_CHUNK, PREFIX_CHUNK), 1)
    tri = jnp.where(r < c, 1.0, 0.0).astype(BF16)
    need = cap - jnp.sum(jnp.where(above, 1.0, 0.0), axis=1, keepdims=True)
    sel = above | (tied & (_prefix_count(tied, tri)[0] < need))
    rank, bounds = _prefix_count(sel, tri)
    slot = jnp.where(sel, rank, -1.0).astype(jnp.int32)
    lane = lax.broadcasted_iota(jnp.int32, (1, LANES), 1)
    starts = jnp.zeros((aff.shape[0], LANES), F32)
    for j, count in enumerate(bounds):
        starts = jnp.where(lane == j, count, starts)
    for b in range(n_seq):
        rows = slice(b * N_EXPERTS, (b + 1) * N_EXPERTS)
        slot_ref[b] = slot[rows]
        aff_ref[b] = aff[rows]
        starts_ref[b] = starts[rows].astype(jnp.int32)


ROUTE_SEQS_PER_STEP = 4


def _route(logits, batch, seq, cap):
    nb = ROUTE_SEQS_PER_STEP if batch % ROUTE_SEQS_PER_STEP == 0 else 1
    blk = pl.BlockSpec((nb, N_EXPERTS, seq), lambda i: (i, 0, 0))
    return pl.pallas_call(
        functools.partial(_route_kernel, cap=cap),
        grid=(batch // nb,),
        in_specs=[blk],
        out_specs=[blk, blk, pl.BlockSpec((nb, N_EXPERTS, LANES), lambda i: (i, 0, 0))],
        out_shape=[jax.ShapeDtypeStruct((batch, N_EXPERTS, seq), jnp.int32),
                   jax.ShapeDtypeStruct((batch, N_EXPERTS, seq), F32),
                   jax.ShapeDtypeStruct((batch, N_EXPERTS, LANES), jnp.int32)],
        compiler_params=_params(1),
        name="route",
    )(logits)


BF16_SUBLANES = 16
DISPATCH_WIN = 64
MOE_CHUNKS_PER_STEP = 4


def _slot_windows(starts_ref, b, j, n_bounds, win):
    base, n_win = [], 0
    for e in range(N_EXPERTS):
        at = (b * N_EXPERTS + e) * n_bounds + j
        first = (starts_ref[at] // BF16_SUBLANES) * BF16_SUBLANES
        base.append(first)
        n_win = jnp.maximum(n_win, (starts_ref[at + 1] - first + win - 1) // win)
    return base, n_win


def _window_rows(base_e, k, wrow, cap, win):
    lower = base_e + k * win
    off = pl.multiple_of(jnp.minimum(lower, cap - win), BF16_SUBLANES)
    return off, jnp.where(wrow + off >= lower, wrow + off, -2)


def _dispatch_kernel(starts_ref, h_ref, slot_ref, aff_ref, xe_ref, gate_ref, *, cap):
    b, step = pl.program_id(0), pl.program_id(1)
    win = DISPATCH_WIN
    n_bounds = pl.num_programs(1) * MOE_CHUNKS_PER_STEP + 1

    @pl.when(step == 0)
    def _():
        xe_ref[...] = jnp.zeros_like(xe_ref)
        gate_ref[...] = jnp.zeros_like(gate_ref)

    wrow = lax.broadcasted_iota(jnp.int32, (win, PREFIX_CHUNK), 0)
    for c in range(MOE_CHUNKS_PER_STEP):
        tokens = slice(c * PREFIX_CHUNK, (c + 1) * PREFIX_CHUNK)
        base, n_win = _slot_windows(starts_ref, b, step * MOE_CHUNKS_PER_STEP + c, n_bounds, win)
        h = h_ref[0, tokens, :]

        def window(k):
            offs, onehots = [], []
            for e in range(N_EXPERTS):
                off, target = _window_rows(base[e], k, wrow, cap, win)
                offs.append(off)
                onehots.append(slot_ref[0, e:e + 1, tokens] == target)
            sel = jnp.concatenate([jnp.where(o, 1.0, 0.0) for o in onehots], axis=0).astype(BF16)
            rows = jnp.dot(sel, h, preferred_element_type=F32).astype(BF16)
            for e in range(N_EXPERTS):
                at = pl.ds(offs[e], win)
                xe_ref[e, at, :] += rows[e * win:(e + 1) * win]
                gate = jnp.sum(jnp.where(onehots[e], aff_ref[0, e:e + 1, tokens], 0.0), axis=1, keepdims=True)
                gate_ref[e, at, :] += jnp.broadcast_to(gate, (win, LANES))

        window(0)
        lax.fori_loop(1, n_win, lambda k, carry: (window(k), carry)[1], 0)


def _dispatch(h1b, slot, aff, starts, cap):
    b, s, _ = h1b.shape
    n_chunks = s // PREFIX_CHUNK
    span = MOE_CHUNKS_PER_STEP * PREFIX_CHUNK
    chunk = pl.BlockSpec((1, N_EXPERTS, span), lambda i, j, st: (i, 0, j))
    grid_spec = pltpu.PrefetchScalarGridSpec(
        num_scalar_prefetch=1,
        grid=(b, s // span),
        in_specs=[pl.BlockSpec((1, span, D_MODEL), lambda i, j, st: (i, j, 0)), chunk, chunk],
        out_specs=[pl.BlockSpec((N_EXPERTS, cap, D_MODEL), lambda i, j, st: (0, i, 0)),
                   pl.BlockSpec((N_EXPERTS, cap, LANES), lambda i, j, st: (0, i, 0))],
    )
    return pl.pallas_call(
        functools.partial(_dispatch_kernel, cap=cap),
        grid_spec=grid_spec,
        out_shape=[jax.ShapeDtypeStruct((N_EXPERTS, b * cap, D_MODEL), BF16),
                   jax.ShapeDtypeStruct((N_EXPERTS, b * cap, LANES), F32)],
        compiler_params=_params(2),
        name="dispatch",
    )(starts[:, :, :n_chunks + 1].reshape(-1), h1b, slot, aff)


FF_CHUNK = 512


def _expert_kernel(x_ref, gate_ref, wg_ref, wu_ref, wd_ref, y_ref):
    x = x_ref[0]
    acc = jnp.zeros((x.shape[0], D_MODEL), F32)
    for c in range(D_FF // FF_CHUNK):
        cols = slice(c * FF_CHUNK, (c + 1) * FF_CHUNK)
        g = jnp.dot(x, wg_ref[0, :, cols], preferred_element_type=F32)
        u = jnp.dot(x, wu_ref[0, :, cols], preferred_element_type=F32)
        act = (g * jax.nn.sigmoid(g) * u).astype(BF16)
        acc = acc + jnp.dot(act, wd_ref[0, cols, :], preferred_element_type=F32)
    y_ref[0] = (acc * gate_ref[0][:, :1]).astype(BF16)


def _expert_ffn(xe, gate, w_gate, w_up, w_down, tm):
    e, rows, _ = xe.shape
    tok = lambda w: pl.BlockSpec((1, tm, w), lambda i, j: (i, j, 0))
    wspec = lambda a: pl.BlockSpec((1,) + a.shape[1:], lambda i, j: (i, 0, 0))
    return pl.pallas_call(
        _expert_kernel,
        grid=(e, rows // tm),
        in_specs=[tok(D_MODEL), tok(LANES), wspec(w_gate), wspec(w_up), wspec(w_down)],
        out_specs=tok(D_MODEL),
        out_shape=jax.ShapeDtypeStruct((e, rows, D_MODEL), BF16),
        compiler_params=_params(2),
        name="expert_ffn",
    )(xe, gate, w_gate, w_up, w_down)


COMBINE_WIN = 64
TN_DIMS = (((0,), (0,)), ((), ()))


def _combine_kernel(starts_ref, h1_ref, slot_ref, y_ref, g2_ref, b2_ref, out_ref, acc_ref, *, cap, n_chunks):
    s = pl.program_id(0)
    cur = jnp.minimum(s, pl.num_programs(0) - 2)
    steps_per_seq = n_chunks // MOE_CHUNKS_PER_STEP
    win = COMBINE_WIN

    @pl.when(s == 0)
    def _():
        acc_ref[1] = jnp.zeros(acc_ref.shape[1:], F32)

    out_ref[0] = _layer_norm(DEEPNORM_ALPHA * h1_ref[0] + acc_ref[(s + 1) % 2], g2_ref[...], b2_ref[...])

    wrow = lax.broadcasted_iota(jnp.int32, (win, PREFIX_CHUNK), 0)
    for c in range(MOE_CHUNKS_PER_STEP):
        tokens = slice(c * PREFIX_CHUNK, (c + 1) * PREFIX_CHUNK)
        chunk = (cur % steps_per_seq) * MOE_CHUNKS_PER_STEP + c
        base, n_win = _slot_windows(starts_ref, cur // steps_per_seq, chunk, n_chunks + 1, win)

        def gather(k):
            sels, rows = [], []
            for e in range(N_EXPERTS):
                off, target = _window_rows(base[e], k, wrow, cap, win)
                sels.append(jnp.where(slot_ref[0, e:e + 1, tokens] == target, 1.0, 0.0))
                rows.append(y_ref[e, pl.ds(off, win), :])
            sel = jnp.concatenate(sels, axis=0).astype(BF16)
            return lax.dot_general(sel, jnp.concatenate(rows, axis=0), TN_DIMS, preferred_element_type=F32)

        acc_ref[s % 2, tokens, :] = gather(0)

        def more(k, carry):
            acc_ref[s % 2, tokens, :] += gather(k)
            return carry

        lax.fori_loop(1, n_win, more, 0)


def _combine(h1, slot, starts, y, ln2_g, ln2_b, cap):
    b, s, _ = h1.shape
    n_chunks = s // PREFIX_CHUNK
    span = MOE_CHUNKS_PER_STEP * PREFIX_CHUNK
    per_seq = s // span
    n_steps = b * per_seq
    cur = lambda i: jnp.minimum(i, n_steps - 1)
    prev = lambda i: jnp.maximum(i - 1, 0)
    const = lambda i, st: (0, 0)
    grid_spec = pltpu.PrefetchScalarGridSpec(
        num_scalar_prefetch=1,
        grid=(n_steps + 1,),
        in_specs=[pl.BlockSpec((1, span, D_MODEL), lambda i, st: (prev(i) // per_seq, prev(i) % per_seq, 0)),
                  pl.BlockSpec((1, N_EXPERTS, span), lambda i, st: (cur(i) // per_seq, 0, cur(i) % per_seq)),
                  pl.BlockSpec((N_EXPERTS, cap, D_MODEL), lambda i, st: (0, cur(i) // per_seq, 0)),
                  pl.BlockSpec((1, D_MODEL), const), pl.BlockSpec((1, D_MODEL), const)],
        out_specs=pl.BlockSpec((1, span, D_MODEL), lambda i, st: (prev(i) // per_seq, prev(i) % per_seq, 0)),
        scratch_shapes=[pltpu.VMEM((2, span, D_MODEL), F32)],
    )
    return pl.pallas_call(
        functools.partial(_combine_kernel, cap=cap, n_chunks=n_chunks),
        grid_spec=grid_spec,
        out_shape=jax.ShapeDtypeStruct((b, s, D_MODEL), F32),
        compiler_params=_params(1),
        name="combine_ln2",
    )(starts[:, :, :n_chunks + 1].reshape(-1), h1, slot, y, ln2_g.reshape(1, D_MODEL), ln2_b.reshape(1, D_MODEL))


def kernel(x, ln0_g, ln0_b, w_in, b_gate, qn_g, kn_g, w_branch_a, w_branch_b, w_out, ln1_g, ln1_b, w_router,
           w_gate_e, w_up_e, w_down_e, ln2_g, ln2_b):
    batch, seq, _ = x.shape
    assert w_in.shape[0] == DEPTH
    cap = EC_CAPACITY_FACTOR * seq // N_EXPERTS
    tokens = batch * seq
    x2 = x.reshape(tokens, D_MODEL)
    w_qkv = w_in[0, :, :QKV_W].astype(BF16)
    w_gates = (0.5 * w_in[0, :, QKV_W:]).astype(BF16)

    qkv = _in_projection(x2, ln0_g, ln0_b, w_qkv, qn_g[0], kn_g[0], seq, tm=IN_PROJ_ROW_TILE)
    qa, ka, va = qkv[:3]
    qb, kb, vb = qkv[3:6], qkv[6:9], qkv[9:12]

    oa, (wg_b, wu_b, wd_b) = _attention_a(qa.reshape(batch, seq, A_Q_W), ka.reshape(batch, seq, A_KV_W),
                                          va.reshape(batch, seq, A_KV_W), (w_gate_e[0], w_up_e[0], w_down_e[0]),
                                          tq=ROW_TILE)
    oa = oa.reshape(tokens, A_Q_W)

    obs, lses = [], []
    for gi, (window, dilation) in enumerate(B_GROUPS):
        assert window // (2 * dilation) == BAND_HALF
        o, lse = _dilated_attention(qb[gi], kb[gi], vb[gi], seq, dilation)
        obs.append(o)
        lses.append(lse)

    h1, h1b, logits = _merge(x2, ln0_g, ln0_b, w_gates, 0.5 * b_gate[0], oa, obs, lses,
                             w_branch_a[0].astype(BF16), w_branch_b[0].astype(BF16), (0.5 * w_out[0]).astype(BF16),
                             ln1_g[0], ln1_b[0], w_router[0], seq, tm=ROW_TILE)

    slot, aff, starts = _route(logits, batch, seq, cap)
    xe, gate = _dispatch(h1b.reshape(batch, seq, D_MODEL), slot, aff, starts, cap)
    y = _expert_ffn(xe, gate, wg_b, wu_b, wd_b, tm=min(FFN_ROW_TILE, batch * cap))
    return _combine(h1.reshape(batch, seq, D_MODEL), slot, starts, y, ln2_g[0], ln2_b[0], cap)
```

```python
import functools

import jax
import jax.numpy as jnp
import numpy as np
from jax import lax
from jax.experimental import pallas as pl
from jax.experimental.pallas import tpu as pltpu

F32 = jnp.float32
BF16 = jnp.bfloat16

D_MODEL = 1024
HEAD_DIM = 64
A_Q_HEADS = 8
A_KV_HEADS = 2
A_GROUP = A_Q_HEADS // A_KV_HEADS
B_GROUPS = ((128, 1), (512, 4), (2048, 16))
B_HEADS_PER_GROUP = 4
GRID_W = 64
A_ROPE_THETA = 10000.0
B_ROPE_THETA = 500000.0
B_ROPE_DIMS = HEAD_DIM // 4
N_EXPERTS = 16
EC_CAPACITY_FACTOR = 2
D_FF = 2 * D_MODEL
LN_EPS = 1e-5
QK_NORM_EPS = 1e-6
MASK_VALUE = -1e30
DEPTH = 1
DEEPNORM_ALPHA = (2.0 * DEPTH) ** 0.25
LOG2_E = 1.4426950408889634
Q_SCALE = HEAD_DIM ** -0.5 * LOG2_E

A_Q_W = A_Q_HEADS * HEAD_DIM
A_KV_W = A_KV_HEADS * HEAD_DIM
B_GROUP_W = B_HEADS_PER_GROUP * HEAD_DIM
B_W = B_GROUP_W * len(B_GROUPS)
QKV_W = A_Q_W + 2 * A_KV_W + 3 * B_W
BAND_HALF = 64

LANES = 128
VMEM_LIMIT = 56 * 1024 * 1024
ROW_TILE = 512
IN_PROJ_ROW_TILE = 1024
FFN_ROW_TILE = 1024

NT_DIMS = (((1,), (1,)), ((), ()))


def _params(n_axes):
    return pltpu.CompilerParams(dimension_semantics=("arbitrary",) * n_axes, vmem_limit_bytes=VMEM_LIMIT)


def _layer_norm(x, g, b):
    mu = jnp.mean(x, axis=-1, keepdims=True)
    xc = x - mu
    var = jnp.mean(xc * xc, axis=-1, keepdims=True)
    return xc * lax.rsqrt(var + LN_EPS) * g + b


def _split_bf16(x):
    hi = x.astype(BF16)
    lo = (x - hi.astype(F32)).astype(BF16)
    return hi, lo


def _swap_halves(y, lane, period, half):
    fwd = pltpu.roll(y, LANES - half, 1)
    bwd = pltpu.roll(y, half, 1)
    return jnp.where((lane % period) < half, fwd, bwd)


def _inproj_kernel(x_ref, g0_ref, b0_ref, w_ref, qng_ref, kng_ref, cosa_ref, sina_ref, cosb_ref, sinb_ref,
                   qa_ref, ka_ref, va_ref, qb0_ref, qb1_ref, qb2_ref, kb0_ref, kb1_ref, kb2_ref,
                   vb0_ref, vb1_ref, vb2_ref, h_ref):
    step = pl.program_id(0)

    @pl.when(step == 0)
    def _():
        h_ref[1] = jnp.zeros(h_ref.shape[1:], BF16)

    h = h_ref[(step + 1) % 2]
    lane = lax.broadcasted_iota(jnp.int32, (1, LANES), 1)
    r = lax.broadcasted_iota(jnp.int32, (LANES, LANES), 0)
    c = lax.broadcasted_iota(jnp.int32, (LANES, LANES), 1)
    head_sum = jnp.where((r // HEAD_DIM) == (c // HEAD_DIM), 1.0, 0.0).astype(BF16)
    cosa, sina = cosa_ref[...], sina_ref[...]
    cosb, sinb = cosb_ref[...], sinb_ref[...]

    def proj(off):
        y = jnp.dot(h, w_ref[:, off:off + 2 * LANES], preferred_element_type=F32)
        return y[:, :LANES], y[:, LANES:]

    def norm_rope_a(y, gain, scale):
        hi, lo = _split_bf16(y * y)
        ss = (jnp.dot(hi, head_sum, preferred_element_type=F32)
              + jnp.dot(lo, head_sum, preferred_element_type=F32))
        y = y * lax.rsqrt(ss * (1.0 / HEAD_DIM) + QK_NORM_EPS) * gain
        y = y * cosa + _swap_halves(y, lane, HEAD_DIM // 2, HEAD_DIM // 4) * sina
        return (y * scale).astype(BF16)

    def rope_b(y, scale):
        y = y * cosb + _swap_halves(y, lane, HEAD_DIM, B_ROPE_DIMS // 2) * sinb
        return y * scale

    b_base = A_Q_W + 2 * A_KV_W

    def project_b_qk(gi, q_ref, k_ref):
        off = b_base + gi * B_GROUP_W
        for j, (q, k) in enumerate(zip(proj(off), proj(off + B_W))):
            q_ref[j] = rope_b(q, Q_SCALE)
            k_ref[j] = rope_b(k, 1.0)

    qa_tiles = [y for j in range(A_Q_W // (2 * LANES)) for y in proj(2 * j * LANES)]
    ka, va = proj(A_Q_W)
    h_next = _layer_norm(x_ref[...], g0_ref[...], b0_ref[...]).astype(BF16)
    project_b_qk(0, qb0_ref, kb0_ref)
    for j, y in enumerate(qa_tiles):
        qa_ref[:, j * LANES:(j + 1) * LANES] = norm_rope_a(y, qng_ref[...], Q_SCALE)
    ka_ref[...] = norm_rope_a(ka, kng_ref[...], 1.0)
    va_ref[...] = va.astype(BF16)
    project_b_qk(1, qb1_ref, kb1_ref)
    project_b_qk(2, qb2_ref, kb2_ref)
    for gi, v_ref in enumerate((vb0_ref, vb1_ref, vb2_ref)):
        for j, v in enumerate(proj(b_base + 2 * B_W + gi * B_GROUP_W)):
            v_ref[j] = v

    h_ref[step % 2] = h_next


def _rope_tables(seq):
    f32 = np.float32
    d = np.arange(LANES) % HEAD_DIM
    t = np.arange(seq)
    half_rot = HEAD_DIM // 2
    inv_a = f32(A_ROPE_THETA) ** (-np.arange(0, half_rot, 2, dtype=f32) / f32(half_rot))
    j = d % half_rot
    pos_row = (t // GRID_W).astype(f32)
    pos_col = (t % GRID_W).astype(f32)
    freq_a = inv_a[j % (half_rot // 2)].astype(f32)
    ang_a = np.where((d < half_rot)[None, :], pos_row[:, None], pos_col[:, None]) * freq_a[None, :]
    sign_a = np.where(j < half_rot // 2, -1.0, 1.0).astype(f32)
    cosa = np.cos(ang_a).astype(f32)
    sina = (np.sin(ang_a) * sign_a[None, :]).astype(f32)
    inv_b = f32(B_ROPE_THETA) ** (-np.arange(0, B_ROPE_DIMS, 2, dtype=f32) / f32(B_ROPE_DIMS))
    freq_b = inv_b[d % (B_ROPE_DIMS // 2)].astype(f32)
    ang_b = t.astype(f32)[:, None] * freq_b[None, :]
    rot = (d < B_ROPE_DIMS)[None, :]
    sign_b = np.where(d < B_ROPE_DIMS // 2, -1.0, 1.0).astype(f32)
    cosb = np.where(rot, np.cos(ang_b), 1.0).astype(f32)
    sinb = np.where(rot, np.sin(ang_b) * sign_b[None, :], 0.0).astype(f32)
    return tuple(jnp.asarray(a) for a in (cosa, sina, cosb, sinb))


def _in_projection(x2, ln0_g, ln0_b, w_qkv, qn_g, kn_g, seq, tm):
    m = x2.shape[0]
    cosa, sina, cosb, sinb = _rope_tables(seq)
    n_tiles = m // tm
    tiles_per_seq = seq // tm
    dst = lambda i: jnp.maximum(i - 1, 0)
    row = lambda i: (dst(i), 0)
    const = lambda i: (0, 0)
    table = lambda i: (dst(i) % tiles_per_seq, 0)
    widths = (A_Q_W, A_KV_W, A_KV_W) + (B_GROUP_W,) * 9
    gain_tile = lambda g: jnp.tile(g.reshape(1, HEAD_DIM), (1, LANES // HEAD_DIM))
    return pl.pallas_call(
        _inproj_kernel,
        grid=(n_tiles + 1,),
        in_specs=[
            pl.BlockSpec((tm, D_MODEL), lambda i: (jnp.minimum(i, n_tiles - 1), 0)),
            pl.BlockSpec((1, D_MODEL), const),
            pl.BlockSpec((1, D_MODEL), const),
            pl.BlockSpec((D_MODEL, QKV_W), const),
            pl.BlockSpec((1, LANES), const),
            pl.BlockSpec((1, LANES), const),
            pl.BlockSpec((tm, LANES), table),
            pl.BlockSpec((tm, LANES), table),
            pl.BlockSpec((tm, LANES), table),
            pl.BlockSpec((tm, LANES), table),
        ],
        out_specs=([pl.BlockSpec((tm, w), row) for w in widths[:3]]
                   + [pl.BlockSpec((B_GROUP_W // LANES, tm, LANES), lambda i: (0, dst(i), 0))] * 9),
        out_shape=([jax.ShapeDtypeStruct((m, w), BF16) for w in widths[:3]]
                   + [jax.ShapeDtypeStruct((B_GROUP_W // LANES, m, LANES), F32)] * 9),
        scratch_shapes=[pltpu.VMEM((2, tm, D_MODEL), BF16)],
        compiler_params=_params(1),
        name="in_projection",
    )(x2, ln0_g.reshape(1, D_MODEL), ln0_b.reshape(1, D_MODEL), w_qkv, gain_tile(qn_g), gain_tile(kn_g),
      cosa, sina, cosb, sinb)


A_HEADS_PER_UNIT = 2


def _attn_a_kernel(q_ref, k_ref, v_ref, wg_ref, wu_ref, wd_ref, o_ref, wgb_ref, wub_ref, wdb_ref):
    wgb_ref[...] = wg_ref[...].astype(BF16)
    wub_ref[...] = wu_ref[...].astype(BF16)
    wdb_ref[...] = wd_ref[...].astype(BF16)
    tq = q_ref.shape[1]
    n_units = A_Q_HEADS // A_HEADS_PER_UNIT

    def scores(u):
        heads = range(u * A_HEADS_PER_UNIT, (u + 1) * A_HEADS_PER_UNIT)
        kh = heads[0] // A_GROUP
        q = jnp.concatenate([q_ref[0, :, h * HEAD_DIM:(h + 1) * HEAD_DIM] for h in heads], axis=0)
        k = k_ref[0, :, kh * HEAD_DIM:(kh + 1) * HEAD_DIM]
        return lax.dot_general(q, k, NT_DIMS, preferred_element_type=F32)

    outs = []
    v_ones = []
    for kh in range(A_KV_HEADS):
        v = v_ref[0, :, kh * HEAD_DIM:(kh + 1) * HEAD_DIM]
        v_ones.append(jnp.concatenate([v, jnp.ones_like(v)], axis=1))
    s = scores(0)
    for u in range(n_units):
        s_next = scores(u + 1) if u + 1 < n_units else None
        p = jnp.exp2(s - jnp.max(s, axis=-1, keepdims=True)).astype(BF16)
        o = jnp.dot(p, v_ones[u * A_HEADS_PER_UNIT // A_GROUP], preferred_element_type=F32)
        o = o[:, :HEAD_DIM] / o[:, HEAD_DIM:HEAD_DIM + 1]
        outs.extend(o[i * tq:(i + 1) * tq] for i in range(A_HEADS_PER_UNIT))
        s = s_next
    o_ref[0] = jnp.concatenate(outs, axis=1).astype(BF16)


def _attention_a(qa, ka, va, expert_weights, tq):
    b, s, _ = qa.shape
    steps = b * (s // tq)
    flat = [w.reshape(-1, w.shape[-1]) for w in expert_weights]
    wslice = lambda w: pl.BlockSpec((w.shape[0] // steps, w.shape[1]), lambda i, j: (i * (s // tq) + j, 0))
    o, *rounded = pl.pallas_call(
        _attn_a_kernel,
        grid=(b, s // tq),
        in_specs=[
            pl.BlockSpec((1, tq, A_Q_W), lambda i, j: (i, j, 0)),
            pl.BlockSpec((1, s, A_KV_W), lambda i, j: (i, 0, 0)),
            pl.BlockSpec((1, s, A_KV_W), lambda i, j: (i, 0, 0)),
            *map(wslice, flat),
        ],
        out_specs=[pl.BlockSpec((1, tq, A_Q_W), lambda i, j: (i, j, 0)), *map(wslice, flat)],
        out_shape=[jax.ShapeDtypeStruct((b, s, A_Q_W), BF16),
                   *[jax.ShapeDtypeStruct(w.shape, BF16) for w in flat]],
        compiler_params=_params(2),
        name="attention_a",
    )(qa, ka, va, *flat)
    return o, [r.reshape(w.shape) for r, w in zip(rounded, expert_weights)]


BAND_QB = 128
BAND_UNITS_PER_STEP = 16
BAND_STEPS = 1


def _band_bias(n):
    win = min(2 * BAND_QB, n)
    rows = B_HEADS_PER_GROUP * BAND_QB
    rel = (np.arange(rows)[:, None] % BAND_QB) - np.arange(win)[None, :]
    shifts = np.arange(2 * BAND_QB // BAND_HALF - 1) * BAND_HALF
    inside = np.abs(rel[None] + shifts[:, None, None]) <= BAND_HALF
    return jnp.asarray(np.where(inside, 0.0, MASK_VALUE).astype(np.float32))


def _dilated_kernel(q_ref, k_ref, v_ref, bias_ref, o_ref, lse_ref, *, n, dilation):
    win = min(2 * BAND_QB, n)
    n_blk = n // BAND_QB
    lane_head = lax.broadcasted_iota(jnp.int32, (1, B_GROUP_W), 1) // HEAD_DIM

    planes = B_GROUP_W // LANES

    def class_rows(r, start, size):
        if dilation == 1:
            return pl.ds(pl.multiple_of(start, BAND_HALF), size)
        return pl.ds(r + start * dilation, size, stride=dilation)

    def load(ref, rows_):
        return jnp.concatenate([ref[j, rows_, :] for j in range(planes)], axis=1)

    def scores(i):
        unit = pl.program_id(1) * BAND_UNITS_PER_STEP + i
        r = unit // n_blk
        qs = (unit % n_blk) * BAND_QB
        ks = jnp.clip(qs - BAND_HALF, 0, n - win)
        q = load(q_ref, class_rows(r, qs, BAND_QB))
        k = load(k_ref, class_rows(r, ks, win)).astype(BF16)
        q_stack = jnp.concatenate([jnp.where(lane_head == h, q, 0.0) for h in range(B_HEADS_PER_GROUP)], axis=0)
        s = lax.dot_general(q_stack.astype(BF16), k, NT_DIMS, preferred_element_type=F32)
        return r, qs, ks, s + bias_ref[(qs - ks) // BAND_HALF]

    nxt = scores(0)
    for i in range(BAND_UNITS_PER_STEP):
        r, qs, ks, s = nxt
        if i + 1 < BAND_UNITS_PER_STEP:
            nxt = scores(i + 1)
        v = load(v_ref, class_rows(r, ks, win)).astype(BF16)
        mx = jnp.max(s, axis=-1, keepdims=True)
        p = jnp.exp2(s - mx)
        denom = jnp.sum(p, axis=-1, keepdims=True)
        pv = jnp.dot(p.astype(BF16), v, preferred_element_type=F32) / denom
        lse = mx + jnp.log2(denom)
        o = jnp.zeros((BAND_QB, B_GROUP_W), F32)
        l = jnp.zeros((BAND_QB, B_GROUP_W), F32)
        for h in range(B_HEADS_PER_GROUP):
            blk = slice(h * BAND_QB, (h + 1) * BAND_QB)
            o = jnp.where(lane_head == h, pv[blk], o)
            l = jnp.where(lane_head == h, lse[blk], l)
        for j in range(planes):
            o_ref[j, class_rows(r, qs, BAND_QB), :] = o[:, j * LANES:(j + 1) * LANES]
            lse_ref[j, class_rows(r, qs, BAND_QB), :] = l[:, j * LANES:(j + 1) * LANES]


def _dilated_attention(q, k, v, seq, dilation):
    planes, tokens, _ = q.shape
    n = seq // dilation
    assert dilation * (n // BAND_QB) == BAND_UNITS_PER_STEP * BAND_STEPS
    blk = pl.BlockSpec((planes, seq, LANES), lambda i, j: (0, i, 0))
    bias = _band_bias(n)
    return pl.pallas_call(
        functools.partial(_dilated_kernel, n=n, dilation=dilation),
        grid=(tokens // seq, BAND_STEPS),
        in_specs=[blk, blk, blk, pl.BlockSpec(bias.shape, lambda i, j: (0, 0, 0))],
        out_specs=[blk, blk],
        out_shape=[jax.ShapeDtypeStruct(q.shape, F32)] * 2,
        compiler_params=_params(2),
        name=f"dilated_attention_d{dilation}",
    )(q, k, v, bias)


def _merge_kernel(x_ref, g0_ref, b0_ref, wg_ref, bg_ref, oa_ref, o0_ref, o1_ref, o2_ref, l0_ref, l1_ref, l2_ref,
                  wa_ref, wb_ref, wo_ref, g1_ref, b1_ref, wr_hi_ref, wr_lo_ref,
                  h1_ref, h1b_ref, logit_ref, z_ref):
    i = pl.program_id(0)

    @pl.when(i == 0)
    def _():
        z_ref[1] = jnp.zeros(z_ref.shape[1:], F32)

    planes = lambda ref: jnp.concatenate([ref[j] for j in range(ref.shape[0])], axis=1)
    ya = jnp.dot(oa_ref[...], wa_ref[...], preferred_element_type=F32)
    h0 = _layer_norm(x_ref[...], g0_ref[...], b0_ref[...])
    gate_t = jnp.tanh(jnp.dot(h0.astype(BF16), wg_ref[...], preferred_element_type=F32) + bg_ref[...])

    h1 = _layer_norm(z_ref[(i + 1) % 2], g1_ref[...], b1_ref[...])
    h1_ref[...] = h1
    h1b_ref[...] = h1.astype(BF16)
    hi, lo = _split_bf16(h1)
    logit_ref[0] = (lax.dot_general(wr_hi_ref[...], hi, NT_DIMS, preferred_element_type=F32)
                    + lax.dot_general(wr_hi_ref[...], lo, NT_DIMS, preferred_element_type=F32)
                    + lax.dot_general(wr_lo_ref[...], hi, NT_DIMS, preferred_element_type=F32))

    l0, l1, l2 = planes(l0_ref), planes(l1_ref), planes(l2_ref)
    mx = jnp.maximum(jnp.maximum(l0, l1), l2)
    e0, e1, e2 = jnp.exp2(l0 - mx), jnp.exp2(l1 - mx), jnp.exp2(l2 - mx)
    ob = (e0 * planes(o0_ref) + e1 * planes(o1_ref) + e2 * planes(o2_ref)) / (e0 + e1 + e2)
    yb = jnp.dot(ob.astype(BF16), wb_ref[...], preferred_element_type=F32)
    merged = (ya + yb) + (gate_t[:, :D_MODEL] * ya + gate_t[:, D_MODEL:] * yb)
    mix = jnp.dot(merged.astype(BF16), wo_ref[...], preferred_element_type=F32)
    z_ref[i % 2] = DEEPNORM_ALPHA * h0 + mix


def _merge(x2, ln0_g, ln0_b, w_gates, b_gate, oa, obs, lses, w_a, w_b, w_o, ln1_g, ln1_b, w_router, seq, tm):
    m = x2.shape[0]
    n_tiles = m // tm
    tiles_per_seq = seq // tm
    src = lambda i: jnp.minimum(i, n_tiles - 1)
    dst = lambda i: jnp.maximum(i - 1, 0)
    const = lambda i: (0, 0)
    wr_t = w_router.T
    wr_hi = wr_t.astype(BF16)
    wr_lo = (wr_t - wr_hi.astype(F32)).astype(BF16)
    vec = lambda v: v.reshape(1, -1)
    full = lambda a: pl.BlockSpec(a.shape, const)
    tile_in = lambda w: pl.BlockSpec((tm, w), lambda i: (src(i), 0))
    tile_out = lambda w: pl.BlockSpec((tm, w), lambda i: (dst(i), 0))
    plane_in = pl.BlockSpec((B_GROUP_W // LANES, tm, LANES), lambda i: (0, src(i), 0))
    args = [x2, vec(ln0_g), vec(ln0_b), w_gates, vec(b_gate), oa, *obs, *lses, w_a, w_b, w_o,
            vec(ln1_g), vec(ln1_b), wr_hi, wr_lo]
    in_specs = [tile_in(D_MODEL), full(args[1]), full(args[2]), full(w_gates), full(args[4]), tile_in(A_Q_W),
                *[plane_in] * 6, full(w_a), full(w_b), full(w_o), full(args[15]), full(args[16]),
                full(wr_hi), full(wr_lo)]
    return pl.pallas_call(
        _merge_kernel,
        grid=(n_tiles + 1,),
        in_specs=in_specs,
        out_specs=[tile_out(D_MODEL), tile_out(D_MODEL),
                   pl.BlockSpec((1, N_EXPERTS, tm),
                                lambda i: (dst(i) // tiles_per_seq, 0, dst(i) % tiles_per_seq))],
        out_shape=[jax.ShapeDtypeStruct((m, D_MODEL), F32), jax.ShapeDtypeStruct((m, D_MODEL), BF16),
                   jax.ShapeDtypeStruct((m // seq, N_EXPERTS, seq), F32)],
        scratch_shapes=[pltpu.VMEM((2, tm, D_MODEL), F32)],
        compiler_params=_params(1),
        name="merge_ln1_router",
    )(*args)


PREFIX_CHUNK = 256


def _prefix_count(mask, tri):
    ones = jnp.where(mask, 1.0, 0.0)
    carry = jnp.zeros((mask.shape[0], 1), F32)
    outs, bounds = [], [carry]
    for j in range(mask.shape[1] // PREFIX_CHUNK):
        chunk = ones[:, j * PREFIX_CHUNK:(j + 1) * PREFIX_CHUNK]
        outs.append(jnp.dot(chunk.astype(BF16), tri, preferred_element_type=F32) + carry)
        carry = carry + jnp.sum(chunk, axis=1, keepdims=True)
        bounds.append(carry)
    return jnp.concatenate(outs, axis=1), bounds


THRESHOLD_BITS = 31
THRESHOLD_REFINE = 16


def _route_kernel(logit_ref, slot_ref, aff_ref, starts_ref, *, cap):
    n_seq = logit_ref.shape[0]
    affs = []
    for b in range(n_seq):
        lg = logit_ref[b]
        ex = jnp.exp(lg - jnp.max(lg, axis=0, keepdims=True))
        affs.append(ex / jnp.sum(ex, axis=0, keepdims=True))
    aff = jnp.concatenate(affs, axis=0)

    def count_ge(t):
        return jnp.sum(jnp.where(aff >= t, 1.0, 0.0), axis=1, keepdims=True)

    thr = jnp.zeros((aff.shape[0], 1), jnp.int32)
    for bit in range(THRESHOLD_BITS - 1, -1, -1):
        cand = thr | (1 << bit)
        thr = jnp.where(count_ge(pltpu.bitcast(cand, F32)) >= cap, cand, thr)
    lo = pltpu.bitcast(thr, F32)
    hi = pltpu.bitcast(thr + 1, F32)
    for _ in range(THRESHOLD_REFINE):
        mid = 0.5 * (lo + hi)
        take = count_ge(mid) >= cap
        lo = jnp.where(take, mid, lo)
        hi = jnp.where(take, hi, mid)
    above = aff >= hi
    tied = (aff >= lo) & (aff < hi)
    r = lax.broadcasted_iota(jnp.int32, (PREFIX_CHUNK, PREFIX_CHUNK), 0)
    c = lax.broadcasted_iota(jnp.int32, (PREFIX_CHUNK, PREFIX_CHUNK), 1)
    tri = jnp.where(r < c, 1.0, 0.0).astype(BF16)
    need = cap - jnp.sum(jnp.where(above, 1.0, 0.0), axis=1, keepdims=True)
    sel = above | (tied & (_prefix_count(tied, tri)[0] < need))
    rank, bounds = _prefix_count(sel, tri)
    slot = jnp.where(sel, rank, -1.0).astype(jnp.int32)
    lane = lax.broadcasted_iota(jnp.int32, (1, LANES), 1)
    starts = jnp.zeros((aff.shape[0], LANES), F32)
    for j, count in enumerate(bounds):
        starts = jnp.where(lane == j, count, starts)
    for b in range(n_seq):
        rows = slice(b * N_EXPERTS, (b + 1) * N_EXPERTS)
        slot_ref[b] = slot[rows]
        aff_ref[b] = aff[rows]
        starts_ref[b] = starts[rows].astype(jnp.int32)


ROUTE_SEQS_PER_STEP = 8


def _route(logits, batch, seq, cap):
    nb = ROUTE_SEQS_PER_STEP if batch % ROUTE_SEQS_PER_STEP == 0 else 1
    blk = pl.BlockSpec((nb, N_EXPERTS, seq), lambda i: (i, 0, 0))
    return pl.pallas_call(
        functools.partial(_route_kernel, cap=cap),
        grid=(batch // nb,),
        in_specs=[blk],
        out_specs=[blk, blk, pl.BlockSpec((nb, N_EXPERTS, LANES), lambda i: (i, 0, 0))],
        out_shape=[jax.ShapeDtypeStruct((batch, N_EXPERTS, seq), jnp.int32),
                   jax.ShapeDtypeStruct((batch, N_EXPERTS, seq), F32),
                   jax.ShapeDtypeStruct((batch, N_EXPERTS, LANES), jnp.int32)],
        compiler_params=_params(1),
        name="route",
    )(logits)


BF16_SUBLANES = 16
DISPATCH_WIN = 64
DISPATCH_CHUNKS_PER_STEP = 8
COMBINE_CHUNKS_PER_STEP = 4


def _slot_windows(starts_ref, b, j, n_bounds, win):
    base, n_win = [], 0
    for e in range(N_EXPERTS):
        at = (b * N_EXPERTS + e) * n_bounds + j
        first = (starts_ref[at] // BF16_SUBLANES) * BF16_SUBLANES
        base.append(first)
        n_win = jnp.maximum(n_win, (starts_ref[at + 1] - first + win - 1) // win)
    return base, n_win


def _window_rows(base_e, k, wrow, cap, win):
    lower = base_e + k * win
    off = pl.multiple_of(jnp.minimum(lower, cap - win), BF16_SUBLANES)
    return off, jnp.where(wrow + off >= lower, wrow + off, -2)


def _dispatch_kernel(starts_ref, h_ref, slot_ref, aff_ref, xe_ref, gate_ref, *, cap):
    b, step = pl.program_id(0), pl.program_id(1)
    win = DISPATCH_WIN
    chunks = h_ref.shape[1] // PREFIX_CHUNK
    n_bounds = pl.num_programs(1) * chunks + 1

    @pl.when(step == 0)
    def _():
        xe_ref[...] = jnp.zeros_like(xe_ref)
        gate_ref[...] = jnp.zeros_like(gate_ref)

    wrow = lax.broadcasted_iota(jnp.int32, (win, PREFIX_CHUNK), 0)
    for c in range(chunks):
        tokens = slice(c * PREFIX_CHUNK, (c + 1) * PREFIX_CHUNK)
        base, n_win = _slot_windows(starts_ref, b, step * chunks + c, n_bounds, win)
        h = h_ref[0, tokens, :]

        def window(k):
            offs, onehots = [], []
            for e in range(N_EXPERTS):
                off, target = _window_rows(base[e], k, wrow, cap, win)
                offs.append(off)
                onehots.append(slot_ref[0, e:e + 1, tokens] == target)
            sel = jnp.concatenate([jnp.where(o, 1.0, 0.0) for o in onehots], axis=0).astype(BF16)
            rows = jnp.dot(sel, h, preferred_element_type=F32).astype(BF16)
            for e in range(N_EXPERTS):
                at = pl.ds(offs[e], win)
                xe_ref[e, at, :] += rows[e * win:(e + 1) * win]
                gate = jnp.sum(jnp.where(onehots[e], aff_ref[0, e:e + 1, tokens], 0.0), axis=1, keepdims=True)
                gate_ref[e, at, :] += jnp.broadcast_to(gate, (win, LANES))

        window(0)
        lax.fori_loop(1, n_win, lambda k, carry: (window(k), carry)[1], 0)


def _dispatch(h1b, slot, aff, starts, cap):
    b, s, _ = h1b.shape
    n_chunks = s // PREFIX_CHUNK
    span = DISPATCH_CHUNKS_PER_STEP * PREFIX_CHUNK
    chunk = pl.BlockSpec((1, N_EXPERTS, span), lambda i, j, st: (i, 0, j))
    grid_spec = pltpu.PrefetchScalarGridSpec(
        num_scalar_prefetch=1,
        grid=(b, s // span),
        in_specs=[pl.BlockSpec((1, span, D_MODEL), lambda i, j, st: (i, j, 0)), chunk, chunk],
        out_specs=[pl.BlockSpec((N_EXPERTS, cap, D_MODEL), lambda i, j, st: (0, i, 0)),
                   pl.BlockSpec((N_EXPERTS, cap, LANES), lambda i, j, st: (0, i, 0))],
    )
    return pl.pallas_call(
        functools.partial(_dispatch_kernel, cap=cap),
        grid_spec=grid_spec,
        out_shape=[jax.ShapeDtypeStruct((N_EXPERTS, b * cap, D_MODEL), BF16),
                   jax.ShapeDtypeStruct((N_EXPERTS, b * cap, LANES), F32)],
        compiler_params=_params(2),
        name="dispatch",
    )(starts[:, :, :n_chunks + 1].reshape(-1), h1b, slot, aff)


FF_CHUNK = 1024


def _expert_kernel(x_ref, gate_ref, wg_ref, wu_ref, wd_ref, y_ref):
    x = x_ref[0]
    acc = jnp.zeros((x.shape[0], D_MODEL), F32)
    for c in range(D_FF // FF_CHUNK):
        cols = slice(c * FF_CHUNK, (c + 1) * FF_CHUNK)
        g = jnp.dot(x, wg_ref[0, :, cols], preferred_element_type=F32)
        u = jnp.dot(x, wu_ref[0, :, cols], preferred_element_type=F32)
        act = (g * jax.nn.sigmoid(g) * u).astype(BF16)
        acc = acc + jnp.dot(act, wd_ref[0, cols, :], preferred_element_type=F32)
    y_ref[0] = (acc * gate_ref[0][:, :1]).astype(BF16)


def _expert_ffn(xe, gate, w_gate, w_up, w_down, tm):
    e, rows, _ = xe.shape
    tok = lambda w: pl.BlockSpec((1, tm, w), lambda i, j: (i, j, 0))
    wspec = lambda a: pl.BlockSpec((1,) + a.shape[1:], lambda i, j: (i, 0, 0))
    return pl.pallas_call(
        _expert_kernel,
        grid=(e, rows // tm),
        in_specs=[tok(D_MODEL), tok(LANES), wspec(w_gate), wspec(w_up), wspec(w_down)],
        out_specs=tok(D_MODEL),
        out_shape=jax.ShapeDtypeStruct((e, rows, D_MODEL), BF16),
        compiler_params=_params(2),
        name="expert_ffn",
    )(xe, gate, w_gate, w_up, w_down)


COMBINE_WIN = 64
TN_DIMS = (((0,), (0,)), ((), ()))


def _combine_kernel(starts_ref, h1_ref, slot_ref, y_ref, g2_ref, b2_ref, out_ref, acc_ref, *, cap, n_chunks):
    s = pl.program_id(0)
    cur = jnp.minimum(s, pl.num_programs(0) - 2)
    steps_per_seq = n_chunks // COMBINE_CHUNKS_PER_STEP
    win = COMBINE_WIN

    @pl.when(s == 0)
    def _():
        acc_ref[1] = jnp.zeros(acc_ref.shape[1:], F32)

    out_ref[0] = _layer_norm(DEEPNORM_ALPHA * h1_ref[0] + acc_ref[(s + 1) % 2], g2_ref[...], b2_ref[...])

    wrow = lax.broadcasted_iota(jnp.int32, (win, PREFIX_CHUNK), 0)
    for c in range(COMBINE_CHUNKS_PER_STEP):
        tokens = slice(c * PREFIX_CHUNK, (c + 1) * PREFIX_CHUNK)
        chunk = (cur % steps_per_seq) * COMBINE_CHUNKS_PER_STEP + c
        base, n_win = _slot_windows(starts_ref, cur // steps_per_seq, chunk, n_chunks + 1, win)

        def gather(k):
            sels, rows = [], []
            for e in range(N_EXPERTS):
                off, target = _window_rows(base[e], k, wrow, cap, win)
                sels.append(jnp.where(slot_ref[0, e:e + 1, tokens] == target, 1.0, 0.0))
                rows.append(y_ref[e, pl.ds(off, win), :])
            sel = jnp.concatenate(sels, axis=0).astype(BF16)
            return lax.dot_general(sel, jnp.concatenate(rows, axis=0), TN_DIMS, preferred_element_type=F32)

        acc_ref[s % 2, tokens, :] = gather(0)

        def more(k, carry):
            acc_ref[s % 2, tokens, :] += gather(k)
            return carry

        lax.fori_loop(1, n_win, more, 0)


def _combine(h1, slot, starts, y, ln2_g, ln2_b, cap):
    b, s, _ = h1.shape
    n_chunks = s // PREFIX_CHUNK
    span = COMBINE_CHUNKS_PER_STEP * PREFIX_CHUNK
    per_seq = s // span
    n_steps = b * per_seq
    cur = lambda i: jnp.minimum(i, n_steps - 1)
    prev = lambda i: jnp.maximum(i - 1, 0)
    const = lambda i, st: (0, 0)
    grid_spec = pltpu.PrefetchScalarGridSpec(
        num_scalar_prefetch=1,
        grid=(n_steps + 1,),
        in_specs=[pl.BlockSpec((1, span, D_MODEL), lambda i, st: (prev(i) // per_seq, prev(i) % per_seq, 0)),
                  pl.BlockSpec((1, N_EXPERTS, span), lambda i, st: (cur(i) // per_seq, 0, cur(i) % per_seq)),
                  pl.BlockSpec((N_EXPERTS, cap, D_MODEL), lambda i, st: (0, cur(i) // per_seq, 0)),
                  pl.BlockSpec((1, D_MODEL), const), pl.BlockSpec((1, D_MODEL), const)],
        out_specs=pl.BlockSpec((1, span, D_MODEL), lambda i, st: (prev(i) // per_seq, prev(i) % per_seq, 0)),
        scratch_shapes=[pltpu.VMEM((2, span, D_MODEL), F32)],
    )
    return pl.pallas_call(
        functools.partial(_combine_kernel, cap=cap, n_chunks=n_chunks),
        grid_spec=grid_spec,
        out_shape=jax.ShapeDtypeStruct((b, s, D_MODEL), F32),
        compiler_params=_params(1),
        name="combine_ln2",
    )(starts[:, :, :n_chunks + 1].reshape(-1), h1, slot, y, ln2_g.reshape(1, D_MODEL), ln2_b.reshape(1, D_MODEL))


def kernel(x, ln0_g, ln0_b, w_in, b_gate, qn_g, kn_g, w_branch_a, w_branch_b, w_out, ln1_g, ln1_b, w_router,
           w_gate_e, w_up_e, w_down_e, ln2_g, ln2_b):
    batch, seq, _ = x.shape
    assert w_in.shape[0] == DEPTH
    cap = EC_CAPACITY_FACTOR * seq // N_EXPERTS
    tokens = batch * seq
    x2 = x.reshape(tokens, D_MODEL)
    w_qkv = w_in[0, :, :QKV_W].astype(BF16)
    w_gates = (0.5 * w_in[0, :, QKV_W:]).astype(BF16)

    qkv = _in_projection(x2, ln0_g, ln0_b, w_qkv, qn_g[0], kn_g[0], seq, tm=IN_PROJ_ROW_TILE)
    qa, ka, va = qkv[:3]
    qb, kb, vb = qkv[3:6], qkv[6:9], qkv[9:12]

    oa, (wg_b, wu_b, wd_b) = _attention_a(qa.reshape(batch, seq, A_Q_W), ka.reshape(batch, seq, A_KV_W),
                                          va.reshape(batch, seq, A_KV_W), (w_gate_e[0], w_up_e[0], w_down_e[0]),
                                          tq=ROW_TILE)
    oa = oa.reshape(tokens, A_Q_W)

    obs, lses = [], []
    for gi, (window, dilation) in enumerate(B_GROUPS):
        assert window // (2 * dilation) == BAND_HALF
        o, lse = _dilated_attention(qb[gi], kb[gi], vb[gi], seq, dilation)
        obs.append(o)
        lses.append(lse)

    h1, h1b, logits = _merge(x2, ln0_g, ln0_b, w_gates, 0.5 * b_gate[0], oa, obs, lses,
                             w_branch_a[0].astype(BF16), w_branch_b[0].astype(BF16), (0.5 * w_out[0]).astype(BF16),
                             ln1_g[0], ln1_b[0], w_router[0], seq, tm=ROW_TILE)

    slot, aff, starts = _route(logits, batch, seq, cap)
    xe, gate = _dispatch(h1b.reshape(batch, seq, D_MODEL), slot, aff, starts, cap)
    y = _expert_ffn(xe, gate, wg_b, wu_b, wd_b, tm=min(FFN_ROW_TILE, batch * cap))
    return _combine(h1.reshape(batch, seq, D_MODEL), slot, starts, y, ln2_g[0], ln2_b[0], cap)
```

```python
import functools

import jax
import jax.numpy as jnp
import numpy as np
from jax import lax
from jax.experimental import pallas as pl
from jax.experimental.pallas import tpu as pltpu

F32 = jnp.float32
BF16 = jnp.bfloat16

D_MODEL = 1024
HEAD_DIM = 64
A_Q_HEADS = 8
A_KV_HEADS = 2
A_GROUP = A_Q_HEADS // A_KV_HEADS
B_GROUPS = ((128, 1), (512, 4), (2048, 16))
B_HEADS_PER_GROUP = 4
GRID_W = 64
A_ROPE_THETA = 10000.0
B_ROPE_THETA = 500000.0
B_ROPE_DIMS = HEAD_DIM // 4
N_EXPERTS = 16
EC_CAPACITY_FACTOR = 2
D_FF = 2 * D_MODEL
LN_EPS = 1e-5
QK_NORM_EPS = 1e-6
MASK_VALUE = -1e30
DEPTH = 1
DEEPNORM_ALPHA = (2.0 * DEPTH) ** 0.25
LOG2_E = 1.4426950408889634
Q_SCALE = HEAD_DIM ** -0.5 * LOG2_E

A_Q_W = A_Q_HEADS * HEAD_DIM
A_KV_W = A_KV_HEADS * HEAD_DIM
B_GROUP_W = B_HEADS_PER_GROUP * HEAD_DIM
B_W = B_GROUP_W * len(B_GROUPS)
QKV_W = A_Q_W + 2 * A_KV_W + 3 * B_W
BAND_HALF = 64

LANES = 128
VMEM_LIMIT = 56 * 1024 * 1024
ROW_TILE = 512
IN_PROJ_ROW_TILE = 1024
FFN_ROW_TILE = 1024

NT_DIMS = (((1,), (1,)), ((), ()))


def _params(n_axes):
    return pltpu.CompilerParams(dimension_semantics=("arbitrary",) * n_axes, vmem_limit_bytes=VMEM_LIMIT)


def _layer_norm(x, g, b):
    mu = jnp.mean(x, axis=-1, keepdims=True)
    xc = x - mu
    var = jnp.mean(xc * xc, axis=-1, keepdims=True)
    return xc * lax.rsqrt(var + LN_EPS) * g + b


def _split_bf16(x):
    hi = x.astype(BF16)
    lo = (x - hi.astype(F32)).astype(BF16)
    return hi, lo


def _swap_halves(y, lane, period, half):
    fwd = pltpu.roll(y, LANES - half, 1)
    bwd = pltpu.roll(y, half, 1)
    return jnp.where((lane % period) < half, fwd, bwd)


def _inproj_kernel(x_ref, g0_ref, b0_ref, w_ref, qng_ref, kng_ref, cosa_ref, sina_ref, cosb_ref, sinb_ref,
                   qa_ref, ka_ref, va_ref, qb0_ref, qb1_ref, qb2_ref, kb0_ref, kb1_ref, kb2_ref,
                   vb0_ref, vb1_ref, vb2_ref, h_ref):
    step = pl.program_id(0)

    @pl.when(step == 0)
    def _():
        h_ref[1] = jnp.zeros(h_ref.shape[1:], BF16)

    h = h_ref[(step + 1) % 2]
    lane = lax.broadcasted_iota(jnp.int32, (1, LANES), 1)
    r = lax.broadcasted_iota(jnp.int32, (LANES, LANES), 0)
    c = lax.broadcasted_iota(jnp.int32, (LANES, LANES), 1)
    head_sum = jnp.where((r // HEAD_DIM) == (c // HEAD_DIM), 1.0, 0.0).astype(BF16)
    cosa, sina = cosa_ref[...], sina_ref[...]
    cosb, sinb = cosb_ref[...], sinb_ref[...]

    def proj(off):
        y = jnp.dot(h, w_ref[:, off:off + 2 * LANES], preferred_element_type=F32)
        return y[:, :LANES], y[:, LANES:]

    def norm_rope_a(y, gain, scale):
        hi, lo = _split_bf16(y * y)
        ss = (jnp.dot(hi, head_sum, preferred_element_type=F32)
              + jnp.dot(lo, head_sum, preferred_element_type=F32))
        y = y * lax.rsqrt(ss * (1.0 / HEAD_DIM) + QK_NORM_EPS) * gain
        y = y * cosa + _swap_halves(y, lane, HEAD_DIM // 2, HEAD_DIM // 4) * sina
        return (y * scale).astype(BF16)

    def rope_b(y, scale):
        y = y * cosb + _swap_halves(y, lane, HEAD_DIM, B_ROPE_DIMS // 2) * sinb
        return y * scale

    b_base = A_Q_W + 2 * A_KV_W

    def project_b_qk(gi, q_ref, k_ref):
        off = b_base + gi * B_GROUP_W
        for j, (q, k) in enumerate(zip(proj(off), proj(off + B_W))):
            q_ref[j] = rope_b(q, Q_SCALE)
            k_ref[j] = rope_b(k, 1.0)

    qa_tiles = [y for j in range(A_Q_W // (2 * LANES)) for y in proj(2 * j * LANES)]
    ka, va = proj(A_Q_W)
    h_next = _layer_norm(x_ref[...], g0_ref[...], b0_ref[...]).astype(BF16)
    project_b_qk(0, qb0_ref, kb0_ref)
    for j, y in enumerate(qa_tiles):
        qa_ref[:, j * LANES:(j + 1) * LANES] = norm_rope_a(y, qng_ref[...], Q_SCALE)
    ka_ref[...] = norm_rope_a(ka, kng_ref[...], 1.0)
    va_ref[...] = va.astype(BF16)
    project_b_qk(1, qb1_ref, kb1_ref)
    project_b_qk(2, qb2_ref, kb2_ref)
    for gi, v_ref in enumerate((vb0_ref, vb1_ref, vb2_ref)):
        for j, v in enumerate(proj(b_base + 2 * B_W + gi * B_GROUP_W)):
            v_ref[j] = v

    h_ref[step % 2] = h_next


def _rope_tables(seq):
    f32 = np.float32
    d = np.arange(LANES) % HEAD_DIM
    t = np.arange(seq)
    half_rot = HEAD_DIM // 2
    inv_a = f32(A_ROPE_THETA) ** (-np.arange(0, half_rot, 2, dtype=f32) / f32(half_rot))
    j = d % half_rot
    pos_row = (t // GRID_W).astype(f32)
    pos_col = (t % GRID_W).astype(f32)
    freq_a = inv_a[j % (half_rot // 2)].astype(f32)
    ang_a = np.where((d < half_rot)[None, :], pos_row[:, None], pos_col[:, None]) * freq_a[None, :]
    sign_a = np.where(j < half_rot // 2, -1.0, 1.0).astype(f32)
    cosa = np.cos(ang_a).astype(f32)
    sina = (np.sin(ang_a) * sign_a[None, :]).astype(f32)
    inv_b = f32(B_ROPE_THETA) ** (-np.arange(0, B_ROPE_DIMS, 2, dtype=f32) / f32(B_ROPE_DIMS))
    freq_b = inv_b[d % (B_ROPE_DIMS // 2)].astype(f32)
    ang_b = t.astype(f32)[:, None] * freq_b[None, :]
    rot = (d < B_ROPE_DIMS)[None, :]
    sign_b = np.where(d < B_ROPE_DIMS // 2, -1.0, 1.0).astype(f32)
    cosb = np.where(rot, np.cos(ang_b), 1.0).astype(f32)
    sinb = np.where(rot, np.sin(ang_b) * sign_b[None, :], 0.0).astype(f32)
    return tuple(jnp.asarray(a) for a in (cosa, sina, cosb, sinb))


def _in_projection(x2, ln0_g, ln0_b, w_qkv, qn_g, kn_g, seq, tm):
    m = x2.shape[0]
    cosa, sina, cosb, sinb = _rope_tables(seq)
    n_tiles = m // tm
    tiles_per_seq = seq // tm
    dst = lambda i: jnp.maximum(i - 1, 0)
    row = lambda i: (dst(i), 0)
    const = lambda i: (0, 0)
    table = lambda i: (dst(i) % tiles_per_seq, 0)
    widths = (A_Q_W, A_KV_W, A_KV_W) + (B_GROUP_W,) * 9
    gain_tile = lambda g: jnp.tile(g.reshape(1, HEAD_DIM), (1, LANES // HEAD_DIM))
    return pl.pallas_call(
        _inproj_kernel,
        grid=(n_tiles + 1,),
        in_specs=[
            pl.BlockSpec((tm, D_MODEL), lambda i: (jnp.minimum(i, n_tiles - 1), 0)),
            pl.BlockSpec((1, D_MODEL), const),
            pl.BlockSpec((1, D_MODEL), const),
            pl.BlockSpec((D_MODEL, QKV_W), const),
            pl.BlockSpec((1, LANES), const),
            pl.BlockSpec((1, LANES), const),
            pl.BlockSpec((tm, LANES), table),
            pl.BlockSpec((tm, LANES), table),
            pl.BlockSpec((tm, LANES), table),
            pl.BlockSpec((tm, LANES), table),
        ],
        out_specs=([pl.BlockSpec((tm, w), row) for w in widths[:3]]
                   + [pl.BlockSpec((B_GROUP_W // LANES, tm, LANES), lambda i: (0, dst(i), 0))] * 9),
        out_shape=([jax.ShapeDtypeStruct((m, w), BF16) for w in widths[:3]]
                   + [jax.ShapeDtypeStruct((B_GROUP_W // LANES, m, LANES), F32)] * 9),
        scratch_shapes=[pltpu.VMEM((2, tm, D_MODEL), BF16)],
        compiler_params=_params(1),
        name="in_projection",
    )(x2, ln0_g.reshape(1, D_MODEL), ln0_b.reshape(1, D_MODEL), w_qkv, gain_tile(qn_g), gain_tile(kn_g),
      cosa, sina, cosb, sinb)


A_HEADS_PER_UNIT = 2


def _attn_a_kernel(q_ref, k_ref, v_ref, wg_ref, wu_ref, wd_ref, o_ref, wgb_ref, wub_ref, wdb_ref):
    wgb_ref[...] = wg_ref[...].astype(BF16)
    wub_ref[...] = wu_ref[...].astype(BF16)
    wdb_ref[...] = wd_ref[...].astype(BF16)
    tq = q_ref.shape[1]
    n_units = A_Q_HEADS // A_HEADS_PER_UNIT

    def scores(u):
        heads = range(u * A_HEADS_PER_UNIT, (u + 1) * A_HEADS_PER_UNIT)
        kh = heads[0] // A_GROUP
        q = jnp.concatenate([q_ref[0, :, h * HEAD_DIM:(h + 1) * HEAD_DIM] for h in heads], axis=0)
        k = k_ref[0, :, kh * HEAD_DIM:(kh + 1) * HEAD_DIM]
        return lax.dot_general(q, k, NT_DIMS, preferred_element_type=F32)

    outs = []
    v_ones = []
    for kh in range(A_KV_HEADS):
        v = v_ref[0, :, kh * HEAD_DIM:(kh + 1) * HEAD_DIM]
        v_ones.append(jnp.concatenate([v, jnp.ones_like(v)], axis=1))
    s = scores(0)
    for u in range(n_units):
        s_next = scores(u + 1) if u + 1 < n_units else None
        p = jnp.exp2(s - jnp.max(s, axis=-1, keepdims=True)).astype(BF16)
        o = jnp.dot(p, v_ones[u * A_HEADS_PER_UNIT // A_GROUP], preferred_element_type=F32)
        o = o[:, :HEAD_DIM] / o[:, HEAD_DIM:HEAD_DIM + 1]
        outs.extend(o[i * tq:(i + 1) * tq] for i in range(A_HEADS_PER_UNIT))
        s = s_next
    o_ref[0] = jnp.concatenate(outs, axis=1).astype(BF16)


def _attention_a(qa, ka, va, expert_weights, tq):
    b, s, _ = qa.shape
    steps = b * (s // tq)
    flat = [w.reshape(-1, w.shape[-1]) for w in expert_weights]
    wslice = lambda w: pl.BlockSpec((w.shape[0] // steps, w.shape[1]), lambda i, j: (i * (s // tq) + j, 0))
    o, *rounded = pl.pallas_call(
        _attn_a_kernel,
        grid=(b, s // tq),
        in_specs=[
            pl.BlockSpec((1, tq, A_Q_W), lambda i, j: (i, j, 0)),
            pl.BlockSpec((1, s, A_KV_W), lambda i, j: (i, 0, 0)),
            pl.BlockSpec((1, s, A_KV_W), lambda i, j: (i, 0, 0)),
            *map(wslice, flat),
        ],
        out_specs=[pl.BlockSpec((1, tq, A_Q_W), lambda i, j: (i, j, 0)), *map(wslice, flat)],
        out_shape=[jax.ShapeDtypeStruct((b, s, A_Q_W), BF16),
                   *[jax.ShapeDtypeStruct(w.shape, BF16) for w in flat]],
        compiler_params=_params(2),
        name="attention_a",
    )(qa, ka, va, *flat)
    return o, [r.reshape(w.shape) for r, w in zip(rounded, expert_weights)]


BAND_QB = 128
BAND_UNITS_PER_STEP = 16
BAND_STEPS = 1


def _band_bias(n):
    win = min(2 * BAND_QB, n)
    rows = B_HEADS_PER_GROUP * BAND_QB
    rel = (np.arange(rows)[:, None] % BAND_QB) - np.arange(win)[None, :]
    shifts = np.arange(2 * BAND_QB // BAND_HALF - 1) * BAND_HALF
    inside = np.abs(rel[None] + shifts[:, None, None]) <= BAND_HALF
    return jnp.asarray(np.where(inside, 0.0, MASK_VALUE).astype(np.float32))


def _dilated_kernel(q_ref, k_ref, v_ref, bias_ref, o_ref, lse_ref, *, n, dilation):
    win = min(2 * BAND_QB, n)
    n_blk = n // BAND_QB
    lane_head = lax.broadcasted_iota(jnp.int32, (1, B_GROUP_W), 1) // HEAD_DIM

    planes = B_GROUP_W // LANES

    def class_rows(r, start, size):
        if dilation == 1:
            return pl.ds(pl.multiple_of(start, BAND_HALF), size)
        return pl.ds(r + start * dilation, size, stride=dilation)

    def load(ref, rows_):
        return jnp.concatenate([ref[j, rows_, :] for j in range(planes)], axis=1)

    def scores(i):
        unit = pl.program_id(1) * BAND_UNITS_PER_STEP + i
        r = unit // n_blk
        qs = (unit % n_blk) * BAND_QB
        ks = jnp.clip(qs - BAND_HALF, 0, n - win)
        q = load(q_ref, class_rows(r, qs, BAND_QB))
        k = load(k_ref, class_rows(r, ks, win)).astype(BF16)
        q_stack = jnp.concatenate([jnp.where(lane_head == h, q, 0.0) for h in range(B_HEADS_PER_GROUP)], axis=0)
        s = lax.dot_general(q_stack.astype(BF16), k, NT_DIMS, preferred_element_type=F32)
        return r, qs, ks, s + bias_ref[(qs - ks) // BAND_HALF]

    nxt = scores(0)
    for i in range(BAND_UNITS_PER_STEP):
        r, qs, ks, s = nxt
        if i + 1 < BAND_UNITS_PER_STEP:
            nxt = scores(i + 1)
        v = load(v_ref, class_rows(r, ks, win)).astype(BF16)
        mx = jnp.max(s, axis=-1, keepdims=True)
        p = jnp.exp2(s - mx)
        denom = jnp.sum(p, axis=-1, keepdims=True)
        pv = jnp.dot(p.astype(BF16), v, preferred_element_type=F32)
        o = jnp.zeros((BAND_QB, B_GROUP_W), F32)
        d = jnp.ones((BAND_QB, B_GROUP_W), F32)
        l = jnp.zeros((BAND_QB, B_GROUP_W), F32)
        for h in range(B_HEADS_PER_GROUP):
            blk = slice(h * BAND_QB, (h + 1) * BAND_QB)
            o = jnp.where(lane_head == h, pv[blk], o)
            d = jnp.where(lane_head == h, denom[blk], d)
            l = jnp.where(lane_head == h, mx[blk], l)
        o = o / d
        l = l + jnp.log2(d)
        for j in range(planes):
            o_ref[j, class_rows(r, qs, BAND_QB), :] = o[:, j * LANES:(j + 1) * LANES]
            lse_ref[j, class_rows(r, qs, BAND_QB), :] = l[:, j * LANES:(j + 1) * LANES]


def _dilated_attention(q, k, v, seq, dilation):
    planes, tokens, _ = q.shape
    n = seq // dilation
    assert dilation * (n // BAND_QB) == BAND_UNITS_PER_STEP * BAND_STEPS
    blk = pl.BlockSpec((planes, seq, LANES), lambda i, j: (0, i, 0))
    bias = _band_bias(n)
    return pl.pallas_call(
        functools.partial(_dilated_kernel, n=n, dilation=dilation),
        grid=(tokens // seq, BAND_STEPS),
        in_specs=[blk, blk, blk, pl.BlockSpec(bias.shape, lambda i, j: (0, 0, 0))],
        out_specs=[blk, blk],
        out_shape=[jax.ShapeDtypeStruct(q.shape, F32)] * 2,
        compiler_params=_params(2),
        name=f"dilated_attention_d{dilation}",
    )(q, k, v, bias)


def _merge_kernel(x_ref, g0_ref, b0_ref, wg_ref, bg_ref, oa_ref, o0_ref, o1_ref, o2_ref, l0_ref, l1_ref, l2_ref,
                  wa_ref, wb_ref, wo_ref, g1_ref, b1_ref, wr_hi_ref, wr_lo_ref,
                  h1_ref, h1b_ref, logit_ref, z_ref):
    i = pl.program_id(0)

    @pl.when(i == 0)
    def _():
        z_ref[1] = jnp.zeros(z_ref.shape[1:], F32)

    planes = lambda ref: jnp.concatenate([ref[j] for j in range(ref.shape[0])], axis=1)
    ya = jnp.dot(oa_ref[...], wa_ref[...], preferred_element_type=F32)
    h0 = _layer_norm(x_ref[...], g0_ref[...], b0_ref[...])
    gate_t = jnp.tanh(jnp.dot(h0.astype(BF16), wg_ref[...], preferred_element_type=F32) + bg_ref[...])

    h1 = _layer_norm(z_ref[(i + 1) % 2], g1_ref[...], b1_ref[...])
    h1_ref[...] = h1
    h1b_ref[...] = h1.astype(BF16)
    hi, lo = _split_bf16(h1)
    logit_ref[0] = (lax.dot_general(wr_hi_ref[...], hi, NT_DIMS, preferred_element_type=F32)
                    + lax.dot_general(wr_hi_ref[...], lo, NT_DIMS, preferred_element_type=F32)
                    + lax.dot_general(wr_lo_ref[...], hi, NT_DIMS, preferred_element_type=F32))

    l0, l1, l2 = planes(l0_ref), planes(l1_ref), planes(l2_ref)
    mx = jnp.maximum(jnp.maximum(l0, l1), l2)
    e0, e1, e2 = jnp.exp2(l0 - mx), jnp.exp2(l1 - mx), jnp.exp2(l2 - mx)
    ob = (e0 * planes(o0_ref) + e1 * planes(o1_ref) + e2 * planes(o2_ref)) / (e0 + e1 + e2)
    yb = jnp.dot(ob.astype(BF16), wb_ref[...], preferred_element_type=F32)
    merged = (ya + yb) + (gate_t[:, :D_MODEL] * ya + gate_t[:, D_MODEL:] * yb)
    mix = jnp.dot(merged.astype(BF16), wo_ref[...], preferred_element_type=F32)
    z_ref[i % 2] = DEEPNORM_ALPHA * h0 + mix


def _merge(x2, ln0_g, ln0_b, w_gates, b_gate, oa, obs, lses, w_a, w_b, w_o, ln1_g, ln1_b, w_router, seq, tm):
    m = x2.shape[0]
    n_tiles = m // tm
    tiles_per_seq = seq // tm
    src = lambda i: jnp.minimum(i, n_tiles - 1)
    dst = lambda i: jnp.maximum(i - 1, 0)
    const = lambda i: (0, 0)
    wr_t = w_router.T
    wr_hi = wr_t.astype(BF16)
    wr_lo = (wr_t - wr_hi.astype(F32)).astype(BF16)
    vec = lambda v: v.reshape(1, -1)
    full = lambda a: pl.BlockSpec(a.shape, const)
    tile_in = lambda w: pl.BlockSpec((tm, w), lambda i: (src(i), 0))
    tile_out = lambda w: pl.BlockSpec((tm, w), lambda i: (dst(i), 0))
    plane_in = pl.BlockSpec((B_GROUP_W // LANES, tm, LANES), lambda i: (0, src(i), 0))
    args = [x2, vec(ln0_g), vec(ln0_b), w_gates, vec(b_gate), oa, *obs, *lses, w_a, w_b, w_o,
            vec(ln1_g), vec(ln1_b), wr_hi, wr_lo]
    in_specs = [tile_in(D_MODEL), full(args[1]), full(args[2]), full(w_gates), full(args[4]), tile_in(A_Q_W),
                *[plane_in] * 6, full(w_a), full(w_b), full(w_o), full(args[15]), full(args[16]),
                full(wr_hi), full(wr_lo)]
    return pl.pallas_call(
        _merge_kernel,
        grid=(n_tiles + 1,),
        in_specs=in_specs,
        out_specs=[tile_out(D_MODEL), tile_out(D_MODEL),
                   pl.BlockSpec((1, N_EXPERTS, tm),
                                lambda i: (dst(i) // tiles_per_seq, 0, dst(i) % tiles_per_seq))],
        out_shape=[jax.ShapeDtypeStruct((m, D_MODEL), F32), jax.ShapeDtypeStruct((m, D_MODEL), BF16),
                   jax.ShapeDtypeStruct((m // seq, N_EXPERTS, seq), F32)],
        scratch_shapes=[pltpu.VMEM((2, tm, D_MODEL), F32)],
        compiler_params=_params(1),
        name="merge_ln1_router",
    )(*args)


PREFIX_CHUNK = 256


def _prefix_count(mask, tri):
    ones = jnp.where(mask, 1.0, 0.0)
    carry = jnp.zeros((mask.shape[0], 1), F32)
    outs, bounds = [], [carry]
    for j in range(mask.shape[1] // PREFIX_CHUNK):
        chunk = ones[:, j * PREFIX_CHUNK:(j + 1) * PREFIX_CHUNK]
        outs.append(jnp.dot(chunk.astype(BF16), tri, preferred_element_type=F32) + carry)
        carry = carry + jnp.sum(chunk, axis=1, keepdims=True)
        bounds.append(carry)
    return jnp.concatenate(outs, axis=1), bounds


THRESHOLD_BITS = 31
THRESHOLD_REFINE = 16


def _route_kernel(logit_ref, slot_ref, aff_ref, starts_ref, *, cap):
    n_seq = logit_ref.shape[0]
    affs = []
    for b in range(n_seq):
        lg = logit_ref[b]
        ex = jnp.exp(lg - jnp.max(lg, axis=0, keepdims=True))
        affs.append(ex / jnp.sum(ex, axis=0, keepdims=True))
    aff = jnp.concatenate(affs, axis=0)

    def count_ge(t):
        return jnp.sum(jnp.where(aff >= t, 1.0, 0.0), axis=1, keepdims=True)

    thr = jnp.zeros((aff.shape[0], 1), jnp.int32)
    for bit in range(THRESHOLD_BITS - 1, -1, -1):
        cand = thr | (1 << bit)
        thr = jnp.where(count_ge(pltpu.bitcast(cand, F32)) >= cap, cand, thr)
    lo = pltpu.bitcast(thr, F32)
    hi = pltpu.bitcast(thr + 1, F32)
    for _ in range(THRESHOLD_REFINE):
        mid = 0.5 * (lo + hi)
        take = count_ge(mid) >= cap
        lo = jnp.where(take, mid, lo)
        hi = jnp.where(take, hi, mid)
    above = aff >= hi
    tied = (aff >= lo) & (aff < hi)
    r = lax.broadcasted_iota(jnp.int32, (PREFIX_CHUNK, PREFIX_CHUNK), 0)
    c = lax.broadcasted_iota(jnp.int32, (PREFIX_CHUNK, PREFIX_CHUNK), 1)
    tri = jnp.where(r < c, 1.0, 0.0).astype(BF16)
    need = cap - jnp.sum(jnp.where(above, 1.0, 0.0), axis=1, keepdims=True)
    sel = above | (tied & (_prefix_count(tied, tri)[0] < need))
    rank, bounds = _prefix_count(sel, tri)
    slot = jnp.where(sel, rank, -1.0).astype(jnp.int32)
    lane = lax.broadcasted_iota(jnp.int32, (1, LANES), 1)
    starts = jnp.zeros((aff.shape[0], LANES), F32)
    for j, count in enumerate(bounds):
        starts = jnp.where(lane == j, count, starts)
    for b in range(n_seq):
        rows = slice(b * N_EXPERTS, (b + 1) * N_EXPERTS)
        slot_ref[b] = slot[rows]
        aff_ref[b] = aff[rows]
        starts_ref[b] = starts[rows].astype(jnp.int32)


ROUTE_SEQS_PER_STEP = 8


def _route(logits, batch, seq, cap):
    nb = ROUTE_SEQS_PER_STEP if batch % ROUTE_SEQS_PER_STEP == 0 else 1
    blk = pl.BlockSpec((nb, N_EXPERTS, seq), lambda i: (i, 0, 0))
    return pl.pallas_call(
        functools.partial(_route_kernel, cap=cap),
        grid=(batch // nb,),
        in_specs=[blk],
        out_specs=[blk, blk, pl.BlockSpec((nb, N_EXPERTS, LANES), lambda i: (i, 0, 0))],
        out_shape=[jax.ShapeDtypeStruct((batch, N_EXPERTS, seq), jnp.int32),
                   jax.ShapeDtypeStruct((batch, N_EXPERTS, seq), F32),
                   jax.ShapeDtypeStruct((batch, N_EXPERTS, LANES), jnp.int32)],
        compiler_params=_params(1),
        name="route",
    )(logits)


BF16_SUBLANES = 16
DISPATCH_WIN = 64
DISPATCH_CHUNKS_PER_STEP = 8
COMBINE_CHUNKS_PER_STEP = 4


def _slot_windows(starts_ref, b, j, n_bounds, win):
    base, n_win = [], 0
    for e in range(N_EXPERTS):
        at = (b * N_EXPERTS + e) * n_bounds + j
        first = (starts_ref[at] // BF16_SUBLANES) * BF16_SUBLANES
        base.append(first)
        n_win = jnp.maximum(n_win, (starts_ref[at + 1] - first + win - 1) // win)
    return base, n_win


def _window_rows(base_e, k, wrow, cap, win):
    lower = base_e + k * win
    off = pl.multiple_of(jnp.minimum(lower, cap - win), BF16_SUBLANES)
    return off, jnp.where(wrow + off >= lower, wrow + off, -2)


def _dispatch_kernel(starts_ref, h_ref, slot_ref, aff_ref, xe_ref, gate_ref, *, cap):
    b, step = pl.program_id(0), pl.program_id(1)
    win = DISPATCH_WIN
    chunks = h_ref.shape[1] // PREFIX_CHUNK
    n_bounds = pl.num_programs(1) * chunks + 1

    @pl.when(step == 0)
    def _():
        xe_ref[...] = jnp.zeros_like(xe_ref)
        gate_ref[...] = jnp.zeros_like(gate_ref)

    wrow = lax.broadcasted_iota(jnp.int32, (win, PREFIX_CHUNK), 0)
    for c in range(chunks):
        tokens = slice(c * PREFIX_CHUNK, (c + 1) * PREFIX_CHUNK)
        base, n_win = _slot_windows(starts_ref, b, step * chunks + c, n_bounds, win)
        h = h_ref[0, tokens, :]

        def window(k):
            offs, onehots = [], []
            for e in range(N_EXPERTS):
                off, target = _window_rows(base[e], k, wrow, cap, win)
                offs.append(off)
                onehots.append(slot_ref[0, e:e + 1, tokens] == target)
            sel = jnp.concatenate([jnp.where(o, 1.0, 0.0) for o in onehots], axis=0).astype(BF16)
            rows = jnp.dot(sel, h, preferred_element_type=F32).astype(BF16)
            for e in range(N_EXPERTS):
                at = pl.ds(offs[e], win)
                xe_ref[e, at, :] += rows[e * win:(e + 1) * win]
                gate = jnp.sum(jnp.where(onehots[e], aff_ref[0, e:e + 1, tokens], 0.0), axis=1, keepdims=True)
                gate_ref[e, at, :] += jnp.broadcast_to(gate, (win, LANES))

        window(0)
        lax.fori_loop(1, n_win, lambda k, carry: (window(k), carry)[1], 0)


def _dispatch(h1b, slot, aff, starts, cap):
    b, s, _ = h1b.shape
    n_chunks = s // PREFIX_CHUNK
    span = DISPATCH_CHUNKS_PER_STEP * PREFIX_CHUNK
    chunk = pl.BlockSpec((1, N_EXPERTS, span), lambda i, j, st: (i, 0, j))
    grid_spec = pltpu.PrefetchScalarGridSpec(
        num_scalar_prefetch=1,
        grid=(b, s // span),
        in_specs=[pl.BlockSpec((1, span, D_MODEL), lambda i, j, st: (i, j, 0)), chunk, chunk],
        out_specs=[pl.BlockSpec((N_EXPERTS, cap, D_MODEL), lambda i, j, st: (0, i, 0)),
                   pl.BlockSpec((N_EXPERTS, cap, LANES), lambda i, j, st: (0, i, 0))],
    )
    return pl.pallas_call(
        functools.partial(_dispatch_kernel, cap=cap),
        grid_spec=grid_spec,
        out_shape=[jax.ShapeDtypeStruct((N_EXPERTS, b * cap, D_MODEL), BF16),
                   jax.ShapeDtypeStruct((N_EXPERTS, b * cap, LANES), F32)],
        compiler_params=_params(2),
        name="dispatch",
    )(starts[:, :, :n_chunks + 1].reshape(-1), h1b, slot, aff)


FF_CHUNK = 1024


def _expert_kernel(x_ref, gate_ref, wg_ref, wu_ref, wd_ref, y_ref):
    x = x_ref[0]
    acc = jnp.zeros((x.shape[0], D_MODEL), F32)
    for c in range(D_FF // FF_CHUNK):
        cols = slice(c * FF_CHUNK, (c + 1) * FF_CHUNK)
        g = jnp.dot(x, wg_ref[0, :, cols], preferred_element_type=F32)
        u = jnp.dot(x, wu_ref[0, :, cols], preferred_element_type=F32)
        act = (g * jax.nn.sigmoid(g) * u).astype(BF16)
        acc = acc + jnp.dot(act, wd_ref[0, cols, :], preferred_element_type=F32)
    y_ref[0] = (acc * gate_ref[0][:, :1]).astype(BF16)


def _expert_ffn(xe, gate, w_gate, w_up, w_down, tm):
    e, rows, _ = xe.shape
    tok = lambda w: pl.BlockSpec((1, tm, w), lambda i, j: (i, j, 0))
    wspec = lambda a: pl.BlockSpec((1,) + a.shape[1:], lambda i, j: (i, 0, 0))
    return pl.pallas_call(
        _expert_kernel,
        grid=(e, rows // tm),
        in_specs=[tok(D_MODEL), tok(LANES), wspec(w_gate), wspec(w_up), wspec(w_down)],
        out_specs=tok(D_MODEL),
        out_shape=jax.ShapeDtypeStruct((e, rows, D_MODEL), BF16),
        compiler_params=_params(2),
        name="expert_ffn",
    )(xe, gate, w_gate, w_up, w_down)


COMBINE_WIN = 64
TN_DIMS = (((0,), (0,)), ((), ()))


def _combine_kernel(starts_ref, h1_ref, slot_ref, y_ref, g2_ref, b2_ref, out_ref, acc_ref, *, cap, n_chunks):
    s = pl.program_id(0)
    cur = jnp.minimum(s, pl.num_programs(0) - 2)
    steps_per_seq = n_chunks // COMBINE_CHUNKS_PER_STEP
    win = COMBINE_WIN

    @pl.when(s == 0)
    def _():
        acc_ref[1] = jnp.zeros(acc_ref.shape[1:], F32)

    out_ref[0] = _layer_norm(DEEPNORM_ALPHA * h1_ref[0] + acc_ref[(s + 1) % 2], g2_ref[...], b2_ref[...])

    wrow = lax.broadcasted_iota(jnp.int32, (win, PREFIX_CHUNK), 0)
    for c in range(COMBINE_CHUNKS_PER_STEP):
        tokens = slice(c * PREFIX_CHUNK, (c + 1) * PREFIX_CHUNK)
        chunk = (cur % steps_per_seq) * COMBINE_CHUNKS_PER_STEP + c
        base, n_win = _slot_windows(starts_ref, cur // steps_per_seq, chunk, n_chunks + 1, win)

        def gather(k):
            sels, rows = [], []
            for e in range(N_EXPERTS):
                off, target = _window_rows(base[e], k, wrow, cap, win)
                sels.append(jnp.where(slot_ref[0, e:e + 1, tokens] == target, 1.0, 0.0))
                rows.append(y_ref[e, pl.ds(off, win), :])
            sel = jnp.concatenate(sels, axis=0).astype(BF16)
            return lax.dot_general(sel, jnp.concatenate(rows, axis=0), TN_DIMS, preferred_element_type=F32)

        acc_ref[s % 2, tokens, :] = gather(0)

        def more(k, carry):
            acc_ref[s % 2, tokens, :] += gather(k)
            return carry

        lax.fori_loop(1, n_win, more, 0)


def _combine(h1, slot, starts, y, ln2_g, ln2_b, cap):
    b, s, _ = h1.shape
    n_chunks = s // PREFIX_CHUNK
    span = COMBINE_CHUNKS_PER_STEP * PREFIX_CHUNK
    per_seq = s // span
    n_steps = b * per_seq
    cur = lambda i: jnp.minimum(i, n_steps - 1)
    prev = lambda i: jnp.maximum(i - 1, 0)
    const = lambda i, st: (0, 0)
    grid_spec = pltpu.PrefetchScalarGridSpec(
        num_scalar_prefetch=1,
        grid=(n_steps + 1,),
        in_specs=[pl.BlockSpec((1, span, D_MODEL), lambda i, st: (prev(i) // per_seq, prev(i) % per_seq, 0)),
                  pl.BlockSpec((1, N_EXPERTS, span), lambda i, st: (cur(i) // per_seq, 0, cur(i) % per_seq)),
                  pl.BlockSpec((N_EXPERTS, cap, D_MODEL), lambda i, st: (0, cur(i) // per_seq, 0)),
                  pl.BlockSpec((1, D_MODEL), const), pl.BlockSpec((1, D_MODEL), const)],
        out_specs=pl.BlockSpec((1, span, D_MODEL), lambda i, st: (prev(i) // per_seq, prev(i) % per_seq, 0)),
        scratch_shapes=[pltpu.VMEM((2, span, D_MODEL), F32)],
    )
    return pl.pallas_call(
        functools.partial(_combine_kernel, cap=cap, n_chunks=n_chunks),
        grid_spec=grid_spec,
        out_shape=jax.ShapeDtypeStruct((b, s, D_MODEL), F32),
        compiler_params=_params(1),
        name="combine_ln2",
    )(starts[:, :, :n_chunks + 1].reshape(-1), h1, slot, y, ln2_g.reshape(1, D_MODEL), ln2_b.reshape(1, D_MODEL))


def kernel(x, ln0_g, ln0_b, w_in, b_gate, qn_g, kn_g, w_branch_a, w_branch_b, w_out, ln1_g, ln1_b, w_router,
           w_gate_e, w_up_e, w_down_e, ln2_g, ln2_b):
    batch, seq, _ = x.shape
    assert w_in.shape[0] == DEPTH
    cap = EC_CAPACITY_FACTOR * seq // N_EXPERTS
    tokens = batch * seq
    x2 = x.reshape(tokens, D_MODEL)
    w_qkv = w_in[0, :, :QKV_W].astype(BF16)
    w_gates = (0.5 * w_in[0, :, QKV_W:]).astype(BF16)

    qkv = _in_projection(x2, ln0_g, ln0_b, w_qkv, qn_g[0], kn_g[0], seq, tm=IN_PROJ_ROW_TILE)
    qa, ka, va = qkv[:3]
    qb, kb, vb = qkv[3:6], qkv[6:9], qkv[9:12]

    oa, (wg_b, wu_b, wd_b) = _attention_a(qa.reshape(batch, seq, A_Q_W), ka.reshape(batch, seq, A_KV_W),
                                          va.reshape(batch, seq, A_KV_W), (w_gate_e[0], w_up_e[0], w_down_e[0]),
                                          tq=ROW_TILE)
    oa = oa.reshape(tokens, A_Q_W)

    obs, lses = [], []
    for gi, (window, dilation) in enumerate(B_GROUPS):
        assert window // (2 * dilation) == BAND_HALF
        o, lse = _dilated_attention(qb[gi], kb[gi], vb[gi], seq, dilation)
        obs.append(o)
        lses.append(lse)

    h1, h1b, logits = _merge(x2, ln0_g, ln0_b, w_gates, 0.5 * b_gate[0], oa, obs, lses,
                             w_branch_a[0].astype(BF16), w_branch_b[0].astype(BF16), (0.5 * w_out[0]).astype(BF16),
                             ln1_g[0], ln1_b[0], w_router[0], seq, tm=ROW_TILE)

    slot, aff, starts = _route(logits, batch, seq, cap)
    xe, gate = _dispatch(h1b.reshape(batch, seq, D_MODEL), slot, aff, starts, cap)
    y = _expert_ffn(xe, gate, wg_b, wu_b, wd_b, tm=min(FFN_ROW_TILE, batch * cap))
    return _combine(h1.reshape(batch, seq, D_MODEL), slot, starts, y, ln2_g[0], ln2_b[0], cap)
```
